```python
import math
import jax
import jax.numpy as jnp
from jax import lax
import numpy as np


D_MODEL = 2048
BATCH = 4
SEQ = 4096
DEPTH = 2

N_A_LAYERS = DEPTH // 2
N_B_LAYERS = DEPTH - N_A_LAYERS
HEAD_DIM = 128
N_HEADS = D_MODEL // HEAD_DIM
HGRN_CHUNK = 64
MOBA_BLOCK = 256
MOBA_TOP_K = 3
MOBA_Q_CHUNK = 16
ROPE_THETA = 10000.0
N_GROUPS = 4
EXPERTS_PER_GROUP = 8
N_EXPERTS = N_GROUPS * EXPERTS_PER_GROUP
TOP_K_IN_GROUP = 2
EXPERT_FF = D_MODEL // 4
EXPERT_ROW_BLOCK = 128
RMS_EPS = 1e-6

kernel_name = 'hgrn2_moba_yoco_hmoe'


def rms_norm(x, gain):
    xf = x.astype(jnp.float32)
    y = xf * lax.rsqrt(jnp.mean(xf * xf, axis=-1, keepdims=True) + RMS_EPS)
    return (y * gain.astype(jnp.float32)).astype(x.dtype)


def apply_rope(x):
    T, dh = x.shape[1], x.shape[-1]
    half = dh // 2
    inv_freq = ROPE_THETA ** (-jnp.arange(0, dh, 2, dtype=jnp.float32) / dh)
    ang = jnp.arange(T, dtype=jnp.float32)[:, None] * inv_freq[None, :]
    cos = jnp.cos(ang)[None, :, None, :]
    sin = jnp.sin(ang)[None, :, None, :]
    xf = x.astype(jnp.float32)
    x1, x2 = xf[..., :half], xf[..., half:]
    out = jnp.concatenate([x1 * cos - x2 * sin, x2 * cos + x1 * sin], axis=-1)
    return out.astype(x.dtype)


def _hgrn2_chunk_step(S, xs):
    q, k, v, g = xs
    C = q.shape[2]
    G = jnp.cumsum(g, axis=2)
    o_inter = jnp.einsum('bhtk,bhkv->bhtv', q * jnp.exp(G), S)
    causal = jnp.tril(jnp.ones((C, C), dtype=bool))
    diff = G[:, :, :, None, :] - G[:, :, None, :, :]
    decay = jnp.where(causal[None, None, :, :, None], jnp.exp(jnp.minimum(diff, 0.0)), 0.0)
    scores = jnp.einsum('bhtk,bhsk,bhtsk->bhts', q, k, decay)
    o_intra = jnp.einsum('bhts,bhsv->bhtv', scores, v)
    G_end = G[:, :, -1, :]
    S_new = jnp.exp(G_end)[..., None] * S + jnp.einsum(
        'bhsk,bhsv->bhkv', k * jnp.exp(G_end[:, :, None, :] - G), v)
    return S_new, o_inter + o_intra


def hgrn2_mixer(h, w_in, lower_bound, o_gain, w_out):
    B, T, D = h.shape
    C = HGRN_CHUNK
    z = jnp.einsum('btd,de->bte', h, w_in)
    zq, zf, zi, zg = jnp.split(z, 4, axis=-1)
    lb = lower_bound.astype(jnp.float32)
    f = lb + (1.0 - lb) * jax.nn.sigmoid(zf.astype(jnp.float32))

    def heads(t):
        t = t.astype(jnp.float32).reshape(B, T // C, C, N_HEADS, HEAD_DIM)
        return t.transpose(1, 0, 3, 2, 4)

    q, k, v, g = heads(zq), heads(1.0 - f), heads(zi), heads(jnp.log(f))
    S0 = jnp.zeros((B, N_HEADS, HEAD_DIM, HEAD_DIM), jnp.float32)
    _, o = lax.scan(_hgrn2_chunk_step, S0, (q, k, v, g))
    o = o.transpose(1, 0, 3, 2, 4).reshape(B, T, N_HEADS, HEAD_DIM)
    o = rms_norm(o, o_gain) * jax.nn.silu(zg.astype(jnp.float32)).reshape(B, T, N_HEADS, HEAD_DIM)
    o = o.reshape(B, T, D).astype(h.dtype)
    return jnp.einsum('btd,de->bte', o, w_out)


def shared_kv(x, kv_norm, kv_w, k_norm):
    B, T, D = x.shape
    h = rms_norm(x, kv_norm)
    kv = jnp.einsum('btd,de->bte', h, kv_w)
    k, v = jnp.split(kv, 2, axis=-1)
    k = apply_rope(rms_norm(k.reshape(B, T, N_HEADS, HEAD_DIM), k_norm))
    v = v.reshape(B, T, N_HEADS, HEAD_DIM)
    Tp = ((T + MOBA_BLOCK - 1) // MOBA_BLOCK) * MOBA_BLOCK
    pad = ((0, 0), (0, 0), (0, Tp - T), (0, 0))
    k = jnp.pad(k.transpose(0, 2, 1, 3), pad)
    v = jnp.pad(v.transpose(0, 2, 1, 3), pad)
    k_mean = jnp.mean(k.reshape(B, N_HEADS, Tp // MOBA_BLOCK, MOBA_BLOCK, HEAD_DIM).astype(jnp.float32),
                      axis=3).astype(k.dtype)
    return k, v, k_mean


def moba_attention(q, k, v, k_mean):
    B, H, Tp, dh = q.shape
    NB = Tp // MOBA_BLOCK
    QC = MOBA_Q_CHUNK
    nq = Tp // QC
    kk = min(MOBA_TOP_K, NB)
    scale = HEAD_DIM ** -0.5
    q_blk = jnp.arange(Tp) // MOBA_BLOCK
    gate = jnp.einsum('bhtd,bhnd->bhtn', q, k_mean).astype(jnp.float32)
    fully_past = jnp.arange(NB)[None, :] < q_blk[:, None]
    gate = jnp.where(fully_past[None, None], gate, -jnp.inf)
    _, sel = lax.top_k(gate, kk)
    valid = sel < q_blk[None, None, :, None]
    kb = k.reshape(B, H, NB, MOBA_BLOCK, dh)
    vb = v.reshape(B, H, NB, MOBA_BLOCK, dh)
    b_idx = jnp.arange(B)[:, None, None, None]
    h_idx = jnp.arange(H)[None, :, None, None]

    def to_chunks(t):
        return t.reshape((B, H, nq, QC) + t.shape[3:]).transpose((2, 0, 1, 3) + tuple(range(4, t.ndim + 1)))

    def attend_chunk(args):
        qc, selc, validc, ci = args
        q0 = ci * QC
        blk = q0 // MOBA_BLOCK
        q_pos = q0 + jnp.arange(QC)
        k_own = lax.dynamic_index_in_dim(kb, blk, axis=2, keepdims=False)
        v_own = lax.dynamic_index_in_dim(vb, blk, axis=2, keepdims=False)
        key_pos = blk * MOBA_BLOCK + jnp.arange(MOBA_BLOCK)
        s_own = jnp.einsum('bhqd,bhjd->bhqj', qc, k_own).astype(jnp.float32) * scale
        s_own = jnp.where(key_pos[None, :] <= q_pos[:, None], s_own, -jnp.inf)
        k_sel = kb[b_idx, h_idx, selc]
        v_sel = vb[b_idx, h_idx, selc]
        s_sel = jnp.einsum('bhqd,bhqnjd->bhqnj', qc, k_sel).astype(jnp.float32) * scale
        s_sel = jnp.where(validc[..., None], s_sel, -jnp.inf)
        s = jnp.concatenate([s_own, s_sel.reshape(B, H, QC, kk * MOBA_BLOCK)], axis=-1)
        p = jax.nn.softmax(s, axis=-1).astype(qc.dtype)
        p_own = p[..., :MOBA_BLOCK]
        p_sel = p[..., MOBA_BLOCK:].reshape(B, H, QC, kk, MOBA_BLOCK)
        return (jnp.einsum('bhqj,bhjd->bhqd', p_own, v_own)
                + jnp.einsum('bhqnj,bhqnjd->bhqd', p_sel, v_sel))

    o = lax.map(attend_chunk, (to_chunks(q), to_chunks(sel), to_chunks(valid),
                               jnp.arange(nq, dtype=jnp.int32)))
    return o.transpose(1, 2, 0, 3, 4).reshape(B, H, Tp, dh)


def moba_mixer(h, w_q, q_norm, w_out, k, v, k_mean):
    B, T, D = h.shape
    Tp = k.shape[2]
    q = jnp.einsum('btd,de->bte', h, w_q).reshape(B, T, N_HEADS, HEAD_DIM)
    q = apply_rope(rms_norm(q, q_norm)).transpose(0, 2, 1, 3)
    q = jnp.pad(q, ((0, 0), (0, 0), (0, Tp - T), (0, 0)))
    o = moba_attention(q, k, v, k_mean)[:, :, :T]
    o = o.transpose(0, 2, 1, 3).reshape(B, T, D)
    return jnp.einsum('btd,de->bte', o, w_out)


def routed_experts(xf, expert_id, weight, w_gate, w_up, w_down):
    N, D = xf.shape
    E = w_gate.shape[0]
    R = EXPERT_ROW_BLOCK
    K = expert_id.shape[1]
    A = N * K
    flat_e = expert_id.reshape(-1)
    flat_w = weight.reshape(-1)
    order = jnp.argsort(flat_e)
    e_sorted = flat_e[order]
    tok = order // K
    counts = jnp.bincount(flat_e, length=E)
    padded = (counts + R - 1) // R * R
    pad_end = jnp.cumsum(padded)
    pad_start = pad_end - padded
    start = jnp.cumsum(counts) - counts
    dest = pad_start[e_sorted] + jnp.arange(A) - start[e_sorted]
    P = ((A + E * (R - 1) + R - 1) // R) * R
    n_blk = P // R
    rows = jnp.zeros((P, D), xf.dtype).at[dest].set(xf[tok])
    blk_e = jnp.minimum(jnp.searchsorted(pad_end, jnp.arange(n_blk) * R, side='right'), E - 1)

    def expert_block(args):
        xb, e = args
        return (jax.nn.silu(xb @ w_gate[e]) * (xb @ w_up[e])) @ w_down[e]

    out = lax.map(expert_block, (rows.reshape(n_blk, R, D), blk_e)).reshape(P, D)
    y = jnp.zeros((N, D), jnp.float32).at[tok].add(
        flat_w[order][:, None] * out[dest].astype(jnp.float32))
    return y.astype(xf.dtype)


def hier_moe(h, w_group, b_group, w_expert, b_expert, w_gate, w_up, w_down):
    B, T, D = h.shape
    xf = h.reshape(B * T, D)
    g_logits = (xf @ w_group + b_group).astype(jnp.float32)
    g_top_p, g_top = lax.top_k(jax.nn.softmax(g_logits, axis=-1), 1)
    e_logits = (xf @ w_expert + b_expert).astype(jnp.float32).reshape(-1, N_GROUPS, EXPERTS_PER_GROUP)
    e_logits = jnp.take_along_axis(e_logits, g_top[:, :, None], axis=1)[:, 0]
    e_top_p, e_top = lax.top_k(jax.nn.softmax(e_logits, axis=-1), TOP_K_IN_GROUP)
    weight = g_top_p * e_top_p / jnp.sum(e_top_p, axis=-1, keepdims=True)
    expert_id = g_top * EXPERTS_PER_GROUP + e_top
    y = routed_experts(xf, expert_id, weight, w_gate, w_up, w_down)
    return y.reshape(B, T, D)


def setup_inputs(seed: int = 0) -> dict:
    key = jax.random.key(seed)
    ks = jax.random.split(key, 21)
    f32 = jnp.float32
    D = D_MODEL

    def w(k, shape, fan_in):
        return jax.random.normal(k, shape, f32) * (fan_in ** -0.5)

    def gain(k, shape):
        return 1.0 + 0.05 * jax.random.normal(k, shape, f32)

    return {
        'x': jax.random.normal(ks[0], (BATCH, SEQ, D), f32),
        'norm_mix': gain(ks[1], (DEPTH, D)),
        'norm_ffn': gain(ks[2], (DEPTH, D)),
        'a_w_in': w(ks[3], (N_A_LAYERS, D, 4 * D), D),
        'a_lb_logits': 0.5 * jax.random.normal(ks[4], (DEPTH + 1, D), f32),
        'a_o_gain': gain(ks[5], (N_A_LAYERS, HEAD_DIM)),
        'a_w_out': w(ks[6], (N_A_LAYERS, D, D), D),
        'kv_norm': gain(ks[7], (D,)),
        'kv_w': w(ks[8], (D, 2 * D), D),
        'k_norm': gain(ks[9], (HEAD_DIM,)),
        'b_w_q': w(ks[10], (N_B_LAYERS, D, D), D),
        'b_q_norm': gain(ks[11], (N_B_LAYERS, HEAD_DIM)),
        'b_w_out': w(ks[12], (N_B_LAYERS, D, D), D),
        'moe_w_group': w(ks[13], (DEPTH, D, N_GROUPS), D),
        'moe_b_group': 0.01 * jax.random.normal(ks[14], (DEPTH, N_GROUPS), f32),
        'moe_w_expert': w(ks[15], (DEPTH, D, N_EXPERTS), D),
        'moe_b_expert': 0.01 * jax.random.normal(ks[16], (DEPTH, N_EXPERTS), f32),
        'moe_w_gate': w(ks[17], (DEPTH, N_EXPERTS, D, EXPERT_FF), D),
        'moe_w_up': w(ks[18], (DEPTH, N_EXPERTS, D, EXPERT_FF), D),
        'moe_w_down': w(ks[19], (DEPTH, N_EXPERTS, EXPERT_FF, D), EXPERT_FF),
    }


def reference(x, norm_mix, norm_ffn, a_w_in, a_lb_logits, a_o_gain, a_w_out,
              kv_norm, kv_w, k_norm, b_w_q, b_q_norm, b_w_out,
              moe_w_group, moe_b_group, moe_w_expert, moe_b_expert,
              moe_w_gate, moe_w_up, moe_w_down):
    lower_bounds = jnp.cumsum(jax.nn.softmax(a_lb_logits.astype(jnp.float32), axis=0), axis=0)
    k_sh = v_sh = km_sh = None
    for l in range(DEPTH):
        if l < N_A_LAYERS:
            h = rms_norm(x, norm_mix[l])
            x = x + hgrn2_mixer(h, a_w_in[l], lower_bounds[l], a_o_gain[l], a_w_out[l])
        else:
            if l == N_A_LAYERS:
                k_sh, v_sh, km_sh = shared_kv(x, kv_norm, kv_w, k_norm)
            b = l - N_A_LAYERS
            h = rms_norm(x, norm_mix[l])
            x = x + moba_mixer(h, b_w_q[b], b_q_norm[b], b_w_out[b], k_sh, v_sh, km_sh)
        h = rms_norm(x, norm_ffn[l])
        x = x + hier_moe(h, moe_w_group[l], moe_b_group[l], moe_w_expert[l], moe_b_expert[l],
                         moe_w_gate[l], moe_w_up[l], moe_w_down[l])
    return x
```

```python
import functools

import jax
import jax.numpy as jnp
from jax import lax
from jax.experimental import pallas as pl
from jax.experimental.pallas import tpu as pltpu

HEAD_DIM = 128
HGRN_CHUNK = 64
HGRN_SUB = 16
MOBA_BLOCK = 256
MOBA_TOP_K = 3
ROPE_THETA = 10000.0
N_GROUPS = 4
EXPERTS_PER_GROUP = 8
N_EXPERTS = N_GROUPS * EXPERTS_PER_GROUP
TOP_K_IN_GROUP = 2
RMS_EPS = 1e-6
ROUTER_LANES = 128
EXPERT_ROWS = 256

V7X_VMEM_LIMIT = 56 * 1024 * 1024

F32 = jnp.float32
BF16 = jnp.bfloat16
HIGHEST = lax.Precision.HIGHEST
NT_DIMS = (((1,), (1,)), ((), ()))


def _params(*sem):
    return pltpu.CompilerParams(dimension_semantics=sem, vmem_limit_bytes=V7X_VMEM_LIMIT)


def _rms_rows_to(x_ref, g_ref, out_ref, rows, chunk):
    gain = g_ref[...]

    def body(c, carry):
        sl = pl.ds(pl.multiple_of(c * chunk, chunk), chunk)
        x = x_ref[sl, :]
        ms = jnp.mean(x * x, axis=-1, keepdims=True)
        out_ref[sl, :] = (x * lax.rsqrt(ms + RMS_EPS) * gain).astype(out_ref.dtype)
        return carry

    lax.fori_loop(0, rows // chunk, body, 0)


def _head_norm_rope(y, hg, cos, sin):
    ms = jnp.mean(y * y, axis=-1, keepdims=True)
    yn = y * lax.rsqrt(ms + RMS_EPS) * hg
    return yn * cos + pltpu.roll(yn, HEAD_DIM // 2, axis=1) * sin


def _rms_matmul_kernel(*refs, tm, tn, n_rope_tiles):
    if n_rope_tiles:
        x_ref, g_ref, w_ref, hg_ref, cos_ref, sin_ref, o_ref, hn_ref = refs
    else:
        x_ref, g_ref, w_ref, o_ref, hn_ref = refs
    j = pl.program_id(1)

    @pl.when(j == 0)
    def _():
        _rms_rows_to(x_ref, g_ref, hn_ref, tm, 64)

    y = jnp.dot(hn_ref[...], w_ref[...], preferred_element_type=F32)

    if not n_rope_tiles:
        o_ref[...] = y.astype(o_ref.dtype)
        return

    @pl.when(j < n_rope_tiles)
    def _():
        hg = hg_ref[...]
        cos = cos_ref[...]
        sin = sin_ref[...]
        for h in range(tn // HEAD_DIM):
            sl = slice(h * HEAD_DIM, (h + 1) * HEAD_DIM)
            o_ref[:, sl] = _head_norm_rope(y[:, sl], hg, cos, sin).astype(o_ref.dtype)

    @pl.when(j >= n_rope_tiles)
    def _():
        o_ref[...] = y.astype(o_ref.dtype)


def rms_matmul(x, gain, w, out_dtype, *, rope=None, seq_len=None, tm=1024, tn=512):
    n, d = x.shape
    e = w.shape[1]
    tm = min(tm, n)
    tn = min(tn, e)
    in_specs = [
        pl.BlockSpec((tm, d), lambda i, j: (i, 0)),
        pl.BlockSpec((1, d), lambda i, j: (0, 0)),
        pl.BlockSpec((d, tn), lambda i, j: (0, j)),
    ]
    args = [x, gain.reshape(1, d), w]
    n_rope_tiles = 0
    if rope is not None:
        hg, cos, sin, n_rope_cols = rope
        tm = min(tm, seq_len)
        tn = min(tn, n_rope_cols)
        in_specs[0] = pl.BlockSpec((tm, d), lambda i, j: (i, 0))
        in_specs[2] = pl.BlockSpec((d, tn), lambda i, j: (0, j))
        n_rope_tiles = n_rope_cols // tn
        pos_tiles = seq_len // tm
        in_specs += [
            pl.BlockSpec((1, HEAD_DIM), lambda i, j: (0, 0)),
            pl.BlockSpec((tm, HEAD_DIM), lambda i, j: (i % pos_tiles, 0)),
            pl.BlockSpec((tm, HEAD_DIM), lambda i, j: (i % pos_tiles, 0)),
        ]
        args += [hg.reshape(1, HEAD_DIM), cos, sin]
    return pl.pallas_call(
        functools.partial(_rms_matmul_kernel, tm=tm, tn=tn, n_rope_tiles=n_rope_tiles),
        grid=(n // tm, e // tn),
        in_specs=in_specs,
        out_specs=pl.BlockSpec((tm, tn), lambda i, j: (i, j)),
        out_shape=jax.ShapeDtypeStruct((n, e), out_dtype),
        scratch_shapes=[pltpu.VMEM((tm, d), BF16)],
        compiler_params=_params("arbitrary", "arbitrary"),
        name="rms_matmul",
    )(*args)


def _matmul_res_kernel(a_ref, w_ref, r_ref, o_ref):
    o_ref[...] = r_ref[...] + jnp.dot(a_ref[...], w_ref[...], preferred_element_type=F32)


def matmul_residual(a, w, res, *, tm=1024, tn=512):
    n, d = a.shape
    e = w.shape[1]
    tm = min(tm, n)
    tn = min(tn, e)
    return pl.pallas_call(
        _matmul_res_kernel,
        grid=(n // tm, e // tn),
        in_specs=[
            pl.BlockSpec((tm, d), lambda i, j: (i, 0)),
            pl.BlockSpec((d, tn), lambda i, j: (0, j)),
            pl.BlockSpec((tm, tn), lambda i, j: (i, j)),
        ],
        out_specs=pl.BlockSpec((tm, tn), lambda i, j: (i, j)),
        out_shape=jax.ShapeDtypeStruct((n, e), F32),
        compiler_params=_params("arbitrary", "arbitrary"),
        name="matmul_residual",
    )(a, w, res)


def _hgrn_kernel(zq_ref, zf_ref, zi_ref, zg_ref, lb_ref, og_ref, o_ref,
                 st_ref, g_scr, k_scr, *, n_chunks):
    C = HGRN_CHUNK
    SB = HGRN_SUB

    @pl.when(pl.program_id(2) == 0)
    def _():
        st_ref[...] = jnp.zeros_like(st_ref)

    lb = lb_ref[...]
    og = og_ref[...]
    row_c = lax.broadcasted_iota(jnp.int32, (C, C), 0)
    col_c = lax.broadcasted_iota(jnp.int32, (C, C), 1)
    tril = (row_c >= col_c).astype(F32)
    row_k = lax.broadcasted_iota(jnp.int32, (C, HEAD_DIM), 0)
    row_s = lax.broadcasted_iota(jnp.int32, (SB, 1), 0)
    lane_s = lax.broadcasted_iota(jnp.int32, (SB, C), 1)

    for c in range(n_chunks):
        sl = slice(c * C, (c + 1) * C)
        q = zq_ref[0, sl, :]
        v = zi_ref[0, sl, :]
        f = lb + (1.0 - lb) * jax.nn.sigmoid(zf_ref[0, sl, :])
        kk = 1.0 - f
        g = jnp.log(f)
        G = jnp.dot(tril, g, precision=HIGHEST, preferred_element_type=F32)
        g_scr[...] = G
        k_scr[...] = kk
        G_end = G[C - 1:C, :]

        st = st_ref[...]
        o_inter = lax.dot_general((q * jnp.exp(G)).astype(BF16), st.astype(BF16), NT_DIMS,
                                  preferred_element_type=F32)

        a_rows = []
        for i in range(C // SB):
            t0 = i * SB
            Gt = G[t0:t0 + SB, :]
            qt = q[t0:t0 + SB, :]
            a_i = jnp.zeros((SB, C), F32)
            for sl_ in range(SB):
                s = t0 + sl_
                Gs = g_scr[s:s + 1, :]
                ks = k_scr[s:s + 1, :]
                e = jnp.exp(jnp.minimum(Gt - Gs, 0.0))
                col = jnp.sum(qt * ks * e, axis=-1, keepdims=True)
                col = jnp.where(row_s >= sl_, col, 0.0)
                a_i = jnp.where(lane_s == s, col, a_i)
            if i > 0:
                Gr = g_scr[t0 - 1:t0, :]
                qa = qt * jnp.exp(Gt - Gr)
                kb = jnp.where(row_k < t0, kk * jnp.exp(jnp.minimum(Gr - G, 0.0)), 0.0)
                a_i = a_i + lax.dot_general(qa.astype(BF16), kb.astype(BF16), NT_DIMS,
                                            preferred_element_type=F32)
            a_rows.append(a_i)
        A = jnp.concatenate(a_rows, axis=0)
        o = o_inter + jnp.dot(A.astype(BF16), v.astype(BF16), preferred_element_type=F32)

        kd = kk * jnp.exp(G_end - G)
        st_ref[...] = jnp.exp(G_end) * st + jnp.dot(v.T.astype(BF16), kd.astype(BF16),
                                                    preferred_element_type=F32)

        ms = jnp.mean(o * o, axis=-1, keepdims=True)
        zg = zg_ref[0, sl, :]
        o_ref[0, sl, :] = (o * lax.rsqrt(ms + RMS_EPS) * og * jax.nn.silu(zg)).astype(o_ref.dtype)


def hgrn2_scan(z, lb, o_gain, *, tt=512):
    b, t, d4 = z.shape
    d = d4 // 4
    nh = d // HEAD_DIM
    tt = min(tt, t)

    def zspec(part):
        return pl.BlockSpec((1, tt, HEAD_DIM), lambda bi, h, ti: (bi, ti, part * nh + h))

    return pl.pallas_call(
        functools.partial(_hgrn_kernel, n_chunks=tt // HGRN_CHUNK),
        grid=(b, nh, t // tt),
        in_specs=[zspec(0), zspec(1), zspec(2), zspec(3),
                  pl.BlockSpec((1, HEAD_DIM), lambda bi, h, ti: (0, h)),
                  pl.BlockSpec((1, HEAD_DIM), lambda bi, h, ti: (0, 0))],
        out_specs=pl.BlockSpec((1, tt, HEAD_DIM), lambda bi, h, ti: (bi, ti, h)),
        out_shape=jax.ShapeDtypeStruct((b, t, d), BF16),
        scratch_shapes=[pltpu.VMEM((HEAD_DIM, HEAD_DIM), F32),
                        pltpu.VMEM((HGRN_CHUNK, HEAD_DIM), F32),
                        pltpu.VMEM((HGRN_CHUNK, HEAD_DIM), F32)],
        compiler_params=_params("arbitrary", "arbitrary", "arbitrary"),
        name="hgrn2_scan",
    )(z, z, z, z, lb.reshape(1, d), o_gain.reshape(1, HEAD_DIM))


def _moba_kernel(q_ref, k_ref, v_ref, o_ref, km_ref, m_ref, l_ref, acc_ref, *, n_blocks):
    BLK = MOBA_BLOCK
    qi = pl.program_id(2)
    scale = HEAD_DIM ** -0.5

    @pl.when(qi == 0)
    def _():
        for j in range(n_blocks):
            kb = k_ref[0, j * BLK:(j + 1) * BLK, :].astype(F32)
            km_ref[j:j + 1, :] = jnp.mean(kb, axis=0, keepdims=True)

    q = q_ref[0]
    gate = lax.dot_general(q.astype(F32), km_ref[...], NT_DIMS, precision=HIGHEST,
                           preferred_element_type=F32)
    blk_id = lax.broadcasted_iota(jnp.int32, (BLK, n_blocks), 1)
    past = blk_id < qi
    gm = jnp.where(past, gate, -jnp.inf)
    cnt = jnp.zeros((BLK, n_blocks), jnp.int32)
    for m in range(n_blocks):
        gmm = gm[:, m:m + 1]
        ahead = (gmm > gm) | ((gmm == gm) & (blk_id > m))
        cnt = cnt + ahead.astype(jnp.int32)
    bias = jnp.where(past & (cnt < MOBA_TOP_K), 0.0, -jnp.inf).astype(F32)

    own = pl.ds(pl.multiple_of(qi * BLK, BLK), BLK)
    s = lax.dot_general(q, k_ref[0, own, :], NT_DIMS, preferred_element_type=F32) * scale
    row = lax.broadcasted_iota(jnp.int32, (BLK, BLK), 0)
    col = lax.broadcasted_iota(jnp.int32, (BLK, BLK), 1)
    s = jnp.where(col <= row, s, -jnp.inf)
    m0 = jnp.max(s, axis=-1, keepdims=True)
    p = jnp.exp(s - m0)
    m_ref[...] = m0
    l_ref[...] = jnp.sum(p, axis=-1, keepdims=True)
    acc_ref[...] = jnp.dot(p.astype(BF16), v_ref[0, own, :], preferred_element_type=F32)

    for j in range(n_blocks - 1):
        @pl.when(j < qi)
        def _(j=j):
            kj = k_ref[0, j * BLK:(j + 1) * BLK, :]
            vj = v_ref[0, j * BLK:(j + 1) * BLK, :]
            sj = lax.dot_general(q, kj, NT_DIMS, preferred_element_type=F32) * scale
            sj = sj + bias[:, j:j + 1]
            m_old = m_ref[...]
            m_new = jnp.maximum(m_old, jnp.max(sj, axis=-1, keepdims=True))
            alpha = jnp.exp(m_old - m_new)
            pj = jnp.exp(sj - m_new)
            m_ref[...] = m_new
            l_ref[...] = alpha * l_ref[...] + jnp.sum(pj, axis=-1, keepdims=True)
            acc_ref[...] = alpha * acc_ref[...] + jnp.dot(pj.astype(BF16), vj,
                                                          preferred_element_type=F32)

    o_ref[0] = (acc_ref[...] / l_ref[...]).astype(o_ref.dtype)


def moba_attention(q, kv):
    b, t, d = q.shape
    nh = d // HEAD_DIM
    nb = t // MOBA_BLOCK
    return pl.pallas_call(
        functools.partial(_moba_kernel, n_blocks=nb),
        grid=(b, nh, nb),
        in_specs=[pl.BlockSpec((1, MOBA_BLOCK, HEAD_DIM), lambda bi, h, qi: (bi, qi, h)),
                  pl.BlockSpec((1, t, HEAD_DIM), lambda bi, h, qi: (bi, 0, h)),
                  pl.BlockSpec((1, t, HEAD_DIM), lambda bi, h, qi: (bi, 0, nh + h))],
        out_specs=pl.BlockSpec((1, MOBA_BLOCK, HEAD_DIM), lambda bi, h, qi: (bi, qi, h)),
        out_shape=jax.ShapeDtypeStruct((b, t, d), BF16),
        scratch_shapes=[pltpu.VMEM((nb, HEAD_DIM), F32),
                        pltpu.VMEM((MOBA_BLOCK, 1), F32),
                        pltpu.VMEM((MOBA_BLOCK, 1), F32),
                        pltpu.VMEM((MOBA_BLOCK, HEAD_DIM), F32)],
        compiler_params=_params("arbitrary", "arbitrary", "arbitrary"),
        name="moba_attention",
    )(q, kv, kv)


def _router_kernel(x_ref, g_ref, wr_ref, br_ref, h_ref, ids_ref, wts_ref, *, tm):
    _rms_rows_to(x_ref, g_ref, h_ref, tm, 64)
    logits = jnp.dot(h_ref[...], wr_ref[...], precision=HIGHEST,
                     preferred_element_type=F32) + br_ref[...]
    lane = lax.broadcasted_iota(jnp.int32, logits.shape, 1)
    neg = -jnp.inf
    big = ROUTER_LANES

    gl = jnp.where(lane < N_GROUPS, logits, neg)
    gmax = jnp.max(gl, axis=-1, keepdims=True)
    gsum = jnp.sum(jnp.exp(gl - gmax), axis=-1, keepdims=True)
    g_top_p = 1.0 / gsum
    g_top = jnp.min(jnp.where(gl == gmax, lane, big), axis=-1, keepdims=True)

    lo = N_GROUPS + EXPERTS_PER_GROUP * g_top
    el = jnp.where((lane >= lo) & (lane < lo + EXPERTS_PER_GROUP), logits, neg)
    emax = jnp.max(el, axis=-1, keepdims=True)
    esum = jnp.sum(jnp.exp(el - emax), axis=-1, keepdims=True)
    i1 = jnp.min(jnp.where(el == emax, lane, big), axis=-1, keepdims=True)
    el2 = jnp.where(lane == i1, neg, el)
    emax2 = jnp.max(el2, axis=-1, keepdims=True)
    i2 = jnp.min(jnp.where(el2 == emax2, lane, big), axis=-1, keepdims=True)
    p1 = 1.0 / esum
    p2 = jnp.exp(emax2 - emax) / esum
    psum = p1 + p2
    w1 = g_top_p * p1 / psum
    w2 = g_top_p * p2 / psum

    ids_ref[...] = jnp.where(lane == 0, i1 - N_GROUPS, jnp.where(lane == 1, i2 - N_GROUPS, 0))
    wts_ref[...] = jnp.where(lane == 0, w1, jnp.where(lane == 1, w2, 0.0))


def moe_router(x, gain, w_router, b_router, *, tm=512):
    n, d = x.shape
    tm = min(tm, n)
    return pl.pallas_call(
        functools.partial(_router_kernel, tm=tm),
        grid=(n // tm,),
        in_specs=[pl.BlockSpec((tm, d), lambda i: (i, 0)),
                  pl.BlockSpec((1, d), lambda i: (0, 0)),
                  pl.BlockSpec((d, ROUTER_LANES), lambda i: (0, 0)),
                  pl.BlockSpec((1, ROUTER_LANES), lambda i: (0, 0))],
        out_specs=[pl.BlockSpec((tm, d), lambda i: (i, 0)),
                   pl.BlockSpec((tm, ROUTER_LANES), lambda i: (i, 0)),
                   pl.BlockSpec((tm, ROUTER_LANES), lambda i: (i, 0))],
        out_shape=[jax.ShapeDtypeStruct((n, d), F32),
                   jax.ShapeDtypeStruct((n, ROUTER_LANES), jnp.int32),
                   jax.ShapeDtypeStruct((n, ROUTER_LANES), F32)],
        compiler_params=_params("arbitrary"),
        name="moe_router",
    )(x, gain.reshape(1, d), w_router, b_router)


def _expert_kernel(blk_e_ref, nvalid_ref, src_ref, src_next_ref, h_hbm, wg_ref, wu_ref, wd_ref,
                   out_hbm, xbuf, ybuf, wgb, wub, wdb, sem_in, sem_out, *, n_blk):
    i = pl.program_id(0)
    slot = i % 2

    def row_in(tok, r, s):
        return pltpu.make_async_copy(h_hbm.at[pl.ds(tok, 1), :], xbuf.at[s, pl.ds(r, 1), :],
                                     sem_in.at[s])

    def row_out(a, r, s):
        return pltpu.make_async_copy(ybuf.at[s, pl.ds(r, 1), :], out_hbm.at[pl.ds(a, 1), :],
                                     sem_out.at[s])

    def start_gather(idx_ref, blk, s):
        def body(r, c):
            row_in(idx_ref[0, 0, r] // TOP_K_IN_GROUP, r, s).start()
            return c
        lax.fori_loop(0, nvalid_ref[blk], body, 0)

    def wait_rows(make, count, s):
        def body(r, c):
            make(0, 0, s).wait()
            return c
        lax.fori_loop(0, count, body, 0)

    @pl.when(i == 0)
    def _():
        xbuf[...] = jnp.zeros_like(xbuf)
        start_gather(src_ref, 0, 0)

    @pl.when(i + 1 < n_blk)
    def _():
        start_gather(src_next_ref, i + 1, 1 - slot)

    nv = nvalid_ref[i]
    wait_rows(row_in, nv, slot)

    @pl.when((i == 0) | (blk_e_ref[i] != blk_e_ref[jnp.maximum(i - 1, 0)]))
    def _():
        wgb[...] = wg_ref[0].astype(BF16)
        wub[...] = wu_ref[0].astype(BF16)
        wdb[...] = wd_ref[0].astype(BF16)

    @pl.when(i >= 2)
    def _():
        wait_rows(row_out, nvalid_ref[jnp.maximum(i - 2, 0)], slot)

    @pl.when(nv > 0)
    def _():
        x = xbuf[slot].astype(BF16)
        g = jnp.dot(x, wgb[...], preferred_element_type=F32)
        u = jnp.dot(x, wub[...], preferred_element_type=F32)
        mid = (jax.nn.silu(g) * u).astype(BF16)
        ybuf[slot] = jnp.dot(mid, wdb[...], preferred_element_type=F32)

        def body(r, c):
            row_out(src_ref[0, 0, r], r, slot).start()
            return c
        lax.fori_loop(0, nv, body, 0)

    @pl.when(i == n_blk - 1)
    def _():
        wait_rows(row_out, nv, slot)
        if n_blk >= 2:
            wait_rows(row_out, nvalid_ref[n_blk - 2], 1 - slot)


def expert_mlp(h, blk_e, nvalid, src, w_gate, w_up, w_down):
    n, d = h.shape
    n_blk, _, r = src.shape
    ff = w_gate.shape[2]
    a_total = n * TOP_K_IN_GROUP
    grid_spec = pltpu.PrefetchScalarGridSpec(
        num_scalar_prefetch=2,
        grid=(n_blk,),
        in_specs=[
            pl.BlockSpec((1, 1, r), lambda i, be, nv: (i, 0, 0), memory_space=pltpu.SMEM),
            pl.BlockSpec((1, 1, r), lambda i, be, nv: (jnp.minimum(i + 1, n_blk - 1), 0, 0),
                         memory_space=pltpu.SMEM),
            pl.BlockSpec(memory_space=pl.ANY),
            pl.BlockSpec((1, d, ff), lambda i, be, nv: (be[i], 0, 0)),
            pl.BlockSpec((1, d, ff), lambda i, be, nv: (be[i], 0, 0)),
            pl.BlockSpec((1, ff, d), lambda i, be, nv: (be[i], 0, 0)),
        ],
        out_specs=pl.BlockSpec(memory_space=pl.ANY),
        scratch_shapes=[
            pltpu.VMEM((2, r, d), F32),
            pltpu.VMEM((2, r, d), F32),
            pltpu.VMEM((d, ff), BF16),
            pltpu.VMEM((d, ff), BF16),
            pltpu.VMEM((ff, d), BF16),
            pltpu.SemaphoreType.DMA((2,)),
            pltpu.SemaphoreType.DMA((2,)),
        ],
    )
    return pl.pallas_call(
        functools.partial(_expert_kernel, n_blk=n_blk),
        grid_spec=grid_spec,
        out_shape=jax.ShapeDtypeStruct((a_total, d), F32),
        compiler_params=_params("arbitrary"),
        name="expert_mlp",
    )(blk_e, nvalid, src, src, h, w_gate, w_up, w_down)


def _combine_kernel(x_ref, y0_ref, y1_ref, w_ref, o_ref):
    w = w_ref[...]
    o_ref[...] = x_ref[...] + (w[:, 0:1] * y0_ref[...] + w[:, 1:2] * y1_ref[...])


def moe_combine(x, y2, wts, *, tm=512):
    n, d = x.shape
    tm = min(tm, n)
    return pl.pallas_call(
        _combine_kernel,
        grid=(n // tm,),
        in_specs=[pl.BlockSpec((tm, d), lambda i: (i, 0)),
                  pl.BlockSpec((tm, d), lambda i: (i, 0)),
                  pl.BlockSpec((tm, d), lambda i: (i, 1)),
                  pl.BlockSpec((tm, ROUTER_LANES), lambda i: (i, 0))],
        out_specs=pl.BlockSpec((tm, d), lambda i: (i, 0)),
        out_shape=jax.ShapeDtypeStruct((n, d), F32),
        compiler_params=_params("arbitrary"),
        name="moe_combine",
    )(x, y2, y2, wts)


def _expert_row_index(expert_id, rows_per_block):
    a_total = expert_id.size
    r = rows_per_block
    flat_e = expert_id.reshape(-1)
    order = jnp.argsort(flat_e).astype(jnp.int32)
    counts = jnp.sum((flat_e[:, None] == jnp.arange(N_EXPERTS)[None, :]).astype(jnp.int32), axis=0)
    padded = (counts + r - 1) // r * r
    pad_end = jnp.cumsum(padded)
    pad_start = pad_end - padded
    start = jnp.cumsum(counts) - counts
    n_blk = (a_total + N_EXPERTS * (r - 1) + r - 1) // r
    blk_row0 = jnp.arange(n_blk, dtype=jnp.int32) * r
    blk_e = jnp.minimum(jnp.searchsorted(pad_end, blk_row0, side='right'), N_EXPERTS - 1).astype(jnp.int32)
    in_e = blk_row0 - pad_start[blk_e]
    nvalid = jnp.clip(counts[blk_e] - in_e, 0, r).astype(jnp.int32)
    idx = (start[blk_e] + in_e)[:, None] + jnp.arange(r, dtype=jnp.int32)[None, :]
    src = order[jnp.clip(idx, 0, a_total - 1)]
    return blk_e, nvalid, src.reshape(n_blk, 1, r)


def hier_moe(x, gain, w_group, b_group, w_expert, b_expert, w_gate, w_up, w_down):
    n, d = x.shape
    pad = ROUTER_LANES - N_GROUPS - N_EXPERTS
    w_router = jnp.concatenate([w_group, w_expert, jnp.zeros((d, pad), F32)], axis=1)
    b_router = jnp.concatenate([b_group, b_expert, jnp.zeros((pad,), F32)]).reshape(1, ROUTER_LANES)
    h, ids, wts = moe_router(x, gain, w_router, b_router)
    blk_e, nvalid, src = _expert_row_index(ids[:, :TOP_K_IN_GROUP], EXPERT_ROWS)
    y = expert_mlp(h, blk_e, nvalid, src, w_gate, w_up, w_down)
    return moe_combine(x, y.reshape(n, TOP_K_IN_GROUP * d), wts)


def _rope_tables(t):
    half = HEAD_DIM // 2
    inv_freq = ROPE_THETA ** (-jnp.arange(0, HEAD_DIM, 2, dtype=F32) / HEAD_DIM)
    ang = jnp.arange(t, dtype=F32)[:, None] * inv_freq[None, :]
    cos, sin = jnp.cos(ang), jnp.sin(ang)
    return jnp.concatenate([cos, cos], axis=1), jnp.concatenate([-sin, sin], axis=1)


def kernel(x, norm_mix, norm_ffn, a_w_in, a_lb_logits, a_o_gain, a_w_out, kv_norm, kv_w, k_norm,
           b_w_q, b_q_norm, b_w_out, moe_w_group, moe_b_group, moe_w_expert, moe_b_expert,
           moe_w_gate, moe_w_up, moe_w_down):
    b, t, d = x.shape
    n = b * t
    depth = norm_mix.shape[0]
    n_a = a_w_in.shape[0]
    lower_bounds = jnp.cumsum(jax.nn.softmax(a_lb_logits.astype(F32), axis=0), axis=0)
    cos, sin = _rope_tables(t)
    xf = x.reshape(n, d)
    kv = None
    for l in range(depth):
        if l < n_a:
            z = rms_matmul(xf, norm_mix[l], a_w_in[l].astype(BF16), F32)
            o = hgrn2_scan(z.reshape(b, t, 4 * d), lower_bounds[l], a_o_gain[l])
            xf = matmul_residual(o.reshape(n, d), a_w_out[l].astype(BF16), xf)
        else:
            if l == n_a:
                kv = rms_matmul(xf, kv_norm, kv_w.astype(BF16), BF16,
                                rope=(k_norm, cos, sin, d), seq_len=t)
            lb_ = l - n_a
            q = rms_matmul(xf, norm_mix[l], b_w_q[lb_].astype(BF16), BF16,
                           rope=(b_q_norm[lb_], cos, sin, d), seq_len=t)
            o = moba_attention(q.reshape(b, t, d), kv.reshape(b, t, 2 * d))
            xf = matmul_residual(o.reshape(n, d), b_w_out[lb_].astype(BF16), xf)
        xf = hier_moe(xf, norm_ffn[l], moe_w_group[l], moe_b_group[l], moe_w_expert[l],
                      moe_b_expert[l], moe_w_gate[l], moe_w_up[l], moe_w_down[l])
    return xf.reshape(b, t, d)
```

```python
import functools

import jax
import jax.numpy as jnp
from jax import lax
from jax.experimental import pallas as pl
from jax.experimental.pallas import tpu as pltpu

HEAD_DIM = 128
HGRN_CHUNK = 64
HGRN_SUB = 16
MOBA_BLOCK = 256
MOBA_TOP_K = 3
ROPE_THETA = 10000.0
N_GROUPS = 4
EXPERTS_PER_GROUP = 8
N_EXPERTS = N_GROUPS * EXPERTS_PER_GROUP
TOP_K_IN_GROUP = 2
RMS_EPS = 1e-6
ROUTER_LANES = 128
EXPERT_ROWS = 256
DMA_UNROLL = 8

V7X_VMEM_LIMIT = 56 * 1024 * 1024

F32 = jnp.float32
BF16 = jnp.bfloat16
HIGHEST = lax.Precision.HIGHEST
NT_DIMS = (((1,), (1,)), ((), ()))


def _params(*sem):
    return pltpu.CompilerParams(dimension_semantics=sem, vmem_limit_bytes=V7X_VMEM_LIMIT)


def _rms_rows_to(x_ref, g_ref, out_ref, rows, chunk):
    gain = g_ref[...]

    def body(c, carry):
        sl = pl.ds(pl.multiple_of(c * chunk, chunk), chunk)
        x = x_ref[sl, :]
        ms = jnp.mean(x * x, axis=-1, keepdims=True)
        out_ref[sl, :] = (x * lax.rsqrt(ms + RMS_EPS) * gain).astype(out_ref.dtype)
        return carry

    lax.fori_loop(0, rows // chunk, body, 0)


def _head_norm_rope(y, hg, cos, sin):
    ms = jnp.mean(y * y, axis=-1, keepdims=True)
    yn = y * lax.rsqrt(ms + RMS_EPS) * hg
    return yn * cos + pltpu.roll(yn, HEAD_DIM // 2, axis=1) * sin


def _rms_matmul_kernel(*refs, tm, tn, n_rope_tiles):
    if n_rope_tiles:
        x_ref, g_ref, w_ref, hg_ref, cos_ref, sin_ref, o_ref, hn_ref = refs
    else:
        x_ref, g_ref, w_ref, o_ref, hn_ref = refs
    j = pl.program_id(1)

    @pl.when(j == 0)
    def _():
        _rms_rows_to(x_ref, g_ref, hn_ref, tm, 64)

    y = jnp.dot(hn_ref[...], w_ref[...].astype(BF16), preferred_element_type=F32)

    if not n_rope_tiles:
        o_ref[...] = y.astype(o_ref.dtype)
        return

    @pl.when(j < n_rope_tiles)
    def _():
        hg = hg_ref[...]
        cos = cos_ref[...]
        sin = sin_ref[...]
        for h in range(tn // HEAD_DIM):
            sl = slice(h * HEAD_DIM, (h + 1) * HEAD_DIM)
            o_ref[:, sl] = _head_norm_rope(y[:, sl], hg, cos, sin).astype(o_ref.dtype)

    @pl.when(j >= n_rope_tiles)
    def _():
        o_ref[...] = y.astype(o_ref.dtype)


def rms_matmul(x, gain, w, layer, out_dtype, *, rope=None, seq_len=None, tm=1024, tn=512):
    n, d = x.shape
    e = w.shape[2]
    tm = min(tm, n)
    tn = min(tn, e)
    in_specs = [
        pl.BlockSpec((tm, d), lambda i, j: (i, 0)),
        pl.BlockSpec((1, d), lambda i, j: (0, 0)),
        pl.BlockSpec((None, d, tn), lambda i, j: (layer, 0, j)),
    ]
    args = [x, gain.reshape(1, d), w]
    n_rope_tiles = 0
    if rope is not None:
        hg, cos, sin, n_rope_cols = rope
        tm = min(tm, seq_len)
        tn = min(tn, n_rope_cols)
        in_specs[0] = pl.BlockSpec((tm, d), lambda i, j: (i, 0))
        in_specs[2] = pl.BlockSpec((None, d, tn), lambda i, j: (layer, 0, j))
        n_rope_tiles = n_rope_cols // tn
        pos_tiles = seq_len // tm
        in_specs += [
            pl.BlockSpec((1, HEAD_DIM), lambda i, j: (0, 0)),
            pl.BlockSpec((tm, HEAD_DIM), lambda i, j: (i % pos_tiles, 0)),
            pl.BlockSpec((tm, HEAD_DIM), lambda i, j: (i % pos_tiles, 0)),
        ]
        args += [hg.reshape(1, HEAD_DIM), cos, sin]
    return pl.pallas_call(
        functools.partial(_rms_matmul_kernel, tm=tm, tn=tn, n_rope_tiles=n_rope_tiles),
        grid=(n // tm, e // tn),
        in_specs=in_specs,
        out_specs=pl.BlockSpec((tm, tn), lambda i, j: (i, j)),
        out_shape=jax.ShapeDtypeStruct((n, e), out_dtype),
        scratch_shapes=[pltpu.VMEM((tm, d), BF16)],
        compiler_params=_params("arbitrary", "arbitrary"),
        name="rms_matmul",
    )(*args)


def _matmul_res_kernel(a_ref, w_ref, r_ref, o_ref):
    o_ref[...] = r_ref[...] + jnp.dot(a_ref[...], w_ref[...].astype(BF16),
                                      preferred_element_type=F32)


def matmul_residual(a, w, layer, res, *, tm=1024, tn=512):
    n, d = a.shape
    e = w.shape[2]
    tm = min(tm, n)
    tn = min(tn, e)
    return pl.pallas_call(
        _matmul_res_kernel,
        grid=(n // tm, e // tn),
        in_specs=[
            pl.BlockSpec((tm, d), lambda i, j: (i, 0)),
            pl.BlockSpec((None, d, tn), lambda i, j: (layer, 0, j)),
            pl.BlockSpec((tm, tn), lambda i, j: (i, j)),
        ],
        out_specs=pl.BlockSpec((tm, tn), lambda i, j: (i, j)),
        out_shape=jax.ShapeDtypeStruct((n, e), F32),
        compiler_params=_params("arbitrary", "arbitrary"),
        name="matmul_residual",
    )(a, w, res)


def _hgrn_kernel(zq_ref, zf_ref, zi_ref, zg_ref, lb_ref, og_ref, o_ref,
                 st_ref, g_scr, k_scr, *, n_chunks):
    C = HGRN_CHUNK
    SB = HGRN_SUB

    @pl.when(pl.program_id(2) == 0)
    def _():
        st_ref[...] = jnp.zeros_like(st_ref)

    lb = lb_ref[...]
    og = og_ref[...]
    row_c = lax.broadcasted_iota(jnp.int32, (C, C), 0)
    col_c = lax.broadcasted_iota(jnp.int32, (C, C), 1)
    tril = (row_c >= col_c).astype(F32)
    row_k = lax.broadcasted_iota(jnp.int32, (C, HEAD_DIM), 0)
    row_s = lax.broadcasted_iota(jnp.int32, (SB, 1), 0)
    lane_s = lax.broadcasted_iota(jnp.int32, (SB, C), 1)

    for c in range(n_chunks):
        sl = slice(c * C, (c + 1) * C)
        q = zq_ref[0, sl, :]
        v = zi_ref[0, sl, :]
        f = lb + (1.0 - lb) * jax.nn.sigmoid(zf_ref[0, sl, :])
        kk = 1.0 - f
        g = jnp.log(f)
        G = jnp.dot(tril, g, precision=HIGHEST, preferred_element_type=F32)
        g_scr[...] = G
        k_scr[...] = kk
        G_end = G[C - 1:C, :]

        st = st_ref[...]
        o_inter = lax.dot_general((q * jnp.exp(G)).astype(BF16), st.astype(BF16), NT_DIMS,
                                  preferred_element_type=F32)

        a_rows = []
        for i in range(C // SB):
            t0 = i * SB
            Gt = G[t0:t0 + SB, :]
            qt = q[t0:t0 + SB, :]
            a_i = jnp.zeros((SB, C), F32)
            for sl_ in range(SB):
                s = t0 + sl_
                Gs = g_scr[s:s + 1, :]
                ks = k_scr[s:s + 1, :]
                e = jnp.exp(jnp.minimum(Gt - Gs, 0.0))
                col = jnp.sum(qt * ks * e, axis=-1, keepdims=True)
                col = jnp.where(row_s >= sl_, col, 0.0)
                a_i = jnp.where(lane_s == s, col, a_i)
            if i > 0:
                Gr = g_scr[t0 - 1:t0, :]
                qa = qt * jnp.exp(Gt - Gr)
                kb = jnp.where(row_k < t0, kk * jnp.exp(jnp.minimum(Gr - G, 0.0)), 0.0)
                a_i = a_i + lax.dot_general(qa.astype(BF16), kb.astype(BF16), NT_DIMS,
                                            preferred_element_type=F32)
            a_rows.append(a_i)
        A = jnp.concatenate(a_rows, axis=0)
        o = o_inter + jnp.dot(A.astype(BF16), v.astype(BF16), preferred_element_type=F32)

        kd = kk * jnp.exp(G_end - G)
        st_ref[...] = jnp.exp(G_end) * st + jnp.dot(v.T.astype(BF16), kd.astype(BF16),
                                                    preferred_element_type=F32)

        ms = jnp.mean(o * o, axis=-1, keepdims=True)
        zg = zg_ref[0, sl, :]
        o_ref[0, sl, :] = (o * lax.rsqrt(ms + RMS_EPS) * og * jax.nn.silu(zg)).astype(o_ref.dtype)


def hgrn2_scan(z, lb, o_gain, *, tt=512):
    b, t, d4 = z.shape
    d = d4 // 4
    nh = d // HEAD_DIM
    tt = min(tt, t)

    def zspec(part):
        return pl.BlockSpec((1, tt, HEAD_DIM), lambda bi, h, ti: (bi, ti, part * nh + h))

    return pl.pallas_call(
        functools.partial(_hgrn_kernel, n_chunks=tt // HGRN_CHUNK),
        grid=(b, nh, t // tt),
        in_specs=[zspec(0), zspec(1), zspec(2), zspec(3),
                  pl.BlockSpec((1, HEAD_DIM), lambda bi, h, ti: (0, h)),
                  pl.BlockSpec((1, HEAD_DIM), lambda bi, h, ti: (0, 0))],
        out_specs=pl.BlockSpec((1, tt, HEAD_DIM), lambda bi, h, ti: (bi, ti, h)),
        out_shape=jax.ShapeDtypeStruct((b, t, d), BF16),
        scratch_shapes=[pltpu.VMEM((HEAD_DIM, HEAD_DIM), F32),
                        pltpu.VMEM((HGRN_CHUNK, HEAD_DIM), F32),
                        pltpu.VMEM((HGRN_CHUNK, HEAD_DIM), F32)],
        compiler_params=_params("arbitrary", "arbitrary", "arbitrary"),
        name="hgrn2_scan",
    )(z, z, z, z, lb.reshape(1, d), o_gain.reshape(1, HEAD_DIM))


def _moba_kernel(q_ref, k_ref, v_ref, o_ref, km_ref, vt_ref, bias_ref, m_ref, l_ref, acc_ref,
                 *, n_blocks, heads):
    BLK = MOBA_BLOCK
    qi = pl.program_id(2)
    exp2_scale = HEAD_DIM ** -0.5 * 1.4426950408889634

    def head(h):
        return slice(h * HEAD_DIM, (h + 1) * HEAD_DIM)

    @pl.when(qi == 0)
    def _():
        r = lax.broadcasted_iota(jnp.int32, (HEAD_DIM, HEAD_DIM), 0)
        c = lax.broadcasted_iota(jnp.int32, (HEAD_DIM, HEAD_DIM), 1)
        eye = (r == c).astype(BF16)
        for h in range(heads):
            for j in range(n_blocks):
                rows = slice(j * BLK, (j + 1) * BLK)
                kb = k_ref[0, rows, head(h)].astype(F32)
                km_ref[h, j:j + 1, :] = jnp.mean(kb, axis=0, keepdims=True)
                vt_ref[h, :, rows] = lax.dot_general(eye, v_ref[0, rows, head(h)], NT_DIMS,
                                                     preferred_element_type=F32).astype(BF16)

    blk_id = lax.broadcasted_iota(jnp.int32, (n_blocks, BLK), 0)
    past = blk_id < qi
    key_i = lax.broadcasted_iota(jnp.int32, (BLK, BLK), 0)
    qry_i = lax.broadcasted_iota(jnp.int32, (BLK, BLK), 1)
    own = pl.ds(pl.multiple_of(qi * BLK, BLK), BLK)

    def scores(rows):
        return [lax.dot_general(k_ref[0, rows, head(h)], q_ref[0, :, head(h)], NT_DIMS,
                                preferred_element_type=F32) for h in range(heads)]

    for h in range(heads):
        gate = lax.dot_general(km_ref[h], q_ref[0, :, head(h)].astype(F32), NT_DIMS,
                               precision=HIGHEST, preferred_element_type=F32)
        gm = jnp.where(past, gate, -jnp.inf)
        cnt = jnp.zeros((n_blocks, BLK), jnp.int32)
        for m in range(n_blocks):
            gmm = gm[m:m + 1, :]
            ahead = (gmm > gm) | ((gmm == gm) & (blk_id > m))
            cnt = cnt + ahead.astype(jnp.int32)
        bias_ref[h] = jnp.where(past & (cnt < MOBA_TOP_K), 0.0, -jnp.inf).astype(F32)

    probs = []
    for h, s in enumerate(scores(own)):
        s = jnp.where(key_i <= qry_i, s, -jnp.inf)
        m0 = jnp.max(s, axis=0, keepdims=True)
        p = jnp.exp2((s - m0) * exp2_scale)
        m_ref[h] = m0
        l_ref[h] = jnp.sum(p, axis=0, keepdims=True)
        probs.append(p.astype(BF16))
    for h in range(heads):
        acc_ref[h] = jnp.dot(vt_ref[h, :, own], probs[h], preferred_element_type=F32)

    for j in range(n_blocks - 1):
        @pl.when(j < qi)
        def _(j=j):
            rows = slice(j * BLK, (j + 1) * BLK)
            probs, alphas = [], []
            for h, sj in enumerate(scores(rows)):
                sj = sj + bias_ref[h, j:j + 1, :]
                m_old = m_ref[h]
                m_new = jnp.maximum(m_old, jnp.max(sj, axis=0, keepdims=True))
                alpha = jnp.exp2((m_old - m_new) * exp2_scale)
                pj = jnp.exp2((sj - m_new) * exp2_scale)
                m_ref[h] = m_new
                l_ref[h] = alpha * l_ref[h] + jnp.sum(pj, axis=0, keepdims=True)
                probs.append(pj.astype(BF16))
                alphas.append(alpha)
            for h in range(heads):
                acc_ref[h] = alphas[h] * acc_ref[h] + jnp.dot(vt_ref[h, :, rows], probs[h],
                                                              preferred_element_type=F32)

    for h in range(heads):
        o_ref[0, :, head(h)] = (acc_ref[h] / l_ref[h]).T.astype(o_ref.dtype)


def moba_attention(q, kv, *, heads=4):
    b, t, d = q.shape
    nh = d // HEAD_DIM
    heads = min(heads, nh)
    ng = nh // heads
    nb = t // MOBA_BLOCK
    w = heads * HEAD_DIM
    return pl.pallas_call(
        functools.partial(_moba_kernel, n_blocks=nb, heads=heads),
        grid=(b, ng, nb),
        in_specs=[pl.BlockSpec((1, MOBA_BLOCK, w), lambda bi, g, qi: (bi, qi, g)),
                  pl.BlockSpec((1, t, w), lambda bi, g, qi: (bi, 0, g)),
                  pl.BlockSpec((1, t, w), lambda bi, g, qi: (bi, 0, ng + g))],
        out_specs=pl.BlockSpec((1, MOBA_BLOCK, w), lambda bi, g, qi: (bi, qi, g)),
        out_shape=jax.ShapeDtypeStruct((b, t, d), BF16),
        scratch_shapes=[pltpu.VMEM((heads, nb, HEAD_DIM), F32),
                        pltpu.VMEM((heads, HEAD_DIM, t), BF16),
                        pltpu.VMEM((heads, nb, MOBA_BLOCK), F32),
                        pltpu.VMEM((heads, 1, MOBA_BLOCK), F32),
                        pltpu.VMEM((heads, 1, MOBA_BLOCK), F32),
                        pltpu.VMEM((heads, HEAD_DIM, MOBA_BLOCK), F32)],
        compiler_params=_params("arbitrary", "arbitrary", "arbitrary"),
        name="moba_attention",
    )(q, kv, kv)


def _router_kernel(x_ref, g_ref, wr_ref, br_ref, h_ref, ids_ref, wts_ref, *, tm):
    _rms_rows_to(x_ref, g_ref, h_ref, tm, 64)
    logits = jnp.dot(h_ref[...], wr_ref[...], precision=HIGHEST,
                     preferred_element_type=F32) + br_ref[...]
    lane = lax.broadcasted_iota(jnp.int32, logits.shape, 1)
    neg = -jnp.inf
    big = ROUTER_LANES

    gl = jnp.where(lane < N_GROUPS, logits, neg)
    gmax = jnp.max(gl, axis=-1, keepdims=True)
    gsum = jnp.sum(jnp.exp(gl - gmax), axis=-1, keepdims=True)
    g_top_p = 1.0 / gsum
    g_top = jnp.min(jnp.where(gl == gmax, lane, big), axis=-1, keepdims=True)

    lo = N_GROUPS + EXPERTS_PER_GROUP * g_top
    el = jnp.where((lane >= lo) & (lane < lo + EXPERTS_PER_GROUP), logits, neg)
    emax = jnp.max(el, axis=-1, keepdims=True)
    esum = jnp.sum(jnp.exp(el - emax), axis=-1, keepdims=True)
    i1 = jnp.min(jnp.where(el == emax, lane, big), axis=-1, keepdims=True)
    el2 = jnp.where(lane == i1, neg, el)
    emax2 = jnp.max(el2, axis=-1, keepdims=True)
    i2 = jnp.min(jnp.where(el2 == emax2, lane, big), axis=-1, keepdims=True)
    p1 = 1.0 / esum
    p2 = jnp.exp(emax2 - emax) / esum
    psum = p1 + p2
    w1 = g_top_p * p1 / psum
    w2 = g_top_p * p2 / psum

    ids_ref[...] = jnp.where(lane == 0, i1 - N_GROUPS, jnp.where(lane == 1, i2 - N_GROUPS, 0))
    wts_ref[...] = jnp.where(lane == 0, w1, jnp.where(lane == 1, w2, 0.0))


def moe_router(x, gain, w_router, b_router, *, tm=512):
    n, d = x.shape
    tm = min(tm, n)
    return pl.pallas_call(
        functools.partial(_router_kernel, tm=tm),
        grid=(n // tm,),
        in_specs=[pl.BlockSpec((tm, d), lambda i: (i, 0)),
                  pl.BlockSpec((1, d), lambda i: (0, 0)),
                  pl.BlockSpec((d, ROUTER_LANES), lambda i: (0, 0)),
                  pl.BlockSpec((1, ROUTER_LANES), lambda i: (0, 0))],
        out_specs=[pl.BlockSpec((tm, d), lambda i: (i, 0)),
                   pl.BlockSpec((tm, ROUTER_LANES), lambda i: (i, 0)),
                   pl.BlockSpec((tm, ROUTER_LANES), lambda i: (i, 0))],
        out_shape=[jax.ShapeDtypeStruct((n, d), F32),
                   jax.ShapeDtypeStruct((n, ROUTER_LANES), jnp.int32),
                   jax.ShapeDtypeStruct((n, ROUTER_LANES), F32)],
        compiler_params=_params("arbitrary"),
        name="moe_router",
    )(x, gain.reshape(1, d), w_router, b_router)


def _expert_kernel(blk_e_ref, nvalid_ref, tok_ref, tok_next_ref, dst_ref, h_hbm, wg_ref, wu_ref,
                   wd_ref, out_hbm, xbuf, ybuf, wgb, wub, wdb, sem_in, sem_out, *, n_blk, rows):
    i = pl.program_id(0)
    slot = i % 2

    def row_in(tok, r, n, s):
        return pltpu.make_async_copy(h_hbm.at[pl.ds(tok, n), :], xbuf.at[s, pl.ds(r, n), :],
                                     sem_in.at[s])

    def row_out(dst, r, n, s):
        return pltpu.make_async_copy(ybuf.at[s, pl.ds(r, n), :], out_hbm.at[pl.ds(dst, n), :],
                                     sem_out.at[s])

    def for_rows(count, fn):
        groups = lax.shift_right_logical(count, DMA_UNROLL.bit_length() - 1)

        def group(g, c):
            r0 = pl.multiple_of(g * DMA_UNROLL, DMA_UNROLL)
            for u in range(DMA_UNROLL):
                fn(r0 + u)
            return c

        def single(r, c):
            fn(r)
            return c

        lax.fori_loop(0, groups, group, 0)
        lax.fori_loop(groups * DMA_UNROLL, count, single, 0)

    def wait_rows(make, count, s):
        for bit in reversed(range(rows.bit_length())):
            size = 1 << bit

            @pl.when((count & size) != 0)
            def _(size=size):
                make(0, 0, size, s).wait()

    @pl.when(i == 0)
    def _():
        xbuf[...] = jnp.zeros_like(xbuf)
        for_rows(nvalid_ref[0], lambda r: row_in(tok_ref[0, 0, r], r, 1, 0).start())

    @pl.when(i + 1 < n_blk)
    def _():
        for_rows(nvalid_ref[i + 1], lambda r: row_in(tok_next_ref[0, 0, r], r, 1, 1 - slot).start())

    nv = nvalid_ref[i]
    wait_rows(row_in, nv, slot)

    @pl.when((i == 0) | (blk_e_ref[i] != blk_e_ref[jnp.maximum(i - 1, 0)]))
    def _():
        wgb[...] = wg_ref[...].astype(BF16)
        wub[...] = wu_ref[...].astype(BF16)
        wdb[...] = wd_ref[...].astype(BF16)

    @pl.when(i >= 2)
    def _():
        wait_rows(row_out, nvalid_ref[jnp.maximum(i - 2, 0)], slot)

    @pl.when(nv > 0)
    def _():
        x = xbuf[slot].astype(BF16)
        g = jnp.dot(x, wgb[...], preferred_element_type=F32)
        u = jnp.dot(x, wub[...], preferred_element_type=F32)
        mid = (jax.nn.silu(g) * u).astype(BF16)
        ybuf[slot] = jnp.dot(mid, wdb[...], preferred_element_type=F32)
        for_rows(nv, lambda r: row_out(dst_ref[0, 0, r], r, 1, slot).start())

    @pl.when(i == n_blk - 1)
    def _():
        wait_rows(row_out, nv, slot)
        if n_blk >= 2:
            wait_rows(row_out, nvalid_ref[n_blk - 2], 1 - slot)


def expert_mlp(h, blk_e, nvalid, tok, dst, w_gate, w_up, w_down, layer):
    n, d = h.shape
    n_blk, _, r = tok.shape
    ff = w_gate.shape[3]

    def smem_rows(index_map):
        return pl.BlockSpec((1, 1, r), index_map, memory_space=pltpu.SMEM)

    def weight(shape):
        return pl.BlockSpec((None, None) + shape, lambda i, be, nv: (layer, be[i], 0, 0))

    grid_spec = pltpu.PrefetchScalarGridSpec(
        num_scalar_prefetch=2,
        grid=(n_blk,),
        in_specs=[
            smem_rows(lambda i, be, nv: (i, 0, 0)),
            smem_rows(lambda i, be, nv: (jnp.minimum(i + 1, n_blk - 1), 0, 0)),
            smem_rows(lambda i, be, nv: (i, 0, 0)),
            pl.BlockSpec(memory_space=pl.ANY),
            weight((d, ff)), weight((d, ff)), weight((ff, d)),
        ],
        out_specs=pl.BlockSpec(memory_space=pl.ANY),
        scratch_shapes=[
            pltpu.VMEM((2, r, d), F32),
            pltpu.VMEM((2, r, d), F32),
            pltpu.VMEM((d, ff), BF16),
            pltpu.VMEM((d, ff), BF16),
            pltpu.VMEM((ff, d), BF16),
            pltpu.SemaphoreType.DMA((2,)),
            pltpu.SemaphoreType.DMA((2,)),
        ],
    )
    return pl.pallas_call(
        functools.partial(_expert_kernel, n_blk=n_blk, rows=r),
        grid_spec=grid_spec,
        out_shape=jax.ShapeDtypeStruct((TOP_K_IN_GROUP * n, d), F32),
        compiler_params=_params("arbitrary"),
        name="expert_mlp",
    )(blk_e, nvalid, tok, tok, dst, h, w_gate, w_up, w_down)


def _combine_kernel(x_ref, y0_ref, y1_ref, w_ref, o_ref):
    w = w_ref[...]
    o_ref[...] = x_ref[...] + (w[:, 0:1] * y0_ref[...] + w[:, 1:2] * y1_ref[...])


def moe_combine(x, y, wts, *, tm=512):
    n, d = x.shape
    tm = min(tm, n)
    nt = n // tm
    return pl.pallas_call(
        _combine_kernel,
        grid=(nt,),
        in_specs=[pl.BlockSpec((tm, d), lambda i: (i, 0)),
                  pl.BlockSpec((tm, d), lambda i: (i, 0)),
                  pl.BlockSpec((tm, d), lambda i: (nt + i, 0)),
                  pl.BlockSpec((tm, ROUTER_LANES), lambda i: (i, 0))],
        out_specs=pl.BlockSpec((tm, d), lambda i: (i, 0)),
        out_shape=jax.ShapeDtypeStruct((n, d), F32),
        compiler_params=_params("arbitrary"),
        name="moe_combine",
    )(x, y, y, wts)


def _expert_row_index(expert_id, rows_per_block):
    n, k = expert_id.shape
    a_total = n * k
    r = rows_per_block
    flat_e = expert_id.reshape(-1)
    order = jnp.argsort(flat_e).astype(jnp.int32)
    experts = jnp.arange(N_EXPERTS, dtype=jnp.int32)
    counts = jnp.sum((flat_e[:, None] == experts[None, :]).astype(jnp.int32), axis=0)
    padded = (counts + r - 1) // r * r
    pad_end = jnp.cumsum(padded)
    pad_start = pad_end - padded
    start = jnp.cumsum(counts) - counts
    n_blk = (a_total + N_EXPERTS * (r - 1) + r - 1) // r
    blk_row0 = jnp.arange(n_blk, dtype=jnp.int32) * r
    blk_e = jnp.minimum(jnp.sum((pad_end[None, :] <= blk_row0[:, None]).astype(jnp.int32), axis=1),
                        N_EXPERTS - 1)
    in_e = blk_row0 - pad_start[blk_e]
    nvalid = jnp.clip(counts[blk_e] - in_e, 0, r).astype(jnp.int32)
    idx = (start[blk_e] + in_e)[:, None] + jnp.arange(r, dtype=jnp.int32)[None, :]
    src = order[jnp.clip(idx, 0, a_total - 1)]
    tok = src // k
    dst = (src % k) * n + tok
    return blk_e, nvalid, tok.reshape(n_blk, 1, r), dst.reshape(n_blk, 1, r)


def hier_moe(x, gain, w_group, b_group, w_expert, b_expert, w_gate, w_up, w_down, layer):
    n, d = x.shape
    pad = ROUTER_LANES - N_GROUPS - N_EXPERTS
    w_router = jnp.concatenate([w_group, w_expert, jnp.zeros((d, pad), F32)], axis=1)
    b_router = jnp.concatenate([b_group, b_expert, jnp.zeros((pad,), F32)]).reshape(1, ROUTER_LANES)
    h, ids, wts = moe_router(x, gain, w_router, b_router)
    blk_e, nvalid, tok, dst = _expert_row_index(ids[:, :TOP_K_IN_GROUP], EXPERT_ROWS)
    y = expert_mlp(h, blk_e, nvalid, tok, dst, w_gate, w_up, w_down, layer)
    return moe_combine(x, y, wts)


def _rope_tables(t):
    inv_freq = ROPE_THETA ** (-jnp.arange(0, HEAD_DIM, 2, dtype=F32) / HEAD_DIM)
    ang = jnp.arange(t, dtype=F32)[:, None] * inv_freq[None, :]
    cos, sin = jnp.cos(ang), jnp.sin(ang)
    return jnp.concatenate([cos, cos], axis=1), jnp.concatenate([-sin, sin], axis=1)


def kernel(x, norm_mix, norm_ffn, a_w_in, a_lb_logits, a_o_gain, a_w_out, kv_norm, kv_w, k_norm,
           b_w_q, b_q_norm, b_w_out, moe_w_group, moe_b_group, moe_w_expert, moe_b_expert,
           moe_w_gate, moe_w_up, moe_w_down):
    b, t, d = x.shape
    n = b * t
    depth = norm_mix.shape[0]
    n_a = a_w_in.shape[0]
    lower_bounds = jnp.cumsum(jax.nn.softmax(a_lb_logits.astype(F32), axis=0), axis=0)
    cos, sin = _rope_tables(t)
    xf = x.reshape(n, d)
    kv = None
    for l in range(depth):
        if l < n_a:
            z = rms_matmul(xf, norm_mix[l], a_w_in, l, F32)
            o = hgrn2_scan(z.reshape(b, t, 4 * d), lower_bounds[l], a_o_gain[l])
            xf = matmul_residual(o.reshape(n, d), a_w_out, l, xf)
        else:
            if l == n_a:
                kv = rms_matmul(xf, kv_norm, kv_w[None], 0, BF16,
                                rope=(k_norm, cos, sin, d), seq_len=t)
            lb_ = l - n_a
            q = rms_matmul(xf, norm_mix[l], b_w_q, lb_, BF16,
                           rope=(b_q_norm[lb_], cos, sin, d), seq_len=t)
            o = moba_attention(q.reshape(b, t, d), kv.reshape(b, t, 2 * d))
            xf = matmul_residual(o.reshape(n, d), b_w_out, lb_, xf)
        xf = hier_moe(xf, norm_ffn[l], moe_w_group[l], moe_b_group[l], moe_w_expert[l],
                      moe_b_expert[l], moe_w_gate, moe_w_up, moe_w_down, l)
    return xf.reshape(b, t, d)
```

```python
import functools

import jax
import jax.numpy as jnp
from jax import lax
from jax.experimental import pallas as pl
from jax.experimental.pallas import tpu as pltpu

HEAD_DIM = 128
HGRN_CHUNK = 64
HGRN_SUB = 8
HGRN_UNROLL = 8
MOBA_BLOCK = 256
MOBA_TOP_K = 3
ROPE_THETA = 10000.0
N_GROUPS = 4
EXPERTS_PER_GROUP = 8
N_EXPERTS = N_GROUPS * EXPERTS_PER_GROUP
TOP_K_IN_GROUP = 2
RMS_EPS = 1e-6
ROUTER_LANES = 128
EXPERT_ROWS = 256
DMA_UNROLL = 8

V7X_VMEM_LIMIT = 56 * 1024 * 1024

F32 = jnp.float32
BF16 = jnp.bfloat16
HIGHEST = lax.Precision.HIGHEST
NT_DIMS = (((1,), (1,)), ((), ()))


def _params(*sem):
    return pltpu.CompilerParams(dimension_semantics=sem, vmem_limit_bytes=V7X_VMEM_LIMIT)


def _rms_rows_to(x_ref, g_ref, out_ref, rows, chunk):
    gain = g_ref[...]

    def body(c, carry):
        sl = pl.ds(pl.multiple_of(c * chunk, chunk), chunk)
        x = x_ref[sl, :]
        ms = jnp.mean(x * x, axis=-1, keepdims=True)
        out_ref[sl, :] = (x * lax.rsqrt(ms + RMS_EPS) * gain).astype(out_ref.dtype)
        return carry

    lax.fori_loop(0, rows // chunk, body, 0)


def _head_norm_rope(y, hg, cos, sin):
    ms = jnp.mean(y * y, axis=-1, keepdims=True)
    yn = y * lax.rsqrt(ms + RMS_EPS) * hg
    return yn * cos + pltpu.roll(yn, HEAD_DIM // 2, axis=1) * sin


def _rms_matmul_kernel(*refs, tm, tn, n_rope_tiles):
    if n_rope_tiles:
        x_ref, g_ref, w_ref, hg_ref, cos_ref, sin_ref, o_ref, hn_ref = refs
    else:
        x_ref, g_ref, w_ref, o_ref, hn_ref = refs
    j = pl.program_id(1)

    @pl.when(j == 0)
    def _():
        _rms_rows_to(x_ref, g_ref, hn_ref, tm, 64)

    y = jnp.dot(hn_ref[...], w_ref[...].astype(BF16), preferred_element_type=F32)

    if not n_rope_tiles:
        o_ref[...] = y.astype(o_ref.dtype)
        return

    @pl.when(j < n_rope_tiles)
    def _():
        hg = hg_ref[...]
        cos = cos_ref[...]
        sin = sin_ref[...]
        for h in range(tn // HEAD_DIM):
            sl = slice(h * HEAD_DIM, (h + 1) * HEAD_DIM)
            o_ref[:, sl] = _head_norm_rope(y[:, sl], hg, cos, sin).astype(o_ref.dtype)

    @pl.when(j >= n_rope_tiles)
    def _():
        o_ref[...] = y.astype(o_ref.dtype)


def rms_matmul(x, gain, w, layer, out_dtype, *, rope=None, seq_len=None, tm=1024, tn=512):
    n, d = x.shape
    e = w.shape[2]
    tm = min(tm, n)
    tn = min(tn, e)
    in_specs = [
        pl.BlockSpec((tm, d), lambda i, j: (i, 0)),
        pl.BlockSpec((1, d), lambda i, j: (0, 0)),
        pl.BlockSpec((None, d, tn), lambda i, j: (layer, 0, j)),
    ]
    args = [x, gain.reshape(1, d), w]
    n_rope_tiles = 0
    if rope is not None:
        hg, cos, sin, n_rope_cols = rope
        tm = min(tm, seq_len)
        tn = min(tn, n_rope_cols)
        in_specs[0] = pl.BlockSpec((tm, d), lambda i, j: (i, 0))
        in_specs[2] = pl.BlockSpec((None, d, tn), lambda i, j: (layer, 0, j))
        n_rope_tiles = n_rope_cols // tn
        pos_tiles = seq_len // tm
        in_specs += [
            pl.BlockSpec((1, HEAD_DIM), lambda i, j: (0, 0)),
            pl.BlockSpec((tm, HEAD_DIM), lambda i, j: (i % pos_tiles, 0)),
            pl.BlockSpec((tm, HEAD_DIM), lambda i, j: (i % pos_tiles, 0)),
        ]
        args += [hg.reshape(1, HEAD_DIM), cos, sin]
    return pl.pallas_call(
        functools.partial(_rms_matmul_kernel, tm=tm, tn=tn, n_rope_tiles=n_rope_tiles),
        grid=(n // tm, e // tn),
        in_specs=in_specs,
        out_specs=pl.BlockSpec((tm, tn), lambda i, j: (i, j)),
        out_shape=jax.ShapeDtypeStruct((n, e), out_dtype),
        scratch_shapes=[pltpu.VMEM((tm, d), BF16)],
        compiler_params=_params("arbitrary", "arbitrary"),
        name="rms_matmul",
    )(*args)


def _matmul_res_kernel(a_ref, w_ref, r_ref, o_ref):
    o_ref[...] = r_ref[...] + jnp.dot(a_ref[...], w_ref[...].astype(BF16),
                                      preferred_element_type=F32)


def matmul_residual(a, w, layer, res, *, tm=1024, tn=512):
    n, d = a.shape
    e = w.shape[2]
    tm = min(tm, n)
    tn = min(tn, e)
    return pl.pallas_call(
        _matmul_res_kernel,
        grid=(n // tm, e // tn),
        in_specs=[
            pl.BlockSpec((tm, d), lambda i, j: (i, 0)),
            pl.BlockSpec((None, d, tn), lambda i, j: (layer, 0, j)),
            pl.BlockSpec((tm, tn), lambda i, j: (i, j)),
        ],
        out_specs=pl.BlockSpec((tm, tn), lambda i, j: (i, j)),
        out_shape=jax.ShapeDtypeStruct((n, e), F32),
        compiler_params=_params("arbitrary", "arbitrary"),
        name="matmul_residual",
    )(a, w, res)


def _hgrn_kernel(zq_ref, zf_ref, zi_ref, zg_ref, lb_ref, og_ref, o_ref,
                 st_ref, g_scr, h_scr, a_scr, u_scr, *, n_groups):
    C = HGRN_CHUNK
    SB = HGRN_SUB
    NB = C // SB

    @pl.when(pl.program_id(2) == 0)
    def _():
        st_ref[...] = jnp.zeros_like(st_ref)

    lb = lb_ref[...]
    og = og_ref[...]
    row_c = lax.broadcasted_iota(jnp.int32, (C, C), 0)
    col_c = lax.broadcasted_iota(jnp.int32, (C, C), 1)
    causal = row_c >= col_c
    tril = causal.astype(F32)
    lane_s = lax.broadcasted_iota(jnp.int32, (SB, C), 1)
    zero_blk = jnp.zeros((SB, HEAD_DIM), F32)
    halves = [h for h in (32, 16, 8) if SB <= h < C]
    level_mask = {half: ((row_c // (2 * half) == col_c // (2 * half))
                         & ((row_c // half) % 2 == 1) & ((col_c // half) % 2 == 0))
                  for half in halves}

    def blk(x, b):
        return x[b * SB:(b + 1) * SB, :]

    def gates(rows, u):
        f = lb + (1.0 - lb) * jax.nn.sigmoid(zf_ref[0, rows, :])
        G2 = jnp.dot(tril, jnp.log2(f), precision=HIGHEST, preferred_element_type=F32)
        g_scr[u] = G2
        h_scr[u] = G2 - jnp.log2(1.0 - f)

    def scores(rows, u):
        q = zq_ref[0, rows, :]
        G2 = g_scr[u]
        H2 = h_scr[u]
        A = jnp.zeros((C, C), F32)
        for half in halves:
            per_half = half // SB
            qa, kb = [], []
            for b in range(NB):
                grp = b // per_half
                if grp % 2 == 1:
                    r = grp * half - 1
                    qa.append(blk(q, b) * jnp.exp2(blk(G2, b) - g_scr[u, r:r + 1, :]))
                    kb.append(zero_blk)
                else:
                    r = (grp + 1) * half - 1
                    kb.append(jnp.exp2(g_scr[u, r:r + 1, :] - blk(H2, b)))
                    qa.append(zero_blk)
            sq = lax.dot_general(jnp.concatenate(qa, axis=0).astype(BF16),
                                 jnp.concatenate(kb, axis=0).astype(BF16), NT_DIMS,
                                 preferred_element_type=F32)
            A = A + jnp.where(level_mask[half], sq, 0.0)
        a_parts = []
        for b in range(NB):
            Gt = blk(G2, b)
            qt = blk(q, b)
            a_b = jnp.zeros((SB, C), F32)
            for s in range(b * SB, (b + 1) * SB):
                e = jnp.exp2(Gt - h_scr[u, s:s + 1, :])
                a_b = jnp.where(lane_s == s, jnp.sum(qt * e, axis=-1, keepdims=True), a_b)
            a_parts.append(a_b)
        A = jnp.where(causal, A + jnp.concatenate(a_parts, axis=0), 0.0)
        a_scr[u] = A.astype(BF16)

    def state_input(rows, u):
        G2_end = g_scr[u, C - 1:C, :]
        kd = jnp.exp2(G2_end - h_scr[u])
        u_scr[u] = jnp.dot(zi_ref[0, rows, :].T.astype(BF16), kd.astype(BF16),
                           preferred_element_type=F32)

    def output(rows, u):
        G2 = g_scr[u]
        st = st_ref[...]
        qg = (zq_ref[0, rows, :] * jnp.exp2(G2)).astype(BF16)
        o = (lax.dot_general(qg, st.astype(BF16), NT_DIMS, preferred_element_type=F32)
             + jnp.dot(a_scr[u], zi_ref[0, rows, :].astype(BF16), preferred_element_type=F32))
        st_ref[...] = jnp.exp2(G2[C - 1:C, :]) * st + u_scr[u]
        ms = jnp.mean(o * o, axis=-1, keepdims=True)
        zg = zg_ref[0, rows, :]
        o_ref[0, rows, :] = (o * lax.rsqrt(ms + RMS_EPS) * og * jax.nn.silu(zg)).astype(o_ref.dtype)

    def group(gi, carry):
        base = pl.multiple_of(gi * (HGRN_UNROLL * C), HGRN_UNROLL * C)
        for phase in (gates, scores, state_input, output):
            for u in range(HGRN_UNROLL):
                phase(pl.ds(base + u * C, C), u)
        return carry

    lax.fori_loop(0, n_groups, group, 0)


def hgrn2_scan(z, lb, o_gain, *, tt=4096):
    b, t, d4 = z.shape
    d = d4 // 4
    nh = d // HEAD_DIM
    tt = min(tt, t)

    def zspec(part):
        return pl.BlockSpec((1, tt, HEAD_DIM), lambda bi, h, ti: (bi, ti, part * nh + h))

    return pl.pallas_call(
        functools.partial(_hgrn_kernel, n_groups=tt // (HGRN_CHUNK * HGRN_UNROLL)),
        grid=(b, nh, t // tt),
        in_specs=[zspec(0), zspec(1), zspec(2), zspec(3),
                  pl.BlockSpec((1, HEAD_DIM), lambda bi, h, ti: (0, h)),
                  pl.BlockSpec((1, HEAD_DIM), lambda bi, h, ti: (0, 0))],
        out_specs=pl.BlockSpec((1, tt, HEAD_DIM), lambda bi, h, ti: (bi, ti, h)),
        out_shape=jax.ShapeDtypeStruct((b, t, d), BF16),
        scratch_shapes=[pltpu.VMEM((HEAD_DIM, HEAD_DIM), F32),
                        pltpu.VMEM((HGRN_UNROLL, HGRN_CHUNK, HEAD_DIM), F32),
                        pltpu.VMEM((HGRN_UNROLL, HGRN_CHUNK, HEAD_DIM), F32),
                        pltpu.VMEM((HGRN_UNROLL, HGRN_CHUNK, HGRN_CHUNK), BF16),
                        pltpu.VMEM((HGRN_UNROLL, HEAD_DIM, HEAD_DIM), F32)],
        compiler_params=_params("arbitrary", "arbitrary", "arbitrary"),
        name="hgrn2_scan",
    )(z, z, z, z, lb.reshape(1, d), o_gain.reshape(1, HEAD_DIM))


def _moba_kernel(q_ref, k_ref, v_ref, o_ref, km_ref, vt_ref, bias_ref, m_ref, l_ref, acc_ref,
                 *, n_blocks, heads):
    BLK = MOBA_BLOCK
    qi = pl.program_id(2)
    exp2_scale = HEAD_DIM ** -0.5 * 1.4426950408889634

    def head(h):
        return slice(h * HEAD_DIM, (h + 1) * HEAD_DIM)

    @pl.when(qi == 0)
    def _():
        r = lax.broadcasted_iota(jnp.int32, (HEAD_DIM, HEAD_DIM), 0)
        c = lax.broadcasted_iota(jnp.int32, (HEAD_DIM, HEAD_DIM), 1)
        eye = (r == c).astype(BF16)
        for h in range(heads):
            for j in range(n_blocks):
                rows = slice(j * BLK, (j + 1) * BLK)
                kb = k_ref[0, rows, head(h)].astype(F32)
                km_ref[h, j:j + 1, :] = jnp.mean(kb, axis=0, keepdims=True)
                vt_ref[h, :, rows] = lax.dot_general(eye, v_ref[0, rows, head(h)], NT_DIMS,
                                                     preferred_element_type=F32).astype(BF16)

    blk_id = lax.broadcasted_iota(jnp.int32, (n_blocks, BLK), 0)
    past = blk_id < qi
    key_i = lax.broadcasted_iota(jnp.int32, (BLK, BLK), 0)
    qry_i = lax.broadcasted_iota(jnp.int32, (BLK, BLK), 1)
    own = pl.ds(pl.multiple_of(qi * BLK, BLK), BLK)

    def scores(rows):
        return [lax.dot_general(k_ref[0, rows, head(h)], q_ref[0, :, head(h)], NT_DIMS,
                                preferred_element_type=F32) for h in range(heads)]

    for h in range(heads):
        gate = lax.dot_general(km_ref[h], q_ref[0, :, head(h)].astype(F32), NT_DIMS,
                               precision=HIGHEST, preferred_element_type=F32)
        gm = jnp.where(past, gate, -jnp.inf)
        cnt = jnp.zeros((n_blocks, BLK), jnp.int32)
        for m in range(n_blocks):
            gmm = gm[m:m + 1, :]
            ahead = (gmm > gm) | ((gmm == gm) & (blk_id > m))
            cnt = cnt + ahead.astype(jnp.int32)
        bias_ref[h] = jnp.where(past & (cnt < MOBA_TOP_K), 0.0, -jnp.inf).astype(F32)

    probs = []
    for h, s in enumerate(scores(own)):
        s = jnp.where(key_i <= qry_i, s, -jnp.inf)
        m0 = jnp.max(s, axis=0, keepdims=True)
        p = jnp.exp2((s - m0) * exp2_scale)
        m_ref[h] = m0
        l_ref[h] = jnp.sum(p, axis=0, keepdims=True)
        probs.append(p.astype(BF16))
    for h in range(heads):
        acc_ref[h] = jnp.dot(vt_ref[h, :, own], probs[h], preferred_element_type=F32)

    for j in range(n_blocks - 1):
        @pl.when(j < qi)
        def _(j=j):
            rows = slice(j * BLK, (j + 1) * BLK)
            probs, alphas = [], []
            for h, sj in enumerate(scores(rows)):
                sj = sj + bias_ref[h, j:j + 1, :]
                m_old = m_ref[h]
                m_new = jnp.maximum(m_old, jnp.max(sj, axis=0, keepdims=True))
                alpha = jnp.exp2((m_old - m_new) * exp2_scale)
                pj = jnp.exp2((sj - m_new) * exp2_scale)
                m_ref[h] = m_new
                l_ref[h] = alpha * l_ref[h] + jnp.sum(pj, axis=0, keepdims=True)
                probs.append(pj.astype(BF16))
                alphas.append(alpha)
            for h in range(heads):
                acc_ref[h] = alphas[h] * acc_ref[h] + jnp.dot(vt_ref[h, :, rows], probs[h],
                                                              preferred_element_type=F32)

    for h in range(heads):
        o_ref[0, :, head(h)] = (acc_ref[h] / l_ref[h]).T.astype(o_ref.dtype)


def moba_attention(q, kv, *, heads=4):
    b, t, d = q.shape
    nh = d // HEAD_DIM
    heads = min(heads, nh)
    ng = nh // heads
    nb = t // MOBA_BLOCK
    w = heads * HEAD_DIM
    return pl.pallas_call(
        functools.partial(_moba_kernel, n_blocks=nb, heads=heads),
        grid=(b, ng, nb),
        in_specs=[pl.BlockSpec((1, MOBA_BLOCK, w), lambda bi, g, qi: (bi, qi, g)),
                  pl.BlockSpec((1, t, w), lambda bi, g, qi: (bi, 0, g)),
                  pl.BlockSpec((1, t, w), lambda bi, g, qi: (bi, 0, ng + g))],
        out_specs=pl.BlockSpec((1, MOBA_BLOCK, w), lambda bi, g, qi: (bi, qi, g)),
        out_shape=jax.ShapeDtypeStruct((b, t, d), BF16),
        scratch_shapes=[pltpu.VMEM((heads, nb, HEAD_DIM), F32),
                        pltpu.VMEM((heads, HEAD_DIM, t), BF16),
                        pltpu.VMEM((heads, nb, MOBA_BLOCK), F32),
                        pltpu.VMEM((heads, 1, MOBA_BLOCK), F32),
                        pltpu.VMEM((heads, 1, MOBA_BLOCK), F32),
                        pltpu.VMEM((heads, HEAD_DIM, MOBA_BLOCK), F32)],
        compiler_params=_params("arbitrary", "arbitrary", "arbitrary"),
        name="moba_attention",
    )(q, kv, kv)


def _router_kernel(x_ref, g_ref, wr_ref, br_ref, h_ref, ids_ref, wts_ref, *, tm):
    _rms_rows_to(x_ref, g_ref, h_ref, tm, 64)
    logits = jnp.dot(h_ref[...], wr_ref[...], precision=HIGHEST,
                     preferred_element_type=F32) + br_ref[...]
    lane = lax.broadcasted_iota(jnp.int32, logits.shape, 1)
    neg = -jnp.inf
    big = ROUTER_LANES

    gl = jnp.where(lane < N_GROUPS, logits, neg)
    gmax = jnp.max(gl, axis=-1, keepdims=True)
    gsum = jnp.sum(jnp.exp(gl - gmax), axis=-1, keepdims=True)
    g_top_p = 1.0 / gsum
    g_top = jnp.min(jnp.where(gl == gmax, lane, big), axis=-1, keepdims=True)

    lo = N_GROUPS + EXPERTS_PER_GROUP * g_top
    el = jnp.where((lane >= lo) & (lane < lo + EXPERTS_PER_GROUP), logits, neg)
    emax = jnp.max(el, axis=-1, keepdims=True)
    esum = jnp.sum(jnp.exp(el - emax), axis=-1, keepdims=True)
    i1 = jnp.min(jnp.where(el == emax, lane, big), axis=-1, keepdims=True)
    el2 = jnp.where(lane == i1, neg, el)
    emax2 = jnp.max(el2, axis=-1, keepdims=True)
    i2 = jnp.min(jnp.where(el2 == emax2, lane, big), axis=-1, keepdims=True)
    p1 = 1.0 / esum
    p2 = jnp.exp(emax2 - emax) / esum
    psum = p1 + p2
    w1 = g_top_p * p1 / psum
    w2 = g_top_p * p2 / psum

    ids_ref[...] = jnp.where(lane == 0, i1 - N_GROUPS, jnp.where(lane == 1, i2 - N_GROUPS, 0))
    wts_ref[...] = jnp.where(lane == 0, w1, jnp.where(lane == 1, w2, 0.0))


def moe_router(x, gain, w_router, b_router, *, tm=512):
    n, d = x.shape
    tm = min(tm, n)
    return pl.pallas_call(
        functools.partial(_router_kernel, tm=tm),
        grid=(n // tm,),
        in_specs=[pl.BlockSpec((tm, d), lambda i: (i, 0)),
                  pl.BlockSpec((1, d), lambda i: (0, 0)),
                  pl.BlockSpec((d, ROUTER_LANES), lambda i: (0, 0)),
                  pl.BlockSpec((1, ROUTER_LANES), lambda i: (0, 0))],
        out_specs=[pl.BlockSpec((tm, d), lambda i: (i, 0)),
                   pl.BlockSpec((tm, ROUTER_LANES), lambda i: (i, 0)),
                   pl.BlockSpec((tm, ROUTER_LANES), lambda i: (i, 0))],
        out_shape=[jax.ShapeDtypeStruct((n, d), F32),
                   jax.ShapeDtypeStruct((n, ROUTER_LANES), jnp.int32),
                   jax.ShapeDtypeStruct((n, ROUTER_LANES), F32)],
        compiler_params=_params("arbitrary"),
        name="moe_router",
    )(x, gain.reshape(1, d), w_router, b_router)


def _expert_kernel(blk_e_ref, nvalid_ref, tok_ref, tok_next_ref, dst_ref, h_hbm, wg_ref, wu_ref,
                   wd_ref, out_hbm, xbuf, ybuf, wgb, wub, wdb, sem_in, sem_out, *, n_blk, rows):
    i = pl.program_id(0)
    slot = i % 2

    def row_in(tok, r, n, s):
        return pltpu.make_async_copy(h_hbm.at[pl.ds(tok, n), :], xbuf.at[s, pl.ds(r, n), :],
                                     sem_in.at[s])

    def row_out(dst, r, n, s):
        return pltpu.make_async_copy(ybuf.at[s, pl.ds(r, n), :], out_hbm.at[pl.ds(dst, n), :],
                                     sem_out.at[s])

    def for_rows(count, fn):
        groups = lax.shift_right_logical(count, DMA_UNROLL.bit_length() - 1)

        def group(g, c):
            r0 = pl.multiple_of(g * DMA_UNROLL, DMA_UNROLL)
            for u in range(DMA_UNROLL):
                fn(r0 + u)
            return c

        def single(r, c):
            fn(r)
            return c

        lax.fori_loop(0, groups, group, 0)
        lax.fori_loop(groups * DMA_UNROLL, count, single, 0)

    def wait_rows(make, count, s):
        for bit in reversed(range(rows.bit_length())):
            size = 1 << bit

            @pl.when((count & size) != 0)
            def _(size=size):
                make(0, 0, size, s).wait()

    @pl.when(i == 0)
    def _():
        xbuf[...] = jnp.zeros_like(xbuf)
        for_rows(nvalid_ref[0], lambda r: row_in(tok_ref[0, 0, r], r, 1, 0).start())

    @pl.when(i + 1 < n_blk)
    def _():
        for_rows(nvalid_ref[i + 1], lambda r: row_in(tok_next_ref[0, 0, r], r, 1, 1 - slot).start())

    nv = nvalid_ref[i]
    wait_rows(row_in, nv, slot)

    @pl.when((i == 0) | (blk_e_ref[i] != blk_e_ref[jnp.maximum(i - 1, 0)]))
    def _():
        wgb[...] = wg_ref[...].astype(BF16)
        wub[...] = wu_ref[...].astype(BF16)
        wdb[...] = wd_ref[...].astype(BF16)

    @pl.when(i >= 2)
    def _():
        wait_rows(row_out, nvalid_ref[jnp.maximum(i - 2, 0)], slot)

    @pl.when(nv > 0)
    def _():
        x = xbuf[slot].astype(BF16)
        g = jnp.dot(x, wgb[...], preferred_element_type=F32)
        u = jnp.dot(x, wub[...], preferred_element_type=F32)
        mid = (jax.nn.silu(g) * u).astype(BF16)
        ybuf[slot] = jnp.dot(mid, wdb[...], preferred_element_type=F32)
        for_rows(nv, lambda r: row_out(dst_ref[0, 0, r], r, 1, slot).start())

    @pl.when(i == n_blk - 1)
    def _():
        wait_rows(row_out, nv, slot)
        if n_blk >= 2:
            wait_rows(row_out, nvalid_ref[n_blk - 2], 1 - slot)


def expert_mlp(h, blk_e, nvalid, tok, dst, w_gate, w_up, w_down, layer):
    n, d = h.shape
    n_blk, _, r = tok.shape
    ff = w_gate.shape[3]

    def smem_rows(index_map):
        return pl.BlockSpec((1, 1, r), index_map, memory_space=pltpu.SMEM)

    def weight(shape):
        return pl.BlockSpec((None, None) + shape, lambda i, be, nv: (layer, be[i], 0, 0))

    grid_spec = pltpu.PrefetchScalarGridSpec(
        num_scalar_prefetch=2,
        grid=(n_blk,),
        in_specs=[
            smem_rows(lambda i, be, nv: (i, 0, 0)),
            smem_rows(lambda i, be, nv: (jnp.minimum(i + 1, n_blk - 1), 0, 0)),
            smem_rows(lambda i, be, nv: (i, 0, 0)),
            pl.BlockSpec(memory_space=pl.ANY),
            weight((d, ff)), weight((d, ff)), weight((ff, d)),
        ],
        out_specs=pl.BlockSpec(memory_space=pl.ANY),
        scratch_shapes=[
            pltpu.VMEM((2, r, d), F32),
            pltpu.VMEM((2, r, d), F32),
            pltpu.VMEM((d, ff), BF16),
            pltpu.VMEM((d, ff), BF16),
            pltpu.VMEM((ff, d), BF16),
            pltpu.SemaphoreType.DMA((2,)),
            pltpu.SemaphoreType.DMA((2,)),
        ],
    )
    return pl.pallas_call(
        functools.partial(_expert_kernel, n_blk=n_blk, rows=r),
        grid_spec=grid_spec,
        out_shape=jax.ShapeDtypeStruct((TOP_K_IN_GROUP * n, d), F32),
        compiler_params=_params("arbitrary"),
        name="expert_mlp",
    )(blk_e, nvalid, tok, tok, dst, h, w_gate, w_up, w_down)


def _combine_kernel(x_ref, y0_ref, y1_ref, w_ref, o_ref):
    w = w_ref[...]
    o_ref[...] = x_ref[...] + (w[:, 0:1] * y0_ref[...] + w[:, 1:2] * y1_ref[...])


def moe_combine(x, y, wts, *, tm=512):
    n, d = x.shape
    tm = min(tm, n)
    nt = n // tm
    return pl.pallas_call(
        _combine_kernel,
        grid=(nt,),
        in_specs=[pl.BlockSpec((tm, d), lambda i: (i, 0)),
                  pl.BlockSpec((tm, d), lambda i: (i, 0)),
                  pl.BlockSpec((tm, d), lambda i: (nt + i, 0)),
                  pl.BlockSpec((tm, ROUTER_LANES), lambda i: (i, 0))],
        out_specs=pl.BlockSpec((tm, d), lambda i: (i, 0)),
        out_shape=jax.ShapeDtypeStruct((n, d), F32),
        compiler_params=_params("arbitrary"),
        name="moe_combine",
    )(x, y, y, wts)


def _expert_row_index(expert_id, rows_per_block):
    n, k = expert_id.shape
    a_total = n * k
    r = rows_per_block
    flat_e = expert_id.reshape(-1)
    order = jnp.argsort(flat_e).astype(jnp.int32)
    experts = jnp.arange(N_EXPERTS, dtype=jnp.int32)
    counts = jnp.sum((flat_e[:, None] == experts[None, :]).astype(jnp.int32), axis=0)
    padded = (counts + r - 1) // r * r
    pad_end = jnp.cumsum(padded)
    pad_start = pad_end - padded
    start = jnp.cumsum(counts) - counts
    n_blk = (a_total + N_EXPERTS * (r - 1) + r - 1) // r
    blk_row0 = jnp.arange(n_blk, dtype=jnp.int32) * r
    blk_e = jnp.minimum(jnp.sum((pad_end[None, :] <= blk_row0[:, None]).astype(jnp.int32), axis=1),
                        N_EXPERTS - 1)
    in_e = blk_row0 - pad_start[blk_e]
    nvalid = jnp.clip(counts[blk_e] - in_e, 0, r).astype(jnp.int32)
    idx = (start[blk_e] + in_e)[:, None] + jnp.arange(r, dtype=jnp.int32)[None, :]
    src = order[jnp.clip(idx, 0, a_total - 1)]
    tok = src // k
    dst = (src % k) * n + tok
    return blk_e, nvalid, tok.reshape(n_blk, 1, r), dst.reshape(n_blk, 1, r)


def hier_moe(x, gain, w_group, b_group, w_expert, b_expert, w_gate, w_up, w_down, layer):
    n, d = x.shape
    pad = ROUTER_LANES - N_GROUPS - N_EXPERTS
    w_router = jnp.concatenate([w_group, w_expert, jnp.zeros((d, pad), F32)], axis=1)
    b_router = jnp.concatenate([b_group, b_expert, jnp.zeros((pad,), F32)]).reshape(1, ROUTER_LANES)
    h, ids, wts = moe_router(x, gain, w_router, b_router)
    blk_e, nvalid, tok, dst = _expert_row_index(ids[:, :TOP_K_IN_GROUP], EXPERT_ROWS)
    y = expert_mlp(h, blk_e, nvalid, tok, dst, w_gate, w_up, w_down, layer)
    return moe_combine(x, y, wts)


def _rope_tables(t):
    inv_freq = ROPE_THETA ** (-jnp.arange(0, HEAD_DIM, 2, dtype=F32) / HEAD_DIM)
    ang = jnp.arange(t, dtype=F32)[:, None] * inv_freq[None, :]
    cos, sin = jnp.cos(ang), jnp.sin(ang)
    return jnp.concatenate([cos, cos], axis=1), jnp.concatenate([-sin, sin], axis=1)


def kernel(x, norm_mix, norm_ffn, a_w_in, a_lb_logits, a_o_gain, a_w_out, kv_norm, kv_w, k_norm,
           b_w_q, b_q_norm, b_w_out, moe_w_group, moe_b_group, moe_w_expert, moe_b_expert,
           moe_w_gate, moe_w_up, moe_w_down):
    b, t, d = x.shape
    n = b * t
    depth = norm_mix.shape[0]
    n_a = a_w_in.shape[0]
    lower_bounds = jnp.cumsum(jax.nn.softmax(a_lb_logits.astype(F32), axis=0), axis=0)
    cos, sin = _rope_tables(t)
    xf = x.reshape(n, d)
    kv = None
    for l in range(depth):
        if l < n_a:
            z = rms_matmul(xf, norm_mix[l], a_w_in, l, F32)
            o = hgrn2_scan(z.reshape(b, t, 4 * d), lower_bounds[l], a_o_gain[l])
            xf = matmul_residual(o.reshape(n, d), a_w_out, l, xf)
        else:
            if l == n_a:
                kv = rms_matmul(xf, kv_norm, kv_w[None], 0, BF16,
                                rope=(k_norm, cos, sin, d), seq_len=t)
            lb_ = l - n_a
            q = rms_matmul(xf, norm_mix[l], b_w_q, lb_, BF16,
                           rope=(b_q_norm[lb_], cos, sin, d), seq_len=t)
            o = moba_attention(q.reshape(b, t, d), kv.reshape(b, t, 2 * d))
            xf = matmul_residual(o.reshape(n, d), b_w_out, lb_, xf)
        xf = hier_moe(xf, norm_ffn[l], moe_w_group[l], moe_b_group[l], moe_w_expert[l],
                      moe_b_expert[l], moe_w_gate, moe_w_up, moe_w_down, l)
    return xf.reshape(b, t, d)
```

```python
import functools

import jax
import jax.numpy as jnp
from jax import lax
from jax.experimental import pallas as pl
from jax.experimental.pallas import tpu as pltpu

HEAD_DIM = 128
HGRN_CHUNK = 64
HGRN_SUB = 8
HGRN_UNROLL = 8
MOBA_BLOCK = 256
MOBA_TOP_K = 3
ROPE_THETA = 10000.0
N_GROUPS = 4
EXPERTS_PER_GROUP = 8
N_EXPERTS = N_GROUPS * EXPERTS_PER_GROUP
TOP_K_IN_GROUP = 2
RMS_EPS = 1e-6
ROUTER_LANES = 128
EXPERT_ROWS = 256

V7X_VMEM_LIMIT = 56 * 1024 * 1024

F32 = jnp.float32
BF16 = jnp.bfloat16
HIGHEST = lax.Precision.HIGHEST
NT_DIMS = (((1,), (1,)), ((), ()))


def _params(*sem):
    return pltpu.CompilerParams(dimension_semantics=sem, vmem_limit_bytes=V7X_VMEM_LIMIT)


def _rms_rows_to(x_ref, g_ref, out_ref, rows, chunk):
    gain = g_ref[...]

    def body(c, carry):
        sl = pl.ds(pl.multiple_of(c * chunk, chunk), chunk)
        x = x_ref[sl, :]
        ms = jnp.mean(x * x, axis=-1, keepdims=True)
        out_ref[sl, :] = (x * lax.rsqrt(ms + RMS_EPS) * gain).astype(out_ref.dtype)
        return carry

    lax.fori_loop(0, rows // chunk, body, 0)


def _head_norm_rope(y, hg, cos, sin):
    ms = jnp.mean(y * y, axis=-1, keepdims=True)
    yn = y * lax.rsqrt(ms + RMS_EPS) * hg
    return yn * cos + pltpu.roll(yn, HEAD_DIM // 2, axis=1) * sin


def _rms_matmul_kernel(*refs, tm, tn, n_rope_tiles):
    if n_rope_tiles:
        x_ref, g_ref, w_ref, hg_ref, cos_ref, sin_ref, o_ref, hn_ref = refs
    else:
        x_ref, g_ref, w_ref, o_ref, hn_ref = refs
    j = pl.program_id(1)

    @pl.when(j == 0)
    def _():
        _rms_rows_to(x_ref, g_ref, hn_ref, tm, 64)

    y = jnp.dot(hn_ref[...], w_ref[...].astype(BF16), preferred_element_type=F32)

    if not n_rope_tiles:
        o_ref[...] = y.astype(o_ref.dtype)
        return

    @pl.when(j < n_rope_tiles)
    def _():
        hg = hg_ref[...]
        cos = cos_ref[...]
        sin = sin_ref[...]
        for h in range(tn // HEAD_DIM):
            sl = slice(h * HEAD_DIM, (h + 1) * HEAD_DIM)
            o_ref[:, sl] = _head_norm_rope(y[:, sl], hg, cos, sin).astype(o_ref.dtype)

    @pl.when(j >= n_rope_tiles)
    def _():
        o_ref[...] = y.astype(o_ref.dtype)


def rms_matmul(x, gain, w, layer, out_dtype, *, rope=None, seq_len=None, tm=1024, tn=512):
    n, d = x.shape
    e = w.shape[2]
    tm = min(tm, n)
    tn = min(tn, e)
    in_specs = [
        pl.BlockSpec((tm, d), lambda i, j: (i, 0)),
        pl.BlockSpec((1, d), lambda i, j: (0, 0)),
        pl.BlockSpec((None, d, tn), lambda i, j: (layer, 0, j)),
    ]
    args = [x, gain.reshape(1, d), w]
    n_rope_tiles = 0
    if rope is not None:
        hg, cos, sin, n_rope_cols = rope
        tm = min(tm, seq_len)
        tn = min(tn, n_rope_cols)
        in_specs[0] = pl.BlockSpec((tm, d), lambda i, j: (i, 0))
        in_specs[2] = pl.BlockSpec((None, d, tn), lambda i, j: (layer, 0, j))
        n_rope_tiles = n_rope_cols // tn
        pos_tiles = seq_len // tm
        in_specs += [
            pl.BlockSpec((1, HEAD_DIM), lambda i, j: (0, 0)),
            pl.BlockSpec((tm, HEAD_DIM), lambda i, j: (i % pos_tiles, 0)),
            pl.BlockSpec((tm, HEAD_DIM), lambda i, j: (i % pos_tiles, 0)),
        ]
        args += [hg.reshape(1, HEAD_DIM), cos, sin]
    return pl.pallas_call(
        functools.partial(_rms_matmul_kernel, tm=tm, tn=tn, n_rope_tiles=n_rope_tiles),
        grid=(n // tm, e // tn),
        in_specs=in_specs,
        out_specs=pl.BlockSpec((tm, tn), lambda i, j: (i, j)),
        out_shape=jax.ShapeDtypeStruct((n, e), out_dtype),
        scratch_shapes=[pltpu.VMEM((tm, d), BF16)],
        compiler_params=_params("arbitrary", "arbitrary"),
        name="rms_matmul",
    )(*args)


def _matmul_res_kernel(a_ref, w_ref, r_ref, o_ref):
    o_ref[...] = r_ref[...] + jnp.dot(a_ref[...], w_ref[...].astype(BF16),
                                      preferred_element_type=F32)


def matmul_residual(a, w, layer, res, *, tm=1024, tn=512):
    n, d = a.shape
    e = w.shape[2]
    tm = min(tm, n)
    tn = min(tn, e)
    return pl.pallas_call(
        _matmul_res_kernel,
        grid=(n // tm, e // tn),
        in_specs=[
            pl.BlockSpec((tm, d), lambda i, j: (i, 0)),
            pl.BlockSpec((None, d, tn), lambda i, j: (layer, 0, j)),
            pl.BlockSpec((tm, tn), lambda i, j: (i, j)),
        ],
        out_specs=pl.BlockSpec((tm, tn), lambda i, j: (i, j)),
        out_shape=jax.ShapeDtypeStruct((n, e), F32),
        compiler_params=_params("arbitrary", "arbitrary"),
        name="matmul_residual",
    )(a, w, res)


def _hgrn_kernel(zq_ref, zf_ref, zi_ref, zg_ref, lb_ref, og_ref, o_ref,
                 st_ref, g_scr, h_scr, a_scr, u_scr, *, n_groups):
    C = HGRN_CHUNK
    SB = HGRN_SUB
    NB = C // SB

    @pl.when(pl.program_id(2) == 0)
    def _():
        st_ref[...] = jnp.zeros_like(st_ref)

    lb = lb_ref[...]
    og = og_ref[...]
    row_c = lax.broadcasted_iota(jnp.int32, (C, C), 0)
    col_c = lax.broadcasted_iota(jnp.int32, (C, C), 1)
    causal = row_c >= col_c
    tril = causal.astype(F32)
    lane_s = lax.broadcasted_iota(jnp.int32, (SB, C), 1)
    zero_blk = jnp.zeros((SB, HEAD_DIM), F32)
    halves = [h for h in (32, 16, 8) if SB <= h < C]
    level_mask = {half: ((row_c // (2 * half) == col_c // (2 * half))
                         & ((row_c // half) % 2 == 1) & ((col_c // half) % 2 == 0))
                  for half in halves}

    def blk(x, b):
        return x[b * SB:(b + 1) * SB, :]

    def gates(rows, u):
        f = lb + (1.0 - lb) * jax.nn.sigmoid(zf_ref[0, rows, :])
        G2 = jnp.dot(tril, jnp.log2(f), precision=HIGHEST, preferred_element_type=F32)
        g_scr[u] = G2
        h_scr[u] = G2 - jnp.log2(1.0 - f)

    def scores(rows, u):
        q = zq_ref[0, rows, :]
        G2 = g_scr[u]
        H2 = h_scr[u]
        A = jnp.zeros((C, C), F32)
        for half in halves:
            per_half = half // SB
            qa, kb = [], []
            for b in range(NB):
                grp = b // per_half
                if grp % 2 == 1:
                    r = grp * half - 1
                    qa.append(blk(q, b) * jnp.exp2(blk(G2, b) - g_scr[u, r:r + 1, :]))
                    kb.append(zero_blk)
                else:
                    r = (grp + 1) * half - 1
                    kb.append(jnp.exp2(g_scr[u, r:r + 1, :] - blk(H2, b)))
                    qa.append(zero_blk)
            sq = lax.dot_general(jnp.concatenate(qa, axis=0).astype(BF16),
                                 jnp.concatenate(kb, axis=0).astype(BF16), NT_DIMS,
                                 preferred_element_type=F32)
            A = A + jnp.where(level_mask[half], sq, 0.0)
        a_parts = []
        for b in range(NB):
            Gt = blk(G2, b)
            qt = blk(q, b)
            a_b = jnp.zeros((SB, C), F32)
            for s in range(b * SB, (b + 1) * SB):
                e = jnp.exp2(Gt - h_scr[u, s:s + 1, :])
                a_b = jnp.where(lane_s == s, jnp.sum(qt * e, axis=-1, keepdims=True), a_b)
            a_parts.append(a_b)
        A = jnp.where(causal, A + jnp.concatenate(a_parts, axis=0), 0.0)
        a_scr[u] = A.astype(BF16)

    def state_input(rows, u):
        G2_end = g_scr[u, C - 1:C, :]
        kd = jnp.exp2(G2_end - h_scr[u])
        u_scr[u] = jnp.dot(zi_ref[0, rows, :].T.astype(BF16), kd.astype(BF16),
                           preferred_element_type=F32)

    def output(rows, u):
        G2 = g_scr[u]
        st = st_ref[...]
        qg = (zq_ref[0, rows, :] * jnp.exp2(G2)).astype(BF16)
        o = (lax.dot_general(qg, st.astype(BF16), NT_DIMS, preferred_element_type=F32)
             + jnp.dot(a_scr[u], zi_ref[0, rows, :].astype(BF16), preferred_element_type=F32))
        st_ref[...] = jnp.exp2(G2[C - 1:C, :]) * st + u_scr[u]
        ms = jnp.mean(o * o, axis=-1, keepdims=True)
        zg = zg_ref[0, rows, :]
        o_ref[0, rows, :] = (o * lax.rsqrt(ms + RMS_EPS) * og * jax.nn.silu(zg)).astype(o_ref.dtype)

    def group(gi, carry):
        base = pl.multiple_of(gi * (HGRN_UNROLL * C), HGRN_UNROLL * C)
        for phase in (gates, scores, state_input, output):
            for u in range(HGRN_UNROLL):
                phase(pl.ds(base + u * C, C), u)
        return carry

    lax.fori_loop(0, n_groups, group, 0)


def hgrn2_scan(z, lb, o_gain, *, tt=4096):
    b, t, d4 = z.shape
    d = d4 // 4
    nh = d // HEAD_DIM
    tt = min(tt, t)

    def zspec(part):
        return pl.BlockSpec((1, tt, HEAD_DIM), lambda bi, h, ti: (bi, ti, part * nh + h))

    return pl.pallas_call(
        functools.partial(_hgrn_kernel, n_groups=tt // (HGRN_CHUNK * HGRN_UNROLL)),
        grid=(b, nh, t // tt),
        in_specs=[zspec(0), zspec(1), zspec(2), zspec(3),
                  pl.BlockSpec((1, HEAD_DIM), lambda bi, h, ti: (0, h)),
                  pl.BlockSpec((1, HEAD_DIM), lambda bi, h, ti: (0, 0))],
        out_specs=pl.BlockSpec((1, tt, HEAD_DIM), lambda bi, h, ti: (bi, ti, h)),
        out_shape=jax.ShapeDtypeStruct((b, t, d), BF16),
        scratch_shapes=[pltpu.VMEM((HEAD_DIM, HEAD_DIM), F32),
                        pltpu.VMEM((HGRN_UNROLL, HGRN_CHUNK, HEAD_DIM), F32),
                        pltpu.VMEM((HGRN_UNROLL, HGRN_CHUNK, HEAD_DIM), F32),
                        pltpu.VMEM((HGRN_UNROLL, HGRN_CHUNK, HGRN_CHUNK), BF16),
                        pltpu.VMEM((HGRN_UNROLL, HEAD_DIM, HEAD_DIM), F32)],
        compiler_params=_params("arbitrary", "arbitrary", "arbitrary"),
        name="hgrn2_scan",
    )(z, z, z, z, lb.reshape(1, d), o_gain.reshape(1, HEAD_DIM))


def _moba_kernel(q_ref, k_ref, v_ref, o_ref, km_ref, vt_ref, qt_ref, bias_ref, m_ref, l_ref, acc_ref,
                 *, n_blocks, heads):
    BLK = MOBA_BLOCK
    qi = pl.program_id(2)
    exp2_scale = HEAD_DIM ** -0.5 * 1.4426950408889634

    def head(h):
        return slice(h * HEAD_DIM, (h + 1) * HEAD_DIM)

    eye = (lax.broadcasted_iota(jnp.int32, (HEAD_DIM, HEAD_DIM), 0)
           == lax.broadcasted_iota(jnp.int32, (HEAD_DIM, HEAD_DIM), 1)).astype(BF16)

    @pl.when(qi == 0)
    def _():
        for h in range(heads):
            for j in range(n_blocks):
                rows = slice(j * BLK, (j + 1) * BLK)
                kb = k_ref[0, rows, head(h)].astype(F32)
                km_ref[h, j:j + 1, :] = jnp.mean(kb, axis=0, keepdims=True)
                vt_ref[h, :, rows] = lax.dot_general(eye, v_ref[0, rows, head(h)], NT_DIMS,
                                                     preferred_element_type=F32).astype(BF16)

    blk_id = lax.broadcasted_iota(jnp.int32, (n_blocks, BLK), 0)
    past = blk_id < qi
    key_i = lax.broadcasted_iota(jnp.int32, (BLK, BLK), 0)
    qry_i = lax.broadcasted_iota(jnp.int32, (BLK, BLK), 1)
    own = pl.ds(pl.multiple_of(qi * BLK, BLK), BLK)

    def scores(rows):
        return [jnp.dot(k_ref[0, rows, head(h)], qt_ref[h], preferred_element_type=F32)
                for h in range(heads)]

    for h in range(heads):
        qt_ref[h] = lax.dot_general(eye, q_ref[0, :, head(h)], NT_DIMS,
                                    preferred_element_type=F32).astype(BF16)
    for h in range(heads):
        gate = jnp.dot(km_ref[h], qt_ref[h].astype(F32), precision=HIGHEST,
                       preferred_element_type=F32)
        gm = jnp.where(past, gate, -jnp.inf)
        cnt = jnp.zeros((n_blocks, BLK), jnp.int32)
        for m in range(n_blocks):
            gmm = gm[m:m + 1, :]
            ahead = (gmm > gm) | ((gmm == gm) & (blk_id > m))
            cnt = cnt + ahead.astype(jnp.int32)
        bias_ref[h] = jnp.where(past & (cnt < MOBA_TOP_K), 0.0, -jnp.inf).astype(F32)

    probs = []
    for h, s in enumerate(scores(own)):
        s = jnp.where(key_i <= qry_i, s, -jnp.inf)
        m0 = jnp.max(s, axis=0, keepdims=True)
        p = jnp.exp2((s - m0) * exp2_scale)
        m_ref[h] = m0
        l_ref[h] = jnp.sum(p, axis=0, keepdims=True)
        probs.append(p.astype(BF16))
    for h in range(heads):
        acc_ref[h] = jnp.dot(vt_ref[h, :, own], probs[h], preferred_element_type=F32)

    def past_blocks(js):
        rows = [slice(j * BLK, (j + 1) * BLK) for j in js]
        s_all = [scores(r) for r in rows]
        for j, r, s_heads in zip(js, rows, s_all):
            probs, alphas = [], []
            for h, sj in enumerate(s_heads):
                sj = sj + bias_ref[h, j:j + 1, :]
                m_old = m_ref[h]
                m_new = jnp.maximum(m_old, jnp.max(sj, axis=0, keepdims=True))
                alpha = jnp.exp2((m_old - m_new) * exp2_scale)
                pj = jnp.exp2((sj - m_new) * exp2_scale)
                m_ref[h] = m_new
                l_ref[h] = alpha * l_ref[h] + jnp.sum(pj, axis=0, keepdims=True)
                probs.append(pj.astype(BF16))
                alphas.append(alpha)
            for h in range(heads):
                acc_ref[h] = alphas[h] * acc_ref[h] + jnp.dot(vt_ref[h, :, r], probs[h],
                                                              preferred_element_type=F32)

    for j in range(0, n_blocks - 1, 2):
        if j + 1 < n_blocks - 1:
            @pl.when(j + 1 < qi)
            def _(j=j):
                past_blocks((j, j + 1))

        @pl.when(qi == j + 1)
        def _(j=j):
            past_blocks((j,))

    for h in range(heads):
        o_ref[0, :, head(h)] = (acc_ref[h] / l_ref[h]).T.astype(o_ref.dtype)


def moba_attention(q, kv, *, heads=4):
    b, t, d = q.shape
    nh = d // HEAD_DIM
    heads = min(heads, nh)
    ng = nh // heads
    nb = t // MOBA_BLOCK
    w = heads * HEAD_DIM
    return pl.pallas_call(
        functools.partial(_moba_kernel, n_blocks=nb, heads=heads),
        grid=(b, ng, nb),
        in_specs=[pl.BlockSpec((1, MOBA_BLOCK, w), lambda bi, g, qi: (bi, qi, g)),
                  pl.BlockSpec((1, t, w), lambda bi, g, qi: (bi, 0, g)),
                  pl.BlockSpec((1, t, w), lambda bi, g, qi: (bi, 0, ng + g))],
        out_specs=pl.BlockSpec((1, MOBA_BLOCK, w), lambda bi, g, qi: (bi, qi, g)),
        out_shape=jax.ShapeDtypeStruct((b, t, d), BF16),
        scratch_shapes=[pltpu.VMEM((heads, nb, HEAD_DIM), F32),
                        pltpu.VMEM((heads, HEAD_DIM, t), BF16),
                        pltpu.VMEM((heads, HEAD_DIM, MOBA_BLOCK), BF16),
                        pltpu.VMEM((heads, nb, MOBA_BLOCK), F32),
                        pltpu.VMEM((heads, 1, MOBA_BLOCK), F32),
                        pltpu.VMEM((heads, 1, MOBA_BLOCK), F32),
                        pltpu.VMEM((heads, HEAD_DIM, MOBA_BLOCK), F32)],
        compiler_params=_params("arbitrary", "arbitrary", "arbitrary"),
        name="moba_attention",
    )(q, kv, kv)


def _router_kernel(x_ref, g_ref, wr_ref, br_ref, h_ref, ids_ref, wts_ref, *, tm):
    _rms_rows_to(x_ref, g_ref, h_ref, tm, 64)
    logits = jnp.dot(h_ref[...], wr_ref[...], precision=HIGHEST,
                     preferred_element_type=F32) + br_ref[...]
    lane = lax.broadcasted_iota(jnp.int32, logits.shape, 1)
    neg = -jnp.inf
    big = ROUTER_LANES

    gl = jnp.where(lane < N_GROUPS, logits, neg)
    gmax = jnp.max(gl, axis=-1, keepdims=True)
    gsum = jnp.sum(jnp.exp(gl - gmax), axis=-1, keepdims=True)
    g_top_p = 1.0 / gsum
    g_top = jnp.min(jnp.where(gl == gmax, lane, big), axis=-1, keepdims=True)

    lo = N_GROUPS + EXPERTS_PER_GROUP * g_top
    el = jnp.where((lane >= lo) & (lane < lo + EXPERTS_PER_GROUP), logits, neg)
    emax = jnp.max(el, axis=-1, keepdims=True)
    esum = jnp.sum(jnp.exp(el - emax), axis=-1, keepdims=True)
    i1 = jnp.min(jnp.where(el == emax, lane, big), axis=-1, keepdims=True)
    el2 = jnp.where(lane == i1, neg, el)
    emax2 = jnp.max(el2, axis=-1, keepdims=True)
    i2 = jnp.min(jnp.where(el2 == emax2, lane, big), axis=-1, keepdims=True)
    p1 = 1.0 / esum
    p2 = jnp.exp(emax2 - emax) / esum
    psum = p1 + p2
    w1 = g_top_p * p1 / psum
    w2 = g_top_p * p2 / psum

    ids_ref[...] = jnp.where(lane == 0, i1 - N_GROUPS, jnp.where(lane == 1, i2 - N_GROUPS, 0))
    wts_ref[...] = jnp.where(lane == 0, w1, jnp.where(lane == 1, w2, 0.0))


def moe_router(x, gain, w_router, b_router, *, tm=512):
    n, d = x.shape
    tm = min(tm, n)
    return pl.pallas_call(
        functools.partial(_router_kernel, tm=tm),
        grid=(n // tm,),
        in_specs=[pl.BlockSpec((tm, d), lambda i: (i, 0)),
                  pl.BlockSpec((1, d), lambda i: (0, 0)),
                  pl.BlockSpec((d, ROUTER_LANES), lambda i: (0, 0)),
                  pl.BlockSpec((1, ROUTER_LANES), lambda i: (0, 0))],
        out_specs=[pl.BlockSpec((tm, d), lambda i: (i, 0)),
                   pl.BlockSpec((tm, ROUTER_LANES), lambda i: (i, 0)),
                   pl.BlockSpec((tm, ROUTER_LANES), lambda i: (i, 0))],
        out_shape=[jax.ShapeDtypeStruct((n, d), F32),
                   jax.ShapeDtypeStruct((n, ROUTER_LANES), jnp.int32),
                   jax.ShapeDtypeStruct((n, ROUTER_LANES), F32)],
        compiler_params=_params("arbitrary"),
        name="moe_router",
    )(x, gain.reshape(1, d), w_router, b_router)


def _expert_kernel(blk_e_ref, n_used_ref, tok0_ref, tok_ref, dst_ref, h_hbm, wg_ref, wu_ref, wd_ref,
                   out_hbm, xbuf0, xbuf1, ybuf0, ybuf1, wgb, wub, wdb, sem_in, sem_out, *, rows):
    i = pl.program_id(0)
    n_used = n_used_ref[0]
    xbufs = (xbuf0, xbuf1)
    ybufs = (ybuf0, ybuf1)

    def gather(idx_ref, slot):
        for r in range(rows):
            pltpu.make_async_copy(h_hbm.at[pl.ds(idx_ref[0, 0, r], 1), :],
                                  xbufs[slot].at[pl.ds(r, 1), :], sem_in.at[slot]).start()

    def scatter(slot):
        for r in range(rows):
            pltpu.make_async_copy(ybufs[slot].at[pl.ds(r, 1), :],
                                  out_hbm.at[pl.ds(dst_ref[0, 0, r], 1), :], sem_out.at[slot]).start()

    def wait_gather(slot):
        pltpu.make_async_copy(h_hbm.at[pl.ds(0, rows), :], xbufs[slot], sem_in.at[slot]).wait()

    def wait_scatter(slot):
        pltpu.make_async_copy(ybufs[slot], out_hbm.at[pl.ds(0, rows), :], sem_out.at[slot]).wait()

    @pl.when(i == 0)
    def _():
        ybuf0[...] = jnp.zeros_like(ybuf0)
        ybuf1[...] = jnp.zeros_like(ybuf1)
        spare0 = pltpu.make_async_copy(ybuf0, out_hbm.at[pl.ds(out_hbm.shape[0] - 2 * rows, rows), :],
                                       sem_out.at[0])
        spare0.start()
        spare0.wait()
        gather(tok0_ref, 0)

    @pl.when((i == 0) | (blk_e_ref[i] != blk_e_ref[jnp.maximum(i - 1, 0)]))
    def _():
        wgb[...] = wg_ref[...].astype(BF16)
        wub[...] = wu_ref[...].astype(BF16)
        wdb[...] = wd_ref[...].astype(BF16)

    def step(slot):
        other = 1 - slot
        wait_gather(slot)
        gather(tok_ref, other)
        scatter(other)
        x = xbufs[slot][...].astype(BF16)
        g = jnp.dot(x, wgb[...], preferred_element_type=F32)
        u = jnp.dot(x, wub[...], preferred_element_type=F32)
        mid = (jax.nn.silu(g) * u).astype(BF16)
        ybufs[slot][...] = jnp.dot(mid, wdb[...], preferred_element_type=F32)
        wait_scatter(other)

    for slot in range(2):
        @pl.when((i <= n_used) & (i % 2 == slot))
        def _(slot=slot):
            step(slot)

    @pl.when(i == n_used)
    def _():
        for slot in range(2):
            @pl.when(i % 2 == slot)
            def _(slot=slot):
                wait_gather(1 - slot)


def expert_mlp(h, blk_e, n_used, tok, dst, w_gate, w_up, w_down, layer):
    n, d = h.shape
    n_steps, _, r = dst.shape
    ff = w_gate.shape[3]

    def smem_rows(index_map):
        return pl.BlockSpec((1, 1, r), index_map, memory_space=pltpu.SMEM)

    def weight(shape):
        return pl.BlockSpec((None, None) + shape, lambda i, be, nu: (layer, be[i], 0, 0))

    grid_spec = pltpu.PrefetchScalarGridSpec(
        num_scalar_prefetch=2,
        grid=(n_steps,),
        in_specs=[
            smem_rows(lambda i, be, nu: (0, 0, 0)),
            smem_rows(lambda i, be, nu: (i + 1, 0, 0)),
            smem_rows(lambda i, be, nu: (i, 0, 0)),
            pl.BlockSpec(memory_space=pl.ANY),
            weight((d, ff)), weight((d, ff)), weight((ff, d)),
        ],
        out_specs=pl.BlockSpec(memory_space=pl.ANY),
        scratch_shapes=[
            pltpu.VMEM((r, d), F32), pltpu.VMEM((r, d), F32),
            pltpu.VMEM((r, d), F32), pltpu.VMEM((r, d), F32),
            pltpu.VMEM((d, ff), BF16),
            pltpu.VMEM((d, ff), BF16),
            pltpu.VMEM((ff, d), BF16),
            pltpu.SemaphoreType.DMA((2,)),
            pltpu.SemaphoreType.DMA((2,)),
        ],
    )
    return pl.pallas_call(
        functools.partial(_expert_kernel, rows=r),
        grid_spec=grid_spec,
        out_shape=jax.ShapeDtypeStruct((TOP_K_IN_GROUP * n + 2 * r, d), F32),
        compiler_params=_params("arbitrary"),
        name="expert_mlp",
    )(blk_e, n_used, tok, tok, dst, h, w_gate, w_up, w_down)


def _combine_kernel(x_ref, y0_ref, y1_ref, w_ref, o_ref):
    w = w_ref[...]
    o_ref[...] = x_ref[...] + (w[:, 0:1] * y0_ref[...] + w[:, 1:2] * y1_ref[...])


def moe_combine(x, y, wts, *, tm=512):
    n, d = x.shape
    tm = min(tm, n)
    nt = n // tm
    return pl.pallas_call(
        _combine_kernel,
        grid=(nt,),
        in_specs=[pl.BlockSpec((tm, d), lambda i: (i, 0)),
                  pl.BlockSpec((tm, d), lambda i: (i, 0)),
                  pl.BlockSpec((tm, d), lambda i: (nt + i, 0)),
                  pl.BlockSpec((tm, ROUTER_LANES), lambda i: (i, 0))],
        out_specs=pl.BlockSpec((tm, d), lambda i: (i, 0)),
        out_shape=jax.ShapeDtypeStruct((n, d), F32),
        compiler_params=_params("arbitrary"),
        name="moe_combine",
    )(x, y, y, wts)


def _expert_row_index(expert_id, rows_per_block):
    n, k = expert_id.shape
    a_total = n * k
    r = rows_per_block
    flat_e = expert_id.reshape(-1)
    order = jnp.argsort(flat_e).astype(jnp.int32)
    experts = jnp.arange(N_EXPERTS, dtype=jnp.int32)
    counts = jnp.sum((flat_e[:, None] == experts[None, :]).astype(jnp.int32), axis=0)
    padded = (counts + r - 1) // r * r
    pad_end = jnp.cumsum(padded)
    pad_start = pad_end - padded
    start = jnp.cumsum(counts) - counts
    n_blk = (a_total + N_EXPERTS * (r - 1) + r - 1) // r
    n_used = (pad_end[-1] // r).astype(jnp.int32)
    blk = jnp.arange(n_blk, dtype=jnp.int32)
    blk_row0 = blk * r
    blk_e = jnp.minimum(jnp.sum((pad_end[None, :] <= blk_row0[:, None]).astype(jnp.int32), axis=1),
                        N_EXPERTS - 1)
    in_e = blk_row0 - pad_start[blk_e]
    nvalid = jnp.clip(counts[blk_e] - in_e, 0, r)
    lane = jnp.arange(r, dtype=jnp.int32)[None, :]
    valid = (lane < nvalid[:, None]) & (blk[:, None] < n_used)
    idx = (start[blk_e] + in_e)[:, None] + lane
    src = order[jnp.clip(idx, 0, a_total - 1)]
    tok = jnp.where(valid, src // k, 0)
    spare = a_total + (blk[:, None] % 2) * r + lane
    dst = jnp.where(valid, (src % k) * n + src // k, spare)
    tok_steps = jnp.concatenate([tok, jnp.zeros((2, r), jnp.int32)], axis=0)
    first_spare = (a_total + r + lane).astype(jnp.int32)
    dst_steps = jnp.concatenate([first_spare, dst], axis=0)
    blk_e_steps = jnp.concatenate([blk_e, blk_e[-1:]], axis=0)
    n_steps = n_blk + 1
    return (blk_e_steps.astype(jnp.int32), n_used.reshape(1),
            tok_steps.reshape(n_steps + 1, 1, r), dst_steps.reshape(n_steps, 1, r))


def hier_moe(x, gain, w_group, b_group, w_expert, b_expert, w_gate, w_up, w_down, layer):
    n, d = x.shape
    pad = ROUTER_LANES - N_GROUPS - N_EXPERTS
    w_router = jnp.concatenate([w_group, w_expert, jnp.zeros((d, pad), F32)], axis=1)
    b_router = jnp.concatenate([b_group, b_expert, jnp.zeros((pad,), F32)]).reshape(1, ROUTER_LANES)
    h, ids, wts = moe_router(x, gain, w_router, b_router)
    blk_e, n_used, tok, dst = _expert_row_index(ids[:, :TOP_K_IN_GROUP], EXPERT_ROWS)
    y = expert_mlp(h, blk_e, n_used, tok, dst, w_gate, w_up, w_down, layer)
    return moe_combine(x, y, wts)


def _rope_tables(t):
    inv_freq = ROPE_THETA ** (-jnp.arange(0, HEAD_DIM, 2, dtype=F32) / HEAD_DIM)
    ang = jnp.arange(t, dtype=F32)[:, None] * inv_freq[None, :]
    cos, sin = jnp.cos(ang), jnp.sin(ang)
    return jnp.concatenate([cos, cos], axis=1), jnp.concatenate([-sin, sin], axis=1)


def kernel(x, norm_mix, norm_ffn, a_w_in, a_lb_logits, a_o_gain, a_w_out, kv_norm, kv_w, k_norm,
           b_w_q, b_q_norm, b_w_out, moe_w_group, moe_b_group, moe_w_expert, moe_b_expert,
           moe_w_gate, moe_w_up, moe_w_down):
    b, t, d = x.shape
    n = b * t
    depth = norm_mix.shape[0]
    n_a = a_w_in.shape[0]
    lower_bounds = jnp.cumsum(jax.nn.softmax(a_lb_logits.astype(F32), axis=0), axis=0)
    cos, sin = _rope_tables(t)
    xf = x.reshape(n, d)
    kv = None
    for l in range(depth):
        if l < n_a:
            z = rms_matmul(xf, norm_mix[l], a_w_in, l, F32)
            o = hgrn2_scan(z.reshape(b, t, 4 * d), lower_bounds[l], a_o_gain[l])
            xf = matmul_residual(o.reshape(n, d), a_w_out, l, xf)
        else:
            if l == n_a:
                kv = rms_matmul(xf, kv_norm, kv_w[None], 0, BF16,
                                rope=(k_norm, cos, sin, d), seq_len=t)
            lb_ = l - n_a
            q = rms_matmul(xf, norm_mix[l], b_w_q, lb_, BF16,
                           rope=(b_q_norm[lb_], cos, sin, d), seq_len=t)
            o = moba_attention(q.reshape(b, t, d), kv.reshape(b, t, 2 * d))
            xf = matmul_residual(o.reshape(n, d), b_w_out, lb_, xf)
        xf = hier_moe(xf, norm_ffn[l], moe_w_group[l], moe_b_group[l], moe_w_expert[l],
                      moe_b_expert[l], moe_w_gate, moe_w_up, moe_w_down, l)
    return xf.reshape(b, t, d)
```

```python
import functools

import jax
import jax.numpy as jnp
from jax import lax
from jax.experimental import pallas as pl
from jax.experimental.pallas import tpu as pltpu

HEAD_DIM = 128
HGRN_CHUNK = 64
HGRN_SUB = 8
HGRN_UNROLL = 8
MOBA_BLOCK = 256
MOBA_TOP_K = 3
ROPE_THETA = 10000.0
N_GROUPS = 4
EXPERTS_PER_GROUP = 8
N_EXPERTS = N_GROUPS * EXPERTS_PER_GROUP
TOP_K_IN_GROUP = 2
RMS_EPS = 1e-6
ROUTER_LANES = 128
EXPERT_ROWS = 256
DMA_THREADS = 2
ROPE_ROW_PARTS = 4

V7X_VMEM_LIMIT = 56 * 1024 * 1024

F32 = jnp.float32
BF16 = jnp.bfloat16
HIGHEST = lax.Precision.HIGHEST
NT_DIMS = (((1,), (1,)), ((), ()))


def _params(*sem):
    return pltpu.CompilerParams(dimension_semantics=sem, vmem_limit_bytes=V7X_VMEM_LIMIT)


def _rms_rows_to(x_ref, g_ref, out_ref, rows, chunk):
    gain = g_ref[...]

    def body(c, carry):
        sl = pl.ds(pl.multiple_of(c * chunk, chunk), chunk)
        x = x_ref[sl, :]
        ms = jnp.mean(x * x, axis=-1, keepdims=True)
        out_ref[sl, :] = (x * lax.rsqrt(ms + RMS_EPS) * gain).astype(out_ref.dtype)
        return carry

    lax.fori_loop(0, rows // chunk, body, 0)


def _head_norm_rope(y, hg, cos, sin):
    ms = jnp.mean(y * y, axis=-1, keepdims=True)
    yn = y * lax.rsqrt(ms + RMS_EPS) * hg
    return yn * cos + pltpu.roll(yn, HEAD_DIM // 2, axis=1) * sin


def _rms_matmul_kernel(*refs, tm, tn, n_rope_tiles):
    if n_rope_tiles:
        x_ref, g_ref, w_ref, hg_ref, cos_ref, sin_ref, o_ref, hn_ref = refs
    else:
        x_ref, g_ref, w_ref, o_ref, hn_ref = refs
    j = pl.program_id(1)

    @pl.when(j == 0)
    def _():
        _rms_rows_to(x_ref, g_ref, hn_ref, tm, 64)

    def plain():
        o_ref[...] = jnp.dot(hn_ref[...], w_ref[...].astype(BF16),
                             preferred_element_type=F32).astype(o_ref.dtype)

    if not n_rope_tiles:
        plain()
        return

    @pl.when(j < n_rope_tiles)
    def _():
        w = w_ref[...].astype(BF16)
        hg = hg_ref[...]
        part = tm // ROPE_ROW_PARTS
        for p in range(ROPE_ROW_PARTS):
            rows = slice(p * part, (p + 1) * part)
            y = jnp.dot(hn_ref[rows, :], w, preferred_element_type=F32)
            for h in range(tn // HEAD_DIM):
                sl = slice(h * HEAD_DIM, (h + 1) * HEAD_DIM)
                o_ref[rows, sl] = _head_norm_rope(y[:, sl], hg, cos_ref[rows, :],
                                                  sin_ref[rows, :]).astype(o_ref.dtype)

    pl.when(j >= n_rope_tiles)(plain)


def rms_matmul(x, gain, w, layer, out_dtype, *, rope=None, seq_len=None, tm=1024, tn=512):
    n, d = x.shape
    e = w.shape[2]
    tm = min(tm, n)
    tn = min(tn, e)
    in_specs = [
        pl.BlockSpec((tm, d), lambda i, j: (i, 0)),
        pl.BlockSpec((1, d), lambda i, j: (0, 0)),
        pl.BlockSpec((None, d, tn), lambda i, j: (layer, 0, j)),
    ]
    args = [x, gain.reshape(1, d), w]
    n_rope_tiles = 0
    if rope is not None:
        hg, cos, sin, n_rope_cols = rope
        tm = min(tm, seq_len)
        tn = min(tn, n_rope_cols)
        in_specs[0] = pl.BlockSpec((tm, d), lambda i, j: (i, 0))
        in_specs[2] = pl.BlockSpec((None, d, tn), lambda i, j: (layer, 0, j))
        n_rope_tiles = n_rope_cols // tn
        pos_tiles = seq_len // tm
        in_specs += [
            pl.BlockSpec((1, HEAD_DIM), lambda i, j: (0, 0)),
            pl.BlockSpec((tm, HEAD_DIM), lambda i, j: (i % pos_tiles, 0)),
            pl.BlockSpec((tm, HEAD_DIM), lambda i, j: (i % pos_tiles, 0)),
        ]
        args += [hg.reshape(1, HEAD_DIM), cos, sin]
    return pl.pallas_call(
        functools.partial(_rms_matmul_kernel, tm=tm, tn=tn, n_rope_tiles=n_rope_tiles),
        grid=(n // tm, e // tn),
        in_specs=in_specs,
        out_specs=pl.BlockSpec((tm, tn), lambda i, j: (i, j)),
        out_shape=jax.ShapeDtypeStruct((n, e), out_dtype),
        scratch_shapes=[pltpu.VMEM((tm, d), BF16)],
        compiler_params=_params("arbitrary", "arbitrary"),
        name="rms_matmul",
    )(*args)


def _matmul_res_kernel(a_ref, w_ref, r_ref, o_ref):
    o_ref[...] = r_ref[...] + jnp.dot(a_ref[...], w_ref[...].astype(BF16),
                                      preferred_element_type=F32)


def matmul_residual(a, w, layer, res, *, tm=1024, tn=512):
    n, d = a.shape
    e = w.shape[2]
    tm = min(tm, n)
    tn = min(tn, e)
    return pl.pallas_call(
        _matmul_res_kernel,
        grid=(n // tm, e // tn),
        in_specs=[
            pl.BlockSpec((tm, d), lambda i, j: (i, 0)),
            pl.BlockSpec((None, d, tn), lambda i, j: (layer, 0, j)),
            pl.BlockSpec((tm, tn), lambda i, j: (i, j)),
        ],
        out_specs=pl.BlockSpec((tm, tn), lambda i, j: (i, j)),
        out_shape=jax.ShapeDtypeStruct((n, e), F32),
        compiler_params=_params("arbitrary", "arbitrary"),
        name="matmul_residual",
    )(a, w, res)


def _hgrn_kernel(zq_ref, zf_ref, zi_ref, zg_ref, lb_ref, og_ref, o_ref,
                 st_ref, g_scr, h_scr, a_scr, u_scr, *, n_groups):
    C = HGRN_CHUNK
    SB = HGRN_SUB
    NB = C // SB

    @pl.when(pl.program_id(2) == 0)
    def _():
        st_ref[...] = jnp.zeros_like(st_ref)

    lb = lb_ref[...]
    og = og_ref[...]
    row_c = lax.broadcasted_iota(jnp.int32, (C, C), 0)
    col_c = lax.broadcasted_iota(jnp.int32, (C, C), 1)
    causal = row_c >= col_c
    tril = causal.astype(F32)
    lane_s = lax.broadcasted_iota(jnp.int32, (SB, C), 1)
    zero_blk = jnp.zeros((SB, HEAD_DIM), F32)
    halves = [h for h in (32, 16, 8) if SB <= h < C]
    level_mask = {half: ((row_c // (2 * half) == col_c // (2 * half))
                         & ((row_c // half) % 2 == 1) & ((col_c // half) % 2 == 0))
                  for half in halves}

    def blk(x, b):
        return x[b * SB:(b + 1) * SB, :]

    def gates(rows, u):
        f = lb + (1.0 - lb) * jax.nn.sigmoid(zf_ref[0, rows, :])
        G2 = jnp.dot(tril, jnp.log2(f), precision=HIGHEST, preferred_element_type=F32)
        g_scr[u] = G2
        h_scr[u] = G2 - jnp.log2(1.0 - f)

    def scores(rows, u):
        q = zq_ref[0, rows, :]
        G2 = g_scr[u]
        H2 = h_scr[u]
        A = jnp.zeros((C, C), F32)
        for half in halves:
            per_half = half // SB
            qa, kb = [], []
            for b in range(NB):
                grp = b // per_half
                if grp % 2 == 1:
                    r = grp * half - 1
                    qa.append(blk(q, b) * jnp.exp2(blk(G2, b) - g_scr[u, r:r + 1, :]))
                    kb.append(zero_blk)
                else:
                    r = (grp + 1) * half - 1
                    kb.append(jnp.exp2(g_scr[u, r:r + 1, :] - blk(H2, b)))
                    qa.append(zero_blk)
            sq = lax.dot_general(jnp.concatenate(qa, axis=0).astype(BF16),
                                 jnp.concatenate(kb, axis=0).astype(BF16), NT_DIMS,
                                 preferred_element_type=F32)
            A = A + jnp.where(level_mask[half], sq, 0.0)
        a_parts = []
        for b in range(NB):
            Gt = blk(G2, b)
            qt = blk(q, b)
            a_b = jnp.zeros((SB, C), F32)
            for s in range(b * SB, (b + 1) * SB):
                e = jnp.exp2(Gt - h_scr[u, s:s + 1, :])
                a_b = jnp.where(lane_s == s, jnp.sum(qt * e, axis=-1, keepdims=True), a_b)
            a_parts.append(a_b)
        A = jnp.where(causal, A + jnp.concatenate(a_parts, axis=0), 0.0)
        a_scr[u] = A.astype(BF16)

    def state_input(rows, u):
        G2_end = g_scr[u, C - 1:C, :]
        kd = jnp.exp2(G2_end - h_scr[u])
        u_scr[u] = jnp.dot(zi_ref[0, rows, :].T.astype(BF16), kd.astype(BF16),
                           preferred_element_type=F32)

    def output(rows, u):
        G2 = g_scr[u]
        st = st_ref[...]
        qg = (zq_ref[0, rows, :] * jnp.exp2(G2)).astype(BF16)
        o = (lax.dot_general(qg, st.astype(BF16), NT_DIMS, preferred_element_type=F32)
             + jnp.dot(a_scr[u], zi_ref[0, rows, :].astype(BF16), preferred_element_type=F32))
        st_ref[...] = jnp.exp2(G2[C - 1:C, :]) * st + u_scr[u]
        ms = jnp.mean(o * o, axis=-1, keepdims=True)
        zg = zg_ref[0, rows, :]
        o_ref[0, rows, :] = (o * lax.rsqrt(ms + RMS_EPS) * og * jax.nn.silu(zg)).astype(o_ref.dtype)

    def group(gi, carry):
        base = pl.multiple_of(gi * (HGRN_UNROLL * C), HGRN_UNROLL * C)
        for phase in (gates, scores, state_input, output):
            for u in range(HGRN_UNROLL):
                phase(pl.ds(base + u * C, C), u)
        return carry

    lax.fori_loop(0, n_groups, group, 0)


def hgrn2_scan(z, lb, o_gain, *, tt=4096):
    b, t, d4 = z.shape
    d = d4 // 4
    nh = d // HEAD_DIM
    tt = min(tt, t)

    def zspec(part):
        return pl.BlockSpec((1, tt, HEAD_DIM), lambda bi, h, ti: (bi, ti, part * nh + h))

    return pl.pallas_call(
        functools.partial(_hgrn_kernel, n_groups=tt // (HGRN_CHUNK * HGRN_UNROLL)),
        grid=(b, nh, t // tt),
        in_specs=[zspec(0), zspec(1), zspec(2), zspec(3),
                  pl.BlockSpec((1, HEAD_DIM), lambda bi, h, ti: (0, h)),
                  pl.BlockSpec((1, HEAD_DIM), lambda bi, h, ti: (0, 0))],
        out_specs=pl.BlockSpec((1, tt, HEAD_DIM), lambda bi, h, ti: (bi, ti, h)),
        out_shape=jax.ShapeDtypeStruct((b, t, d), BF16),
        scratch_shapes=[pltpu.VMEM((HEAD_DIM, HEAD_DIM), F32),
                        pltpu.VMEM((HGRN_UNROLL, HGRN_CHUNK, HEAD_DIM), F32),
                        pltpu.VMEM((HGRN_UNROLL, HGRN_CHUNK, HEAD_DIM), F32),
                        pltpu.VMEM((HGRN_UNROLL, HGRN_CHUNK, HGRN_CHUNK), BF16),
                        pltpu.VMEM((HGRN_UNROLL, HEAD_DIM, HEAD_DIM), F32)],
        compiler_params=_params("arbitrary", "arbitrary", "arbitrary"),
        name="hgrn2_scan",
    )(z, z, z, z, lb.reshape(1, d), o_gain.reshape(1, HEAD_DIM))


def _moba_kernel(q_ref, k_ref, v_ref, o_ref, km_ref, vt_ref, qt_ref, bias_ref, m_ref, l_ref, acc_ref,
                 *, n_blocks, heads):
    BLK = MOBA_BLOCK
    qi = pl.program_id(2)
    exp2_scale = HEAD_DIM ** -0.5 * 1.4426950408889634

    def head(h):
        return slice(h * HEAD_DIM, (h + 1) * HEAD_DIM)

    eye = (lax.broadcasted_iota(jnp.int32, (HEAD_DIM, HEAD_DIM), 0)
           == lax.broadcasted_iota(jnp.int32, (HEAD_DIM, HEAD_DIM), 1)).astype(BF16)

    @pl.when(qi == 0)
    def _():
        for h in range(heads):
            for j in range(n_blocks):
                rows = slice(j * BLK, (j + 1) * BLK)
                kb = k_ref[0, rows, head(h)].astype(F32)
                km_ref[h, j:j + 1, :] = jnp.mean(kb, axis=0, keepdims=True)
                vt_ref[h, :, rows] = lax.dot_general(eye, v_ref[0, rows, head(h)], NT_DIMS,
                                                     preferred_element_type=F32).astype(BF16)

    blk_id = lax.broadcasted_iota(jnp.int32, (n_blocks, BLK), 0)
    past = blk_id < qi
    key_i = lax.broadcasted_iota(jnp.int32, (BLK, BLK), 0)
    qry_i = lax.broadcasted_iota(jnp.int32, (BLK, BLK), 1)
    own = pl.ds(pl.multiple_of(qi * BLK, BLK), BLK)

    def scores(rows):
        return [jnp.dot(k_ref[0, rows, head(h)], qt_ref[h], preferred_element_type=F32)
                for h in range(heads)]

    for h in range(heads):
        qt_ref[h] = lax.dot_general(eye, q_ref[0, :, head(h)], NT_DIMS,
                                    preferred_element_type=F32).astype(BF16)
    for h in range(heads):
        gate = jnp.dot(km_ref[h], qt_ref[h].astype(F32), precision=HIGHEST,
                       preferred_element_type=F32)
        gm = jnp.where(past, gate, -jnp.inf)
        cnt = jnp.zeros((n_blocks, BLK), jnp.int32)
        for m in range(n_blocks):
            gmm = gm[m:m + 1, :]
            ahead = (gmm > gm) | ((gmm == gm) & (blk_id > m))
            cnt = cnt + ahead.astype(jnp.int32)
        bias_ref[h] = jnp.where(past & (cnt < MOBA_TOP_K), 0.0, -jnp.inf).astype(F32)

    probs = []
    for h, s in enumerate(scores(own)):
        s = jnp.where(key_i <= qry_i, s, -jnp.inf)
        m0 = jnp.max(s, axis=0, keepdims=True)
        p = jnp.exp2((s - m0) * exp2_scale)
        m_ref[h] = m0
        l_ref[h] = jnp.sum(p, axis=0, keepdims=True)
        probs.append(p.astype(BF16))
    for h in range(heads):
        acc_ref[h] = jnp.dot(vt_ref[h, :, own], probs[h], preferred_element_type=F32)

    def past_blocks(js):
        rows = [slice(j * BLK, (j + 1) * BLK) for j in js]
        s_all = [scores(r) for r in rows]
        for j, r, s_heads in zip(js, rows, s_all):
            probs, alphas = [], []
            for h, sj in enumerate(s_heads):
                sj = sj + bias_ref[h, j:j + 1, :]
                m_old = m_ref[h]
                m_new = jnp.maximum(m_old, jnp.max(sj, axis=0, keepdims=True))
                alpha = jnp.exp2((m_old - m_new) * exp2_scale)
                pj = jnp.exp2((sj - m_new) * exp2_scale)
                m_ref[h] = m_new
                l_ref[h] = alpha * l_ref[h] + jnp.sum(pj, axis=0, keepdims=True)
                probs.append(pj.astype(BF16))
                alphas.append(alpha)
            for h in range(heads):
                acc_ref[h] = alphas[h] * acc_ref[h] + jnp.dot(vt_ref[h, :, r], probs[h],
                                                              preferred_element_type=F32)

    for j in range(0, n_blocks - 1, 2):
        if j + 1 < n_blocks - 1:
            @pl.when(j + 1 < qi)
            def _(j=j):
                past_blocks((j, j + 1))

        @pl.when(qi == j + 1)
        def _(j=j):
            past_blocks((j,))

    for h in range(heads):
        o_ref[0, :, head(h)] = (acc_ref[h] / l_ref[h]).T.astype(o_ref.dtype)


def moba_attention(q, kv, *, heads=4):
    b, t, d = q.shape
    nh = d // HEAD_DIM
    heads = min(heads, nh)
    ng = nh // heads
    nb = t // MOBA_BLOCK
    w = heads * HEAD_DIM
    return pl.pallas_call(
        functools.partial(_moba_kernel, n_blocks=nb, heads=heads),
        grid=(b, ng, nb),
        in_specs=[pl.BlockSpec((1, MOBA_BLOCK, w), lambda bi, g, qi: (bi, qi, g)),
                  pl.BlockSpec((1, t, w), lambda bi, g, qi: (bi, 0, g)),
                  pl.BlockSpec((1, t, w), lambda bi, g, qi: (bi, 0, ng + g))],
        out_specs=pl.BlockSpec((1, MOBA_BLOCK, w), lambda bi, g, qi: (bi, qi, g)),
        out_shape=jax.ShapeDtypeStruct((b, t, d), BF16),
        scratch_shapes=[pltpu.VMEM((heads, nb, HEAD_DIM), F32),
                        pltpu.VMEM((heads, HEAD_DIM, t), BF16),
                        pltpu.VMEM((heads, HEAD_DIM, MOBA_BLOCK), BF16),
                        pltpu.VMEM((heads, nb, MOBA_BLOCK), F32),
                        pltpu.VMEM((heads, 1, MOBA_BLOCK), F32),
                        pltpu.VMEM((heads, 1, MOBA_BLOCK), F32),
                        pltpu.VMEM((heads, HEAD_DIM, MOBA_BLOCK), F32)],
        compiler_params=_params("arbitrary", "arbitrary", "arbitrary"),
        name="moba_attention",
    )(q, kv, kv)


def _router_kernel(x_ref, g_ref, wr_ref, br_ref, h_ref, ids_ref, wts_ref, *, tm):
    _rms_rows_to(x_ref, g_ref, h_ref, tm, 64)
    h = h_ref[...]
    hi = h.astype(BF16)
    lo = (h - hi.astype(F32)).astype(BF16)
    w = wr_ref[...]
    hw = jnp.dot(hi, w, preferred_element_type=F32)
    logits = (hw[:, :ROUTER_LANES] + hw[:, ROUTER_LANES:]
              + jnp.dot(lo, w[:, :ROUTER_LANES], preferred_element_type=F32)
              + br_ref[...])
    lane = lax.broadcasted_iota(jnp.int32, logits.shape, 1)
    neg = -jnp.inf
    big = ROUTER_LANES

    gl = jnp.where(lane < N_GROUPS, logits, neg)
    gmax = jnp.max(gl, axis=-1, keepdims=True)
    gsum = jnp.sum(jnp.exp(gl - gmax), axis=-1, keepdims=True)
    g_top_p = 1.0 / gsum
    g_top = jnp.min(jnp.where(gl == gmax, lane, big), axis=-1, keepdims=True)

    lo = N_GROUPS + EXPERTS_PER_GROUP * g_top
    el = jnp.where((lane >= lo) & (lane < lo + EXPERTS_PER_GROUP), logits, neg)
    emax = jnp.max(el, axis=-1, keepdims=True)
    esum = jnp.sum(jnp.exp(el - emax), axis=-1, keepdims=True)
    i1 = jnp.min(jnp.where(el == emax, lane, big), axis=-1, keepdims=True)
    el2 = jnp.where(lane == i1, neg, el)
    emax2 = jnp.max(el2, axis=-1, keepdims=True)
    i2 = jnp.min(jnp.where(el2 == emax2, lane, big), axis=-1, keepdims=True)
    p1 = 1.0 / esum
    p2 = jnp.exp(emax2 - emax) / esum
    psum = p1 + p2
    w1 = g_top_p * p1 / psum
    w2 = g_top_p * p2 / psum

    ids_ref[...] = jnp.where(lane == 0, i1 - N_GROUPS, jnp.where(lane == 1, i2 - N_GROUPS, 0))
    wts_ref[...] = jnp.where(lane == 0, w1, jnp.where(lane == 1, w2, 0.0))


def moe_router(x, gain, w_router, b_router, *, tm=512):
    n, d = x.shape
    tm = min(tm, n)
    return pl.pallas_call(
        functools.partial(_router_kernel, tm=tm),
        grid=(n // tm,),
        in_specs=[pl.BlockSpec((tm, d), lambda i: (i, 0)),
                  pl.BlockSpec((1, d), lambda i: (0, 0)),
                  pl.BlockSpec((d, 2 * ROUTER_LANES), lambda i: (0, 0)),
                  pl.BlockSpec((1, ROUTER_LANES), lambda i: (0, 0))],
        out_specs=[pl.BlockSpec((tm, d), lambda i: (i, 0)),
                   pl.BlockSpec((tm, ROUTER_LANES), lambda i: (i, 0)),
                   pl.BlockSpec((tm, ROUTER_LANES), lambda i: (i, 0))],
        out_shape=[jax.ShapeDtypeStruct((n, d), F32),
                   jax.ShapeDtypeStruct((n, ROUTER_LANES), jnp.int32),
                   jax.ShapeDtypeStruct((n, ROUTER_LANES), F32)],
        compiler_params=_params("arbitrary"),
        name="moe_router",
    )(x, gain.reshape(1, d), w_router, b_router)


def _expert_kernel(blk_e_ref, n_used_ref, tok0_ref, tok_ref, dst_ref, h_hbm, wg_ref, wu_ref, wd_ref,
                   out_hbm, xbuf0, xbuf1, ybuf0, ybuf1, wgb, wub, wdb, sem_in, sem_out, *, rows):
    i = pl.program_id(0)
    n_used = n_used_ref[0]
    xbufs = (xbuf0, xbuf1)
    ybufs = (ybuf0, ybuf1)

    def gather(idx_ref, slot):
        for r in range(rows):
            pltpu.make_async_copy(h_hbm.at[pl.ds(idx_ref[0, 0, r], 1), :],
                                  xbufs[slot].at[pl.ds(r, 1), :],
                                  sem_in.at[slot]).start(priority=r % DMA_THREADS)

    def scatter(slot):
        for r in range(rows):
            pltpu.make_async_copy(ybufs[slot].at[pl.ds(r, 1), :],
                                  out_hbm.at[pl.ds(dst_ref[0, 0, r], 1), :],
                                  sem_out.at[slot]).start(priority=r % DMA_THREADS)

    def wait_gather(slot):
        pltpu.make_async_copy(h_hbm.at[pl.ds(0, rows), :], xbufs[slot], sem_in.at[slot]).wait()

    def wait_scatter(slot):
        pltpu.make_async_copy(ybufs[slot], out_hbm.at[pl.ds(0, rows), :], sem_out.at[slot]).wait()

    @pl.when(i == 0)
    def _():
        ybuf0[...] = jnp.zeros_like(ybuf0)
        ybuf1[...] = jnp.zeros_like(ybuf1)
        spare0 = pltpu.make_async_copy(ybuf0, out_hbm.at[pl.ds(out_hbm.shape[0] - 2 * rows, rows), :],
                                       sem_out.at[0])
        spare0.start()
        spare0.wait()
        gather(tok0_ref, 0)

    @pl.when((i == 0) | (blk_e_ref[i] != blk_e_ref[jnp.maximum(i - 1, 0)]))
    def _():
        wgb[...] = wg_ref[...].astype(BF16)
        wub[...] = wu_ref[...].astype(BF16)
        wdb[...] = wd_ref[...].astype(BF16)

    def step(slot):
        other = 1 - slot
        wait_gather(slot)
        gather(tok_ref, other)
        scatter(other)
        x = xbufs[slot][...].astype(BF16)
        g = jnp.dot(x, wgb[...], preferred_element_type=F32)
        u = jnp.dot(x, wub[...], preferred_element_type=F32)
        mid = (jax.nn.silu(g) * u).astype(BF16)
        ybufs[slot][...] = jnp.dot(mid, wdb[...], preferred_element_type=F32)
        wait_scatter(other)

    for slot in range(2):
        @pl.when((i <= n_used) & (i % 2 == slot))
        def _(slot=slot):
            step(slot)

    @pl.when(i == n_used)
    def _():
        for slot in range(2):
            @pl.when(i % 2 == slot)
            def _(slot=slot):
                wait_gather(1 - slot)


def expert_mlp(h, blk_e, n_used, tok, dst, w_gate, w_up, w_down, layer):
    n, d = h.shape
    n_steps, _, r = dst.shape
    ff = w_gate.shape[3]

    def smem_rows(index_map):
        return pl.BlockSpec((1, 1, r), index_map, memory_space=pltpu.SMEM)

    def weight(shape):
        return pl.BlockSpec((None, None) + shape, lambda i, be, nu: (layer, be[i], 0, 0))

    grid_spec = pltpu.PrefetchScalarGridSpec(
        num_scalar_prefetch=2,
        grid=(n_steps,),
        in_specs=[
            smem_rows(lambda i, be, nu: (0, 0, 0)),
            smem_rows(lambda i, be, nu: (i + 1, 0, 0)),
            smem_rows(lambda i, be, nu: (i, 0, 0)),
            pl.BlockSpec(memory_space=pl.ANY),
            weight((d, ff)), weight((d, ff)), weight((ff, d)),
        ],
        out_specs=pl.BlockSpec(memory_space=pl.ANY),
        scratch_shapes=[
            pltpu.VMEM((r, d), F32), pltpu.VMEM((r, d), F32),
            pltpu.VMEM((r, d), F32), pltpu.VMEM((r, d), F32),
            pltpu.VMEM((d, ff), BF16),
            pltpu.VMEM((d, ff), BF16),
            pltpu.VMEM((ff, d), BF16),
            pltpu.SemaphoreType.DMA((2,)),
            pltpu.SemaphoreType.DMA((2,)),
        ],
    )
    return pl.pallas_call(
        functools.partial(_expert_kernel, rows=r),
        grid_spec=grid_spec,
        out_shape=jax.ShapeDtypeStruct((TOP_K_IN_GROUP * n + 2 * r, d), F32),
        compiler_params=_params("arbitrary"),
        name="expert_mlp",
    )(blk_e, n_used, tok, tok, dst, h, w_gate, w_up, w_down)


def _combine_kernel(x_ref, y0_ref, y1_ref, w_ref, o_ref):
    w = w_ref[...]
    o_ref[...] = x_ref[...] + (w[:, 0:1] * y0_ref[...] + w[:, 1:2] * y1_ref[...])


def moe_combine(x, y, wts, *, tm=512):
    n, d = x.shape
    tm = min(tm, n)
    nt = n // tm
    return pl.pallas_call(
        _combine_kernel,
        grid=(nt,),
        in_specs=[pl.BlockSpec((tm, d), lambda i: (i, 0)),
                  pl.BlockSpec((tm, d), lambda i: (i, 0)),
                  pl.BlockSpec((tm, d), lambda i: (nt + i, 0)),
                  pl.BlockSpec((tm, ROUTER_LANES), lambda i: (i, 0))],
        out_specs=pl.BlockSpec((tm, d), lambda i: (i, 0)),
        out_shape=jax.ShapeDtypeStruct((n, d), F32),
        compiler_params=_params("arbitrary"),
        name="moe_combine",
    )(x, y, y, wts)


def _expert_row_index(expert_id, rows_per_block):
    n, k = expert_id.shape
    a_total = n * k
    r = rows_per_block
    flat_e = expert_id.reshape(-1)
    order = jnp.argsort(flat_e).astype(jnp.int32)
    experts = jnp.arange(N_EXPERTS, dtype=jnp.int32)
    counts = jnp.sum((flat_e[:, None] == experts[None, :]).astype(jnp.int32), axis=0)
    padded = (counts + r - 1) // r * r
    pad_end = jnp.cumsum(padded)
    pad_start = pad_end - padded
    start = jnp.cumsum(counts) - counts
    n_blk = (a_total + N_EXPERTS * (r - 1) + r - 1) // r
    n_used = (pad_end[-1] // r).astype(jnp.int32)
    blk = jnp.arange(n_blk, dtype=jnp.int32)
    blk_row0 = blk * r
    blk_e = jnp.minimum(jnp.sum((pad_end[None, :] <= blk_row0[:, None]).astype(jnp.int32), axis=1),
                        N_EXPERTS - 1)
    in_e = blk_row0 - pad_start[blk_e]
    nvalid = jnp.clip(counts[blk_e] - in_e, 0, r)
    lane = jnp.arange(r, dtype=jnp.int32)[None, :]
    valid = (lane < nvalid[:, None]) & (blk[:, None] < n_used)
    idx = (start[blk_e] + in_e)[:, None] + lane
    src = order[jnp.clip(idx, 0, a_total - 1)]
    tok = jnp.where(valid, src // k, 0)
    spare = a_total + (blk[:, None] % 2) * r + lane
    dst = jnp.where(valid, (src % k) * n + src // k, spare)
    tok_steps = jnp.concatenate([tok, jnp.zeros((2, r), jnp.int32)], axis=0)
    first_spare = (a_total + r + lane).astype(jnp.int32)
    dst_steps = jnp.concatenate([first_spare, dst], axis=0)
    blk_e_steps = jnp.concatenate([blk_e, blk_e[-1:]], axis=0)
    n_steps = n_blk + 1
    return (blk_e_steps.astype(jnp.int32), n_used.reshape(1),
            tok_steps.reshape(n_steps + 1, 1, r), dst_steps.reshape(n_steps, 1, r))


def hier_moe(x, gain, w_group, b_group, w_expert, b_expert, w_gate, w_up, w_down, layer):
    n, d = x.shape
    pad = ROUTER_LANES - N_GROUPS - N_EXPERTS
    w_router = jnp.concatenate([w_group, w_expert, jnp.zeros((d, pad), F32)], axis=1)
    w_hi = w_router.astype(BF16)
    w_lo = (w_router - w_hi.astype(F32)).astype(BF16)
    b_router = jnp.concatenate([b_group, b_expert, jnp.zeros((pad,), F32)]).reshape(1, ROUTER_LANES)
    h, ids, wts = moe_router(x, gain, jnp.concatenate([w_hi, w_lo], axis=1), b_router)
    blk_e, n_used, tok, dst = _expert_row_index(ids[:, :TOP_K_IN_GROUP], EXPERT_ROWS)
    y = expert_mlp(h, blk_e, n_used, tok, dst, w_gate, w_up, w_down, layer)
    return moe_combine(x, y, wts)


def _rope_tables(t):
    inv_freq = ROPE_THETA ** (-jnp.arange(0, HEAD_DIM, 2, dtype=F32) / HEAD_DIM)
    ang = jnp.arange(t, dtype=F32)[:, None] * inv_freq[None, :]
    cos, sin = jnp.cos(ang), jnp.sin(ang)
    return jnp.concatenate([cos, cos], axis=1), jnp.concatenate([-sin, sin], axis=1)


def kernel(x, norm_mix, norm_ffn, a_w_in, a_lb_logits, a_o_gain, a_w_out, kv_norm, kv_w, k_norm,
           b_w_q, b_q_norm, b_w_out, moe_w_group, moe_b_group, moe_w_expert, moe_b_expert,
           moe_w_gate, moe_w_up, moe_w_down):
    b, t, d = x.shape
    n = b * t
    depth = norm_mix.shape[0]
    n_a = a_w_in.shape[0]
    lower_bounds = jnp.cumsum(jax.nn.softmax(a_lb_logits.astype(F32), axis=0), axis=0)
    cos, sin = _rope_tables(t)
    xf = x.reshape(n, d)
    kv = None
    for l in range(depth):
        if l < n_a:
            z = rms_matmul(xf, norm_mix[l], a_w_in, l, F32)
            o = hgrn2_scan(z.reshape(b, t, 4 * d), lower_bounds[l], a_o_gain[l])
            xf = matmul_residual(o.reshape(n, d), a_w_out, l, xf)
        else:
            if l == n_a:
                kv = rms_matmul(xf, kv_norm, kv_w[None], 0, BF16,
                                rope=(k_norm, cos, sin, d), seq_len=t)
            lb_ = l - n_a
            q = rms_matmul(xf, norm_mix[l], b_w_q, lb_, BF16,
                           rope=(b_q_norm[lb_], cos, sin, d), seq_len=t)
            o = moba_attention(q.reshape(b, t, d), kv.reshape(b, t, 2 * d))
            xf = matmul_residual(o.reshape(n, d), b_w_out, lb_, xf)
        xf = hier_moe(xf, norm_ffn[l], moe_w_group[l], moe_b_group[l], moe_w_expert[l],
                      moe_b_expert[l], moe_w_gate, moe_w_up, moe_w_down, l)
    return xf.reshape(b, t, d)
```

```python
import functools

import jax
import jax.numpy as jnp
from jax import lax
from jax.experimental import pallas as pl
from jax.experimental.pallas import tpu as pltpu

HEAD_DIM = 128
HGRN_CHUNK = 64
HGRN_SUB = 8
HGRN_UNROLL = 8
MOBA_BLOCK = 256
MOBA_TOP_K = 3
ROPE_THETA = 10000.0
N_GROUPS = 4
EXPERTS_PER_GROUP = 8
N_EXPERTS = N_GROUPS * EXPERTS_PER_GROUP
TOP_K_IN_GROUP = 2
RMS_EPS = 1e-6
LANES = 128
ROUTER_LANES = LANES
EXPERT_ROWS = 256
DMA_THREADS = 2
ROPE_ROW_PARTS = 4

V7X_VMEM_LIMIT = 56 * 1024 * 1024

F32 = jnp.float32
BF16 = jnp.bfloat16
HIGHEST = lax.Precision.HIGHEST
NT_DIMS = (((1,), (1,)), ((), ()))


def _params(*sem):
    return pltpu.CompilerParams(dimension_semantics=sem, vmem_limit_bytes=V7X_VMEM_LIMIT)


def _rms_rows_to(x_ref, g_ref, out_ref, rows, chunk):
    gain = g_ref[...]

    def body(c, carry):
        sl = pl.ds(pl.multiple_of(c * chunk, chunk), chunk)
        x = x_ref[sl, :]
        ms = jnp.mean(x * x, axis=-1, keepdims=True)
        out_ref[sl, :] = (x * lax.rsqrt(ms + RMS_EPS) * gain).astype(out_ref.dtype)
        return carry

    lax.fori_loop(0, rows // chunk, body, 0)


def _head_norm_rope(y, hg, cos, sin):
    ms = jnp.mean(y * y, axis=-1, keepdims=True)
    yn = y * lax.rsqrt(ms + RMS_EPS) * hg
    return yn * cos + pltpu.roll(yn, HEAD_DIM // 2, axis=1) * sin


def _rms_matmul_kernel(*refs, tm, tn, n_rope_tiles):
    if n_rope_tiles:
        x_ref, g_ref, w_ref, hg_ref, cos_ref, sin_ref, o_ref, hn_ref = refs
    else:
        x_ref, g_ref, w_ref, o_ref, hn_ref = refs
    j = pl.program_id(1)

    @pl.when(j == 0)
    def _():
        _rms_rows_to(x_ref, g_ref, hn_ref, tm, 64)

    def plain():
        o_ref[...] = jnp.dot(hn_ref[...], w_ref[...].astype(BF16),
                             preferred_element_type=F32).astype(o_ref.dtype)

    if not n_rope_tiles:
        plain()
        return

    @pl.when(j < n_rope_tiles)
    def _():
        w = w_ref[...].astype(BF16)
        hg = hg_ref[...]
        part = tm // ROPE_ROW_PARTS
        for p in range(ROPE_ROW_PARTS):
            rows = slice(p * part, (p + 1) * part)
            y = jnp.dot(hn_ref[rows, :], w, preferred_element_type=F32)
            for h in range(tn // HEAD_DIM):
                sl = slice(h * HEAD_DIM, (h + 1) * HEAD_DIM)
                o_ref[rows, sl] = _head_norm_rope(y[:, sl], hg, cos_ref[rows, :],
                                                  sin_ref[rows, :]).astype(o_ref.dtype)

    pl.when(j >= n_rope_tiles)(plain)


def rms_matmul(x, gain, w, layer, out_dtype, *, rope=None, seq_len=None, tm=1024, tn=512):
    n, d = x.shape
    e = w.shape[2]
    tm = min(tm, n)
    tn = min(tn, e)
    in_specs = [
        pl.BlockSpec((tm, d), lambda i, j: (i, 0)),
        pl.BlockSpec((1, d), lambda i, j: (0, 0)),
        pl.BlockSpec((None, d, tn), lambda i, j: (layer, 0, j)),
    ]
    args = [x, gain.reshape(1, d), w]
    n_rope_tiles = 0
    if rope is not None:
        hg, cos, sin, n_rope_cols = rope
        tm = min(tm, seq_len)
        tn = min(tn, n_rope_cols)
        in_specs[0] = pl.BlockSpec((tm, d), lambda i, j: (i, 0))
        in_specs[2] = pl.BlockSpec((None, d, tn), lambda i, j: (layer, 0, j))
        n_rope_tiles = n_rope_cols // tn
        pos_tiles = seq_len // tm
        in_specs += [
            pl.BlockSpec((1, HEAD_DIM), lambda i, j: (0, 0)),
            pl.BlockSpec((tm, HEAD_DIM), lambda i, j: (i % pos_tiles, 0)),
            pl.BlockSpec((tm, HEAD_DIM), lambda i, j: (i % pos_tiles, 0)),
        ]
        args += [hg.reshape(1, HEAD_DIM), cos, sin]
    return pl.pallas_call(
        functools.partial(_rms_matmul_kernel, tm=tm, tn=tn, n_rope_tiles=n_rope_tiles),
        grid=(n // tm, e // tn),
        in_specs=in_specs,
        out_specs=pl.BlockSpec((tm, tn), lambda i, j: (i, j)),
        out_shape=jax.ShapeDtypeStruct((n, e), out_dtype),
        scratch_shapes=[pltpu.VMEM((tm, d), BF16)],
        compiler_params=_params("arbitrary", "arbitrary"),
        name="rms_matmul",
    )(*args)


def _matmul_res_kernel(a_ref, w_ref, r_ref, o_ref):
    o_ref[...] = r_ref[...] + jnp.dot(a_ref[...], w_ref[...].astype(BF16),
                                      preferred_element_type=F32)


def matmul_residual(a, w, layer, res, *, tm=1024, tn=512):
    n, d = a.shape
    e = w.shape[2]
    tm = min(tm, n)
    tn = min(tn, e)
    return pl.pallas_call(
        _matmul_res_kernel,
        grid=(n // tm, e // tn),
        in_specs=[
            pl.BlockSpec((tm, d), lambda i, j: (i, 0)),
            pl.BlockSpec((None, d, tn), lambda i, j: (layer, 0, j)),
            pl.BlockSpec((tm, tn), lambda i, j: (i, j)),
        ],
        out_specs=pl.BlockSpec((tm, tn), lambda i, j: (i, j)),
        out_shape=jax.ShapeDtypeStruct((n, e), F32),
        compiler_params=_params("arbitrary", "arbitrary"),
        name="matmul_residual",
    )(a, w, res)


def _hgrn_kernel(zq_ref, zf_ref, zi_ref, zg_ref, lb_ref, og_ref, o_ref,
                 st_ref, g_scr, h_scr, a_scr, u_scr, *, n_groups):
    C = HGRN_CHUNK
    SB = HGRN_SUB
    NB = C // SB

    @pl.when(pl.program_id(2) == 0)
    def _():
        st_ref[...] = jnp.zeros_like(st_ref)

    lb = lb_ref[...]
    og = og_ref[...]
    row_c = lax.broadcasted_iota(jnp.int32, (C, C), 0)
    col_c = lax.broadcasted_iota(jnp.int32, (C, C), 1)
    causal = row_c >= col_c
    tril = causal.astype(F32)
    lane_s = lax.broadcasted_iota(jnp.int32, (SB, C), 1)
    zero_blk = jnp.zeros((SB, HEAD_DIM), F32)
    halves = [h for h in (32, 16, 8) if SB <= h < C]
    level_mask = {half: ((row_c // (2 * half) == col_c // (2 * half))
                         & ((row_c // half) % 2 == 1) & ((col_c // half) % 2 == 0))
                  for half in halves}

    def blk(x, b):
        return x[b * SB:(b + 1) * SB, :]

    def gates(rows, u):
        f = lb + (1.0 - lb) * jax.nn.sigmoid(zf_ref[0, rows, :])
        G2 = jnp.dot(tril, jnp.log2(f), precision=HIGHEST, preferred_element_type=F32)
        g_scr[u] = G2
        h_scr[u] = G2 - jnp.log2(1.0 - f)

    def scores(rows, u):
        q = zq_ref[0, rows, :]
        G2 = g_scr[u]
        H2 = h_scr[u]
        A = jnp.zeros((C, C), F32)
        for half in halves:
            per_half = half // SB
            qa, kb = [], []
            for b in range(NB):
                grp = b // per_half
                if grp % 2 == 1:
                    r = grp * half - 1
                    qa.append(blk(q, b) * jnp.exp2(blk(G2, b) - g_scr[u, r:r + 1, :]))
                    kb.append(zero_blk)
                else:
                    r = (grp + 1) * half - 1
                    kb.append(jnp.exp2(g_scr[u, r:r + 1, :] - blk(H2, b)))
                    qa.append(zero_blk)
            sq = lax.dot_general(jnp.concatenate(qa, axis=0).astype(BF16),
                                 jnp.concatenate(kb, axis=0).astype(BF16), NT_DIMS,
                                 preferred_element_type=F32)
            A = A + jnp.where(level_mask[half], sq, 0.0)
        a_parts = []
        for b in range(NB):
            Gt = blk(G2, b)
            qt = blk(q, b)
            a_b = jnp.zeros((SB, C), F32)
            for s in range(b * SB, (b + 1) * SB):
                e = jnp.exp2(Gt - h_scr[u, s:s + 1, :])
                a_b = jnp.where(lane_s == s, jnp.sum(qt * e, axis=-1, keepdims=True), a_b)
            a_parts.append(a_b)
        A = jnp.where(causal, A + jnp.concatenate(a_parts, axis=0), 0.0)
        a_scr[u] = A.astype(BF16)

    def state_input(rows, u):
        G2_end = g_scr[u, C - 1:C, :]
        kd = jnp.exp2(G2_end - h_scr[u])
        u_scr[u] = jnp.dot(zi_ref[0, rows, :].T.astype(BF16), kd.astype(BF16),
                           preferred_element_type=F32)

    def output(rows, u):
        G2 = g_scr[u]
        st = st_ref[...]
        qg = (zq_ref[0, rows, :] * jnp.exp2(G2)).astype(BF16)
        o = (lax.dot_general(qg, st.astype(BF16), NT_DIMS, preferred_element_type=F32)
             + jnp.dot(a_scr[u], zi_ref[0, rows, :].astype(BF16), preferred_element_type=F32))
        st_ref[...] = jnp.exp2(G2[C - 1:C, :]) * st + u_scr[u]
        ms = jnp.mean(o * o, axis=-1, keepdims=True)
        zg = zg_ref[0, rows, :]
        o_ref[0, rows, :] = (o * lax.rsqrt(ms + RMS_EPS) * og * jax.nn.silu(zg)).astype(o_ref.dtype)

    def group(gi, carry):
        base = pl.multiple_of(gi * (HGRN_UNROLL * C), HGRN_UNROLL * C)
        for phase in (gates, scores, state_input, output):
            for u in range(HGRN_UNROLL):
                phase(pl.ds(base + u * C, C), u)
        return carry

    lax.fori_loop(0, n_groups, group, 0)


def hgrn2_scan(z, lb, o_gain, *, tt=4096):
    b, t, d4 = z.shape
    d = d4 // 4
    nh = d // HEAD_DIM
    tt = min(tt, t)

    def zspec(part):
        return pl.BlockSpec((1, tt, HEAD_DIM), lambda bi, h, ti: (bi, ti, part * nh + h))

    return pl.pallas_call(
        functools.partial(_hgrn_kernel, n_groups=tt // (HGRN_CHUNK * HGRN_UNROLL)),
        grid=(b, nh, t // tt),
        in_specs=[zspec(0), zspec(1), zspec(2), zspec(3),
                  pl.BlockSpec((1, HEAD_DIM), lambda bi, h, ti: (0, h)),
                  pl.BlockSpec((1, HEAD_DIM), lambda bi, h, ti: (0, 0))],
        out_specs=pl.BlockSpec((1, tt, HEAD_DIM), lambda bi, h, ti: (bi, ti, h)),
        out_shape=jax.ShapeDtypeStruct((b, t, d), BF16),
        scratch_shapes=[pltpu.VMEM((HEAD_DIM, HEAD_DIM), F32),
                        pltpu.VMEM((HGRN_UNROLL, HGRN_CHUNK, HEAD_DIM), F32),
                        pltpu.VMEM((HGRN_UNROLL, HGRN_CHUNK, HEAD_DIM), F32),
                        pltpu.VMEM((HGRN_UNROLL, HGRN_CHUNK, HGRN_CHUNK), BF16),
                        pltpu.VMEM((HGRN_UNROLL, HEAD_DIM, HEAD_DIM), F32)],
        compiler_params=_params("arbitrary", "arbitrary", "arbitrary"),
        name="hgrn2_scan",
    )(z, z, z, z, lb.reshape(1, d), o_gain.reshape(1, HEAD_DIM))


def _moba_kernel(q_ref, k_ref, v_ref, o_ref, km_ref, vt_ref, qt_ref, bias_ref, m_ref, l_ref, acc_ref,
                 *, n_blocks, heads):
    BLK = MOBA_BLOCK
    qi = pl.program_id(2)
    exp2_scale = HEAD_DIM ** -0.5 * 1.4426950408889634

    def head(h):
        return slice(h * HEAD_DIM, (h + 1) * HEAD_DIM)

    eye = (lax.broadcasted_iota(jnp.int32, (HEAD_DIM, HEAD_DIM), 0)
           == lax.broadcasted_iota(jnp.int32, (HEAD_DIM, HEAD_DIM), 1)).astype(BF16)

    @pl.when(qi == 0)
    def _():
        for h in range(heads):
            for j in range(n_blocks):
                rows = slice(j * BLK, (j + 1) * BLK)
                kb = k_ref[0, rows, head(h)].astype(F32)
                km_ref[h, j:j + 1, :] = jnp.mean(kb, axis=0, keepdims=True)
                vt_ref[h, :, rows] = lax.dot_general(eye, v_ref[0, rows, head(h)], NT_DIMS,
                                                     preferred_element_type=F32).astype(BF16)

    blk_id = lax.broadcasted_iota(jnp.int32, (n_blocks, BLK), 0)
    past = blk_id < qi
    key_i = lax.broadcasted_iota(jnp.int32, (BLK, BLK), 0)
    qry_i = lax.broadcasted_iota(jnp.int32, (BLK, BLK), 1)
    own = pl.ds(pl.multiple_of(qi * BLK, BLK), BLK)

    def scores(rows):
        return [jnp.dot(k_ref[0, rows, head(h)], qt_ref[h], preferred_element_type=F32)
                for h in range(heads)]

    for h in range(heads):
        qt_ref[h] = lax.dot_general(eye, q_ref[0, :, head(h)], NT_DIMS,
                                    preferred_element_type=F32).astype(BF16)
    for h in range(heads):
        gate = jnp.dot(km_ref[h], qt_ref[h].astype(F32), precision=HIGHEST,
                       preferred_element_type=F32)
        gm = jnp.where(past, gate, -jnp.inf)
        cnt = jnp.zeros((n_blocks, BLK), jnp.int32)
        for m in range(n_blocks):
            gmm = gm[m:m + 1, :]
            ahead = (gmm > gm) | ((gmm == gm) & (blk_id > m))
            cnt = cnt + ahead.astype(jnp.int32)
        bias_ref[h] = jnp.where(past & (cnt < MOBA_TOP_K), 0.0, -jnp.inf).astype(F32)

    probs = []
    for h, s in enumerate(scores(own)):
        s = jnp.where(key_i <= qry_i, s, -jnp.inf)
        m0 = jnp.max(s, axis=0, keepdims=True)
        p = jnp.exp2((s - m0) * exp2_scale)
        m_ref[h] = m0
        l_ref[h] = jnp.sum(p, axis=0, keepdims=True)
        probs.append(p.astype(BF16))
    for h in range(heads):
        acc_ref[h] = jnp.dot(vt_ref[h, :, own], probs[h], preferred_element_type=F32)

    def past_blocks(js):
        rows = [slice(j * BLK, (j + 1) * BLK) for j in js]
        s_all = [scores(r) for r in rows]
        for j, r, s_heads in zip(js, rows, s_all):
            probs, alphas = [], []
            for h, sj in enumerate(s_heads):
                sj = sj + bias_ref[h, j:j + 1, :]
                m_old = m_ref[h]
                m_new = jnp.maximum(m_old, jnp.max(sj, axis=0, keepdims=True))
                alpha = jnp.exp2((m_old - m_new) * exp2_scale)
                pj = jnp.exp2((sj - m_new) * exp2_scale)
                m_ref[h] = m_new
                l_ref[h] = alpha * l_ref[h] + jnp.sum(pj, axis=0, keepdims=True)
                probs.append(pj.astype(BF16))
                alphas.append(alpha)
            for h in range(heads):
                acc_ref[h] = alphas[h] * acc_ref[h] + jnp.dot(vt_ref[h, :, r], probs[h],
                                                              preferred_element_type=F32)

    for j in range(0, n_blocks - 1, 2):
        if j + 1 < n_blocks - 1:
            @pl.when(j + 1 < qi)
            def _(j=j):
                past_blocks((j, j + 1))

        @pl.when(qi == j + 1)
        def _(j=j):
            past_blocks((j,))

    for h in range(heads):
        o_ref[0, :, head(h)] = (acc_ref[h] / l_ref[h]).T.astype(o_ref.dtype)


def moba_attention(q, kv, *, heads=4):
    b, t, d = q.shape
    nh = d // HEAD_DIM
    heads = min(heads, nh)
    ng = nh // heads
    nb = t // MOBA_BLOCK
    w = heads * HEAD_DIM
    return pl.pallas_call(
        functools.partial(_moba_kernel, n_blocks=nb, heads=heads),
        grid=(b, ng, nb),
        in_specs=[pl.BlockSpec((1, MOBA_BLOCK, w), lambda bi, g, qi: (bi, qi, g)),
                  pl.BlockSpec((1, t, w), lambda bi, g, qi: (bi, 0, g)),
                  pl.BlockSpec((1, t, w), lambda bi, g, qi: (bi, 0, ng + g))],
        out_specs=pl.BlockSpec((1, MOBA_BLOCK, w), lambda bi, g, qi: (bi, qi, g)),
        out_shape=jax.ShapeDtypeStruct((b, t, d), BF16),
        scratch_shapes=[pltpu.VMEM((heads, nb, HEAD_DIM), F32),
                        pltpu.VMEM((heads, HEAD_DIM, t), BF16),
                        pltpu.VMEM((heads, HEAD_DIM, MOBA_BLOCK), BF16),
                        pltpu.VMEM((heads, nb, MOBA_BLOCK), F32),
                        pltpu.VMEM((heads, 1, MOBA_BLOCK), F32),
                        pltpu.VMEM((heads, 1, MOBA_BLOCK), F32),
                        pltpu.VMEM((heads, HEAD_DIM, MOBA_BLOCK), F32)],
        compiler_params=_params("arbitrary", "arbitrary", "arbitrary"),
        name="moba_attention",
    )(q, kv, kv)


def _router_kernel(x_ref, g_ref, wr_ref, br_ref, h3_ref, ids_ref, wts_ref, hn_ref, *, tm):
    _rms_rows_to(x_ref, g_ref, hn_ref, tm, 64)
    h = hn_ref[...]
    for c in range(h3_ref.shape[1]):
        h3_ref[:, c, :] = h[:, c * LANES:(c + 1) * LANES]
    hi = h.astype(BF16)
    lo = (h - hi.astype(F32)).astype(BF16)
    w = wr_ref[...]
    hw = jnp.dot(hi, w, preferred_element_type=F32)
    logits = (hw[:, :ROUTER_LANES] + hw[:, ROUTER_LANES:]
              + jnp.dot(lo, w[:, :ROUTER_LANES], preferred_element_type=F32)
              + br_ref[...])
    lane = lax.broadcasted_iota(jnp.int32, logits.shape, 1)
    neg = -jnp.inf
    big = ROUTER_LANES

    gl = jnp.where(lane < N_GROUPS, logits, neg)
    gmax = jnp.max(gl, axis=-1, keepdims=True)
    gsum = jnp.sum(jnp.exp(gl - gmax), axis=-1, keepdims=True)
    g_top_p = 1.0 / gsum
    g_top = jnp.min(jnp.where(gl == gmax, lane, big), axis=-1, keepdims=True)

    lo = N_GROUPS + EXPERTS_PER_GROUP * g_top
    el = jnp.where((lane >= lo) & (lane < lo + EXPERTS_PER_GROUP), logits, neg)
    emax = jnp.max(el, axis=-1, keepdims=True)
    esum = jnp.sum(jnp.exp(el - emax), axis=-1, keepdims=True)
    i1 = jnp.min(jnp.where(el == emax, lane, big), axis=-1, keepdims=True)
    el2 = jnp.where(lane == i1, neg, el)
    emax2 = jnp.max(el2, axis=-1, keepdims=True)
    i2 = jnp.min(jnp.where(el2 == emax2, lane, big), axis=-1, keepdims=True)
    p1 = 1.0 / esum
    p2 = jnp.exp(emax2 - emax) / esum
    psum = p1 + p2
    w1 = g_top_p * p1 / psum
    w2 = g_top_p * p2 / psum

    ids_ref[...] = jnp.where(lane == 0, i1 - N_GROUPS, jnp.where(lane == 1, i2 - N_GROUPS, 0))
    wts_ref[...] = jnp.where(lane == 0, w1, jnp.where(lane == 1, w2, 0.0))


def moe_router(x, gain, w_router, b_router, *, tm=512):
    n, d = x.shape
    tm = min(tm, n)
    return pl.pallas_call(
        functools.partial(_router_kernel, tm=tm),
        grid=(n // tm,),
        in_specs=[pl.BlockSpec((tm, d), lambda i: (i, 0)),
                  pl.BlockSpec((1, d), lambda i: (0, 0)),
                  pl.BlockSpec((d, 2 * ROUTER_LANES), lambda i: (0, 0)),
                  pl.BlockSpec((1, ROUTER_LANES), lambda i: (0, 0))],
        out_specs=[pl.BlockSpec((tm, d // LANES, LANES), lambda i: (i, 0, 0)),
                   pl.BlockSpec((tm, ROUTER_LANES), lambda i: (i, 0)),
                   pl.BlockSpec((tm, ROUTER_LANES), lambda i: (i, 0))],
        out_shape=[jax.ShapeDtypeStruct((n, d // LANES, LANES), F32),
                   jax.ShapeDtypeStruct((n, ROUTER_LANES), jnp.int32),
                   jax.ShapeDtypeStruct((n, ROUTER_LANES), F32)],
        scratch_shapes=[pltpu.VMEM((tm, d), F32)],
        compiler_params=_params("arbitrary"),
        name="moe_router",
    )(x, gain.reshape(1, d), w_router, b_router)


def _expert_kernel(blk_e_ref, n_used_ref, tok0_ref, tok_ref, dst_ref, h_hbm, wg_ref, wu_ref, wd_ref,
                   out_hbm, xbuf0, xbuf1, ybuf0, ybuf1, wgb, wub, wdb, sem_in, sem_out, *, rows):
    i = pl.program_id(0)
    n_used = n_used_ref[0]
    xbufs = (xbuf0, xbuf1)
    ybufs = (ybuf0, ybuf1)

    def gather(idx_ref, slot):
        for r in range(rows):
            pltpu.make_async_copy(h_hbm.at[idx_ref[0, 0, r]], xbufs[slot].at[r],
                                  sem_in.at[slot]).start(priority=r % DMA_THREADS)

    def scatter(slot):
        for r in range(rows):
            pltpu.make_async_copy(ybufs[slot].at[r], out_hbm.at[dst_ref[0, 0, r]],
                                  sem_out.at[slot]).start(priority=r % DMA_THREADS)

    def wait_gather(slot):
        pltpu.make_async_copy(h_hbm.at[pl.ds(0, rows)], xbufs[slot], sem_in.at[slot]).wait()

    def wait_scatter(slot):
        pltpu.make_async_copy(ybufs[slot], out_hbm.at[pl.ds(0, rows)], sem_out.at[slot]).wait()

    @pl.when(i == 0)
    def _():
        ybuf0[...] = jnp.zeros_like(ybuf0)
        ybuf1[...] = jnp.zeros_like(ybuf1)
        spare0 = pltpu.make_async_copy(ybuf0, out_hbm.at[pl.ds(out_hbm.shape[0] - 2 * rows, rows)],
                                       sem_out.at[0])
        spare0.start()
        spare0.wait()
        gather(tok0_ref, 0)

    @pl.when((i == 0) | (blk_e_ref[i] != blk_e_ref[jnp.maximum(i - 1, 0)]))
    def _():
        wgb[...] = wg_ref[...].astype(BF16)
        wub[...] = wu_ref[...].astype(BF16)
        wdb[...] = wd_ref[...].astype(BF16)

    def step(slot):
        other = 1 - slot
        wait_gather(slot)
        gather(tok_ref, other)
        scatter(other)
        n_slab = xbufs[slot].shape[1]
        x = jnp.concatenate([xbufs[slot][:, c, :] for c in range(n_slab)], axis=1).astype(BF16)
        g = jnp.dot(x, wgb[...], preferred_element_type=F32)
        u = jnp.dot(x, wub[...], preferred_element_type=F32)
        mid = (jax.nn.silu(g) * u).astype(BF16)
        y = jnp.dot(mid, wdb[...], preferred_element_type=F32)
        for c in range(n_slab):
            ybufs[slot][:, c, :] = y[:, c * LANES:(c + 1) * LANES]
        wait_scatter(other)

    for slot in range(2):
        @pl.when((i <= n_used) & (i % 2 == slot))
        def _(slot=slot):
            step(slot)

    @pl.when(i == n_used)
    def _():
        for slot in range(2):
            @pl.when(i % 2 == slot)
            def _(slot=slot):
                wait_gather(1 - slot)


def expert_mlp(h, blk_e, n_used, tok, dst, w_gate, w_up, w_down, layer):
    n, n_slab, _ = h.shape
    d = n_slab * LANES
    n_steps, _, r = dst.shape
    ff = w_gate.shape[3]

    def smem_rows(index_map):
        return pl.BlockSpec((1, 1, r), index_map, memory_space=pltpu.SMEM)

    def weight(shape):
        return pl.BlockSpec((None, None) + shape, lambda i, be, nu: (layer, be[i], 0, 0))

    grid_spec = pltpu.PrefetchScalarGridSpec(
        num_scalar_prefetch=2,
        grid=(n_steps,),
        in_specs=[
            smem_rows(lambda i, be, nu: (0, 0, 0)),
            smem_rows(lambda i, be, nu: (i + 1, 0, 0)),
            smem_rows(lambda i, be, nu: (i, 0, 0)),
            pl.BlockSpec(memory_space=pl.ANY),
            weight((d, ff)), weight((d, ff)), weight((ff, d)),
        ],
        out_specs=pl.BlockSpec(memory_space=pl.ANY),
        scratch_shapes=[
            pltpu.VMEM((r, n_slab, LANES), F32), pltpu.VMEM((r, n_slab, LANES), F32),
            pltpu.VMEM((r, n_slab, LANES), F32), pltpu.VMEM((r, n_slab, LANES), F32),
            pltpu.VMEM((d, ff), BF16),
            pltpu.VMEM((d, ff), BF16),
            pltpu.VMEM((ff, d), BF16),
            pltpu.SemaphoreType.DMA((2,)),
            pltpu.SemaphoreType.DMA((2,)),
        ],
    )
    return pl.pallas_call(
        functools.partial(_expert_kernel, rows=r),
        grid_spec=grid_spec,
        out_shape=jax.ShapeDtypeStruct((TOP_K_IN_GROUP * n + 2 * r, n_slab, LANES), F32),
        compiler_params=_params("arbitrary"),
        name="expert_mlp",
    )(blk_e, n_used, tok, tok, dst, h, w_gate, w_up, w_down)


def _combine_kernel(x_ref, y0_ref, y1_ref, w_ref, o_ref):
    w = w_ref[...]
    w0, w1 = w[:, 0:1], w[:, 1:2]
    for c in range(y0_ref.shape[1]):
        sl = slice(c * LANES, (c + 1) * LANES)
        o_ref[:, sl] = x_ref[:, sl] + (w0 * y0_ref[:, c, :] + w1 * y1_ref[:, c, :])


def moe_combine(x, y, wts, *, tm=512):
    n, d = x.shape
    tm = min(tm, n)
    nt = n // tm
    slab = (tm,) + y.shape[1:]
    return pl.pallas_call(
        _combine_kernel,
        grid=(nt,),
        in_specs=[pl.BlockSpec((tm, d), lambda i: (i, 0)),
                  pl.BlockSpec(slab, lambda i: (i, 0, 0)),
                  pl.BlockSpec(slab, lambda i: (nt + i, 0, 0)),
                  pl.BlockSpec((tm, ROUTER_LANES), lambda i: (i, 0))],
        out_specs=pl.BlockSpec((tm, d), lambda i: (i, 0)),
        out_shape=jax.ShapeDtypeStruct((n, d), F32),
        compiler_params=_params("arbitrary"),
        name="moe_combine",
    )(x, y, y, wts)


def _expert_row_index(expert_id, rows_per_block):
    n, k = expert_id.shape
    a_total = n * k
    r = rows_per_block
    flat_e = expert_id.reshape(-1)
    order = jnp.argsort(flat_e).astype(jnp.int32)
    experts = jnp.arange(N_EXPERTS, dtype=jnp.int32)
    counts = jnp.sum((flat_e[:, None] == experts[None, :]).astype(jnp.int32), axis=0)
    padded = (counts + r - 1) // r * r
    pad_end = jnp.cumsum(padded)
    pad_start = pad_end - padded
    start = jnp.cumsum(counts) - counts
    n_blk = (a_total + N_EXPERTS * (r - 1) + r - 1) // r
    n_used = (pad_end[-1] // r).astype(jnp.int32)
    blk = jnp.arange(n_blk, dtype=jnp.int32)
    blk_row0 = blk * r
    blk_e = jnp.minimum(jnp.sum((pad_end[None, :] <= blk_row0[:, None]).astype(jnp.int32), axis=1),
                        N_EXPERTS - 1)
    in_e = blk_row0 - pad_start[blk_e]
    nvalid = jnp.clip(counts[blk_e] - in_e, 0, r)
    lane = jnp.arange(r, dtype=jnp.int32)[None, :]
    valid = (lane < nvalid[:, None]) & (blk[:, None] < n_used)
    idx = (start[blk_e] + in_e)[:, None] + lane
    src = order[jnp.clip(idx, 0, a_total - 1)]
    tok = jnp.where(valid, src // k, 0)
    spare = a_total + (blk[:, None] % 2) * r + lane
    dst = jnp.where(valid, (src % k) * n + src // k, spare)
    tok_steps = jnp.concatenate([tok, jnp.zeros((2, r), jnp.int32)], axis=0)
    first_spare = (a_total + r + lane).astype(jnp.int32)
    dst_steps = jnp.concatenate([first_spare, dst], axis=0)
    blk_e_steps = jnp.concatenate([blk_e, blk_e[-1:]], axis=0)
    n_steps = n_blk + 1
    return (blk_e_steps.astype(jnp.int32), n_used.reshape(1),
            tok_steps.reshape(n_steps + 1, 1, r), dst_steps.reshape(n_steps, 1, r))


def hier_moe(x, gain, w_group, b_group, w_expert, b_expert, w_gate, w_up, w_down, layer):
    n, d = x.shape
    pad = ROUTER_LANES - N_GROUPS - N_EXPERTS
    w_router = jnp.concatenate([w_group, w_expert, jnp.zeros((d, pad), F32)], axis=1)
    w_hi = w_router.astype(BF16)
    w_lo = (w_router - w_hi.astype(F32)).astype(BF16)
    b_router = jnp.concatenate([b_group, b_expert, jnp.zeros((pad,), F32)]).reshape(1, ROUTER_LANES)
    h, ids, wts = moe_router(x, gain, jnp.concatenate([w_hi, w_lo], axis=1), b_router)
    blk_e, n_used, tok, dst = _expert_row_index(ids[:, :TOP_K_IN_GROUP], EXPERT_ROWS)
    y = expert_mlp(h, blk_e, n_used, tok, dst, w_gate, w_up, w_down, layer)
    return moe_combine(x, y, wts)


def _rope_tables(t):
    inv_freq = ROPE_THETA ** (-jnp.arange(0, HEAD_DIM, 2, dtype=F32) / HEAD_DIM)
    ang = jnp.arange(t, dtype=F32)[:, None] * inv_freq[None, :]
    cos, sin = jnp.cos(ang), jnp.sin(ang)
    return jnp.concatenate([cos, cos], axis=1), jnp.concatenate([-sin, sin], axis=1)


def kernel(x, norm_mix, norm_ffn, a_w_in, a_lb_logits, a_o_gain, a_w_out, kv_norm, kv_w, k_norm,
           b_w_q, b_q_norm, b_w_out, moe_w_group, moe_b_group, moe_w_expert, moe_b_expert,
           moe_w_gate, moe_w_up, moe_w_down):
    b, t, d = x.shape
    n = b * t
    depth = norm_mix.shape[0]
    n_a = a_w_in.shape[0]
    lower_bounds = jnp.cumsum(jax.nn.softmax(a_lb_logits.astype(F32), axis=0), axis=0)
    cos, sin = _rope_tables(t)
    xf = x.reshape(n, d)
    kv = None
    for l in range(depth):
        if l < n_a:
            z = rms_matmul(xf, norm_mix[l], a_w_in, l, F32)
            o = hgrn2_scan(z.reshape(b, t, 4 * d), lower_bounds[l], a_o_gain[l])
            xf = matmul_residual(o.reshape(n, d), a_w_out, l, xf)
        else:
            if l == n_a:
                kv = rms_matmul(xf, kv_norm, kv_w[None], 0, BF16,
                                rope=(k_norm, cos, sin, d), seq_len=t)
            lb_ = l - n_a
            q = rms_matmul(xf, norm_mix[l], b_w_q, lb_, BF16,
                           rope=(b_q_norm[lb_], cos, sin, d), seq_len=t)
            o = moba_attention(q.reshape(b, t, d), kv.reshape(b, t, 2 * d))
            xf = matmul_residual(o.reshape(n, d), b_w_out, lb_, xf)
        xf = hier_moe(xf, norm_ffn[l], moe_w_group[l], moe_b_group[l], moe_w_expert[l],
                      moe_b_expert[l], moe_w_gate, moe_w_up, moe_w_down, l)
    return xf.reshape(b, t, d)
```

```python
import functools

import jax
import jax.numpy as jnp
from jax import lax
from jax.experimental import pallas as pl
from jax.experimental.pallas import tpu as pltpu

HEAD_DIM = 128
HGRN_CHUNK = 64
HGRN_SUB = 8
HGRN_UNROLL = 8
MOBA_BLOCK = 256
MOBA_TOP_K = 3
ROPE_THETA = 10000.0
N_GROUPS = 4
EXPERTS_PER_GROUP = 8
N_EXPERTS = N_GROUPS * EXPERTS_PER_GROUP
TOP_K_IN_GROUP = 2
RMS_EPS = 1e-6
ROUTER_LANES = 128
EXPERT_ROWS = 256
COMBINE_ROWS = 256
DMA_THREADS = 2
ROPE_ROW_PARTS = 4

V7X_VMEM_LIMIT = 56 * 1024 * 1024

F32 = jnp.float32
BF16 = jnp.bfloat16
HIGHEST = lax.Precision.HIGHEST
NT_DIMS = (((1,), (1,)), ((), ()))


def _params(*sem):
    return pltpu.CompilerParams(dimension_semantics=sem, vmem_limit_bytes=V7X_VMEM_LIMIT)


def _rms_rows_to(x_ref, g_ref, out_ref, rows, chunk):
    gain = g_ref[...]

    def body(c, carry):
        sl = pl.ds(pl.multiple_of(c * chunk, chunk), chunk)
        x = x_ref[sl, :]
        ms = jnp.mean(x * x, axis=-1, keepdims=True)
        out_ref[sl, :] = (x * lax.rsqrt(ms + RMS_EPS) * gain).astype(out_ref.dtype)
        return carry

    lax.fori_loop(0, rows // chunk, body, 0)


def _head_norm_rope(y, hg, cos, sin):
    ms = jnp.mean(y * y, axis=-1, keepdims=True)
    yn = y * lax.rsqrt(ms + RMS_EPS) * hg
    return yn * cos + pltpu.roll(yn, HEAD_DIM // 2, axis=1) * sin


def _rms_matmul_kernel(*refs, tm, tn, n_rope_tiles):
    if n_rope_tiles:
        x_ref, g_ref, w_ref, hg_ref, cos_ref, sin_ref, o_ref, hn_ref = refs
    else:
        x_ref, g_ref, w_ref, o_ref, hn_ref = refs
    j = pl.program_id(1)

    @pl.when(j == 0)
    def _():
        _rms_rows_to(x_ref, g_ref, hn_ref, tm, 64)

    def plain():
        o_ref[...] = jnp.dot(hn_ref[...], w_ref[...],
                             preferred_element_type=F32).astype(o_ref.dtype)

    if not n_rope_tiles:
        plain()
        return

    @pl.when(j < n_rope_tiles)
    def _():
        w = w_ref[...]
        hg = hg_ref[...]
        part = tm // ROPE_ROW_PARTS
        for p in range(ROPE_ROW_PARTS):
            rows = slice(p * part, (p + 1) * part)
            y = jnp.dot(hn_ref[rows, :], w, preferred_element_type=F32)
            for h in range(tn // HEAD_DIM):
                sl = slice(h * HEAD_DIM, (h + 1) * HEAD_DIM)
                o_ref[rows, sl] = _head_norm_rope(y[:, sl], hg, cos_ref[rows, :],
                                                  sin_ref[rows, :]).astype(o_ref.dtype)

    pl.when(j >= n_rope_tiles)(plain)


def rms_matmul(x, gain, w, layer, out_dtype, *, rope=None, seq_len=None, tm=1024, tn=512):
    n, d = x.shape
    e = w.shape[2]
    tm = min(tm, n)
    tn = min(tn, e)
    in_specs = [
        pl.BlockSpec((tm, d), lambda i, j: (i, 0)),
        pl.BlockSpec((1, d), lambda i, j: (0, 0)),
        pl.BlockSpec((None, d, tn), lambda i, j: (layer, 0, j)),
    ]
    args = [x, gain.reshape(1, d), w]
    n_rope_tiles = 0
    if rope is not None:
        hg, cos, sin, n_rope_cols = rope
        tm = min(tm, seq_len)
        tn = min(tn, n_rope_cols)
        in_specs[0] = pl.BlockSpec((tm, d), lambda i, j: (i, 0))
        in_specs[2] = pl.BlockSpec((None, d, tn), lambda i, j: (layer, 0, j))
        n_rope_tiles = n_rope_cols // tn
        pos_tiles = seq_len // tm
        in_specs += [
            pl.BlockSpec((1, HEAD_DIM), lambda i, j: (0, 0)),
            pl.BlockSpec((tm, HEAD_DIM), lambda i, j: (i % pos_tiles, 0)),
            pl.BlockSpec((tm, HEAD_DIM), lambda i, j: (i % pos_tiles, 0)),
        ]
        args += [hg.reshape(1, HEAD_DIM), cos, sin]
    return pl.pallas_call(
        functools.partial(_rms_matmul_kernel, tm=tm, tn=tn, n_rope_tiles=n_rope_tiles),
        grid=(n // tm, e // tn),
        in_specs=in_specs,
        out_specs=pl.BlockSpec((tm, tn), lambda i, j: (i, j)),
        out_shape=jax.ShapeDtypeStruct((n, e), out_dtype),
        scratch_shapes=[pltpu.VMEM((tm, d), BF16)],
        compiler_params=_params("arbitrary", "arbitrary"),
        name="rms_matmul",
    )(*args)


def _matmul_res_kernel(a_ref, w_ref, r_ref, o_ref):
    o_ref[...] = r_ref[...] + jnp.dot(a_ref[...], w_ref[...],
                                      preferred_element_type=F32)


def matmul_residual(a, w, layer, res, *, tm=1024, tn=512):
    n, d = a.shape
    e = w.shape[2]
    tm = min(tm, n)
    tn = min(tn, e)
    return pl.pallas_call(
        _matmul_res_kernel,
        grid=(n // tm, e // tn),
        in_specs=[
            pl.BlockSpec((tm, d), lambda i, j: (i, 0)),
            pl.BlockSpec((None, d, tn), lambda i, j: (layer, 0, j)),
            pl.BlockSpec((tm, tn), lambda i, j: (i, j)),
        ],
        out_specs=pl.BlockSpec((tm, tn), lambda i, j: (i, j)),
        out_shape=jax.ShapeDtypeStruct((n, e), F32),
        compiler_params=_params("arbitrary", "arbitrary"),
        name="matmul_residual",
    )(a, w, res)


def _hgrn_kernel(zq_ref, zf_ref, zi_ref, zg_ref, lb_ref, og_ref, o_ref,
                 st_ref, g_scr, h_scr, a_scr, u_scr, *, n_groups):
    C = HGRN_CHUNK
    SB = HGRN_SUB
    NB = C // SB

    @pl.when(pl.program_id(2) == 0)
    def _():
        st_ref[...] = jnp.zeros_like(st_ref)

    lb = lb_ref[...]
    og = og_ref[...]
    row_c = lax.broadcasted_iota(jnp.int32, (C, C), 0)
    col_c = lax.broadcasted_iota(jnp.int32, (C, C), 1)
    causal = row_c >= col_c
    tril = causal.astype(F32)
    lane_s = lax.broadcasted_iota(jnp.int32, (SB, C), 1)
    zero_blk = jnp.zeros((SB, HEAD_DIM), F32)
    halves = [h for h in (32, 16, 8) if SB <= h < C]
    level_mask = {half: ((row_c // (2 * half) == col_c // (2 * half))
                         & ((row_c // half) % 2 == 1) & ((col_c // half) % 2 == 0))
                  for half in halves}

    def blk(x, b):
        return x[b * SB:(b + 1) * SB, :]

    def gates(rows, u):
        f = lb + (1.0 - lb) * jax.nn.sigmoid(zf_ref[0, rows, :])
        G2 = jnp.dot(tril, jnp.log2(f), precision=HIGHEST, preferred_element_type=F32)
        g_scr[u] = G2
        h_scr[u] = G2 - jnp.log2(1.0 - f)

    def scores(rows, u):
        q = zq_ref[0, rows, :]
        G2 = g_scr[u]
        H2 = h_scr[u]
        A = jnp.zeros((C, C), F32)
        for half in halves:
            per_half = half // SB
            qa, kb = [], []
            for b in range(NB):
                grp = b // per_half
                if grp % 2 == 1:
                    r = grp * half - 1
                    qa.append(blk(q, b) * jnp.exp2(blk(G2, b) - g_scr[u, r:r + 1, :]))
                    kb.append(zero_blk)
                else:
                    r = (grp + 1) * half - 1
                    kb.append(jnp.exp2(g_scr[u, r:r + 1, :] - blk(H2, b)))
                    qa.append(zero_blk)
            sq = lax.dot_general(jnp.concatenate(qa, axis=0).astype(BF16),
                                 jnp.concatenate(kb, axis=0).astype(BF16), NT_DIMS,
                                 preferred_element_type=F32)
            A = A + jnp.where(level_mask[half], sq, 0.0)
        a_parts = []
        for b in range(NB):
            Gt = blk(G2, b)
            qt = blk(q, b)
            a_b = jnp.zeros((SB, C), F32)
            for s in range(b * SB, (b + 1) * SB):
                e = jnp.exp2(Gt - h_scr[u, s:s + 1, :])
                a_b = jnp.where(lane_s == s, jnp.sum(qt * e, axis=-1, keepdims=True), a_b)
            a_parts.append(a_b)
        A = jnp.where(causal, A + jnp.concatenate(a_parts, axis=0), 0.0)
        a_scr[u] = A.astype(BF16)

    def state_input(rows, u):
        G2_end = g_scr[u, C - 1:C, :]
        kd = jnp.exp2(G2_end - h_scr[u])
        u_scr[u] = jnp.dot(zi_ref[0, rows, :].T.astype(BF16), kd.astype(BF16),
                           preferred_element_type=F32)

    def output(rows, u):
        G2 = g_scr[u]
        st = st_ref[...]
        qg = (zq_ref[0, rows, :] * jnp.exp2(G2)).astype(BF16)
        o = (lax.dot_general(qg, st.astype(BF16), NT_DIMS, preferred_element_type=F32)
             + jnp.dot(a_scr[u], zi_ref[0, rows, :].astype(BF16), preferred_element_type=F32))
        st_ref[...] = jnp.exp2(G2[C - 1:C, :]) * st + u_scr[u]
        ms = jnp.mean(o * o, axis=-1, keepdims=True)
        zg = zg_ref[0, rows, :]
        o_ref[0, rows, :] = (o * lax.rsqrt(ms + RMS_EPS) * og * jax.nn.silu(zg)).astype(o_ref.dtype)

    def group(gi, carry):
        base = pl.multiple_of(gi * (HGRN_UNROLL * C), HGRN_UNROLL * C)
        for phase in (gates, scores, state_input, output):
            for u in range(HGRN_UNROLL):
                phase(pl.ds(base + u * C, C), u)
        return carry

    lax.fori_loop(0, n_groups, group, 0)


def hgrn2_scan(z, lb, o_gain, *, tt=4096):
    b, t, d4 = z.shape
    d = d4 // 4
    nh = d // HEAD_DIM
    tt = min(tt, t)

    def zspec(part):
        return pl.BlockSpec((1, tt, HEAD_DIM), lambda bi, h, ti: (bi, ti, part * nh + h))

    return pl.pallas_call(
        functools.partial(_hgrn_kernel, n_groups=tt // (HGRN_CHUNK * HGRN_UNROLL)),
        grid=(b, nh, t // tt),
        in_specs=[zspec(0), zspec(1), zspec(2), zspec(3),
                  pl.BlockSpec((1, HEAD_DIM), lambda bi, h, ti: (0, h)),
                  pl.BlockSpec((1, HEAD_DIM), lambda bi, h, ti: (0, 0))],
        out_specs=pl.BlockSpec((1, tt, HEAD_DIM), lambda bi, h, ti: (bi, ti, h)),
        out_shape=jax.ShapeDtypeStruct((b, t, d), BF16),
        scratch_shapes=[pltpu.VMEM((HEAD_DIM, HEAD_DIM), F32),
                        pltpu.VMEM((HGRN_UNROLL, HGRN_CHUNK, HEAD_DIM), F32),
                        pltpu.VMEM((HGRN_UNROLL, HGRN_CHUNK, HEAD_DIM), F32),
                        pltpu.VMEM((HGRN_UNROLL, HGRN_CHUNK, HGRN_CHUNK), BF16),
                        pltpu.VMEM((HGRN_UNROLL, HEAD_DIM, HEAD_DIM), F32)],
        compiler_params=_params("arbitrary", "arbitrary", "arbitrary"),
        name="hgrn2_scan",
    )(z, z, z, z, lb.reshape(1, d), o_gain.reshape(1, HEAD_DIM))


def _moba_kernel(q_ref, k_ref, v_ref, o_ref, km_ref, vt_ref, qt_ref, bias_ref, m_ref, l_ref, acc_ref,
                 *, n_blocks, heads):
    BLK = MOBA_BLOCK
    qi = pl.program_id(2)
    exp2_scale = HEAD_DIM ** -0.5 * 1.4426950408889634

    def head(h):
        return slice(h * HEAD_DIM, (h + 1) * HEAD_DIM)

    eye = (lax.broadcasted_iota(jnp.int32, (HEAD_DIM, HEAD_DIM), 0)
           == lax.broadcasted_iota(jnp.int32, (HEAD_DIM, HEAD_DIM), 1)).astype(BF16)

    @pl.when(qi == 0)
    def _():
        for h in range(heads):
            for j in range(n_blocks):
                rows = slice(j * BLK, (j + 1) * BLK)
                kb = k_ref[0, rows, head(h)].astype(F32)
                km_ref[h, j:j + 1, :] = jnp.mean(kb, axis=0, keepdims=True)
                vt_ref[h, :, rows] = lax.dot_general(eye, v_ref[0, rows, head(h)], NT_DIMS,
                                                     preferred_element_type=F32).astype(BF16)

    blk_id = lax.broadcasted_iota(jnp.int32, (n_blocks, BLK), 0)
    past = blk_id < qi
    key_i = lax.broadcasted_iota(jnp.int32, (BLK, BLK), 0)
    qry_i = lax.broadcasted_iota(jnp.int32, (BLK, BLK), 1)
    own = pl.ds(pl.multiple_of(qi * BLK, BLK), BLK)

    def scores(rows):
        return [jnp.dot(k_ref[0, rows, head(h)], qt_ref[h], preferred_element_type=F32)
                for h in range(heads)]

    for h in range(heads):
        qt_ref[h] = lax.dot_general(eye, q_ref[0, :, head(h)], NT_DIMS,
                                    preferred_element_type=F32).astype(BF16)
    for h in range(heads):
        gate = jnp.dot(km_ref[h], qt_ref[h].astype(F32), precision=HIGHEST,
                       preferred_element_type=F32)
        gm = jnp.where(past, gate, -jnp.inf)
        cnt = jnp.zeros((n_blocks, BLK), jnp.int32)
        for m in range(n_blocks):
            gmm = gm[m:m + 1, :]
            ahead = (gmm > gm) | ((gmm == gm) & (blk_id > m))
            cnt = cnt + ahead.astype(jnp.int32)
        bias_ref[h] = jnp.where(past & (cnt < MOBA_TOP_K), 0.0, -jnp.inf).astype(F32)

    probs = []
    for h, s in enumerate(scores(own)):
        s = jnp.where(key_i <= qry_i, s, -jnp.inf)
        m0 = jnp.max(s, axis=0, keepdims=True)
        p = jnp.exp2((s - m0) * exp2_scale)
        m_ref[h] = m0
        l_ref[h] = jnp.sum(p, axis=0, keepdims=True)
        probs.append(p.astype(BF16))
    for h in range(heads):
        acc_ref[h] = jnp.dot(vt_ref[h, :, own], probs[h], preferred_element_type=F32)

    def past_blocks(js):
        rows = [slice(j * BLK, (j + 1) * BLK) for j in js]
        s_all = [scores(r) for r in rows]
        for j, r, s_heads in zip(js, rows, s_all):
            probs, alphas = [], []
            for h, sj in enumerate(s_heads):
                sj = sj + bias_ref[h, j:j + 1, :]
                m_old = m_ref[h]
                m_new = jnp.maximum(m_old, jnp.max(sj, axis=0, keepdims=True))
                alpha = jnp.exp2((m_old - m_new) * exp2_scale)
                pj = jnp.exp2((sj - m_new) * exp2_scale)
                m_ref[h] = m_new
                l_ref[h] = alpha * l_ref[h] + jnp.sum(pj, axis=0, keepdims=True)
                probs.append(pj.astype(BF16))
                alphas.append(alpha)
            for h in range(heads):
                acc_ref[h] = alphas[h] * acc_ref[h] + jnp.dot(vt_ref[h, :, r], probs[h],
                                                              preferred_element_type=F32)

    for j in range(0, n_blocks - 1, 2):
        if j + 1 < n_blocks - 1:
            @pl.when(j + 1 < qi)
            def _(j=j):
                past_blocks((j, j + 1))

        @pl.when(qi == j + 1)
        def _(j=j):
            past_blocks((j,))

    for h in range(heads):
        o_ref[0, :, head(h)] = (acc_ref[h] / l_ref[h]).T.astype(o_ref.dtype)


def moba_attention(q, kv, *, heads=4):
    b, t, d = q.shape
    nh = d // HEAD_DIM
    heads = min(heads, nh)
    ng = nh // heads
    nb = t // MOBA_BLOCK
    w = heads * HEAD_DIM
    return pl.pallas_call(
        functools.partial(_moba_kernel, n_blocks=nb, heads=heads),
        grid=(b, ng, nb),
        in_specs=[pl.BlockSpec((1, MOBA_BLOCK, w), lambda bi, g, qi: (bi, qi, g)),
                  pl.BlockSpec((1, t, w), lambda bi, g, qi: (bi, 0, g)),
                  pl.BlockSpec((1, t, w), lambda bi, g, qi: (bi, 0, ng + g))],
        out_specs=pl.BlockSpec((1, MOBA_BLOCK, w), lambda bi, g, qi: (bi, qi, g)),
        out_shape=jax.ShapeDtypeStruct((b, t, d), BF16),
        scratch_shapes=[pltpu.VMEM((heads, nb, HEAD_DIM), F32),
                        pltpu.VMEM((heads, HEAD_DIM, t), BF16),
                        pltpu.VMEM((heads, HEAD_DIM, MOBA_BLOCK), BF16),
                        pltpu.VMEM((heads, nb, MOBA_BLOCK), F32),
                        pltpu.VMEM((heads, 1, MOBA_BLOCK), F32),
                        pltpu.VMEM((heads, 1, MOBA_BLOCK), F32),
                        pltpu.VMEM((heads, HEAD_DIM, MOBA_BLOCK), F32)],
        compiler_params=_params("arbitrary", "arbitrary", "arbitrary"),
        name="moba_attention",
    )(q, kv, kv)


def _router_kernel(x_ref, g_ref, wr_ref, br_ref, h_ref, ids_ref, wts_ref, *, tm):
    _rms_rows_to(x_ref, g_ref, h_ref, tm, 64)
    h = h_ref[...]
    hi = h.astype(BF16)
    lo = (h - hi.astype(F32)).astype(BF16)
    w = wr_ref[...]
    hw = jnp.dot(hi, w, preferred_element_type=F32)
    logits = (hw[:, :ROUTER_LANES] + hw[:, ROUTER_LANES:]
              + jnp.dot(lo, w[:, :ROUTER_LANES], preferred_element_type=F32)
              + br_ref[...])
    lane = lax.broadcasted_iota(jnp.int32, logits.shape, 1)
    neg = -jnp.inf
    big = ROUTER_LANES

    gl = jnp.where(lane < N_GROUPS, logits, neg)
    gmax = jnp.max(gl, axis=-1, keepdims=True)
    gsum = jnp.sum(jnp.exp(gl - gmax), axis=-1, keepdims=True)
    g_top_p = 1.0 / gsum
    g_top = jnp.min(jnp.where(gl == gmax, lane, big), axis=-1, keepdims=True)

    lo = N_GROUPS + EXPERTS_PER_GROUP * g_top
    el = jnp.where((lane >= lo) & (lane < lo + EXPERTS_PER_GROUP), logits, neg)
    emax = jnp.max(el, axis=-1, keepdims=True)
    esum = jnp.sum(jnp.exp(el - emax), axis=-1, keepdims=True)
    i1 = jnp.min(jnp.where(el == emax, lane, big), axis=-1, keepdims=True)
    el2 = jnp.where(lane == i1, neg, el)
    emax2 = jnp.max(el2, axis=-1, keepdims=True)
    i2 = jnp.min(jnp.where(el2 == emax2, lane, big), axis=-1, keepdims=True)
    p1 = 1.0 / esum
    p2 = jnp.exp(emax2 - emax) / esum
    psum = p1 + p2
    w1 = g_top_p * p1 / psum
    w2 = g_top_p * p2 / psum

    ids_ref[...] = jnp.where(lane == 0, i1 - N_GROUPS, jnp.where(lane == 1, i2 - N_GROUPS, 0))
    wts_ref[...] = jnp.where(lane == 0, w1, jnp.where(lane == 1, w2, 0.0))


def moe_router(x, gain, w_router, b_router, *, tm=512):
    n, d = x.shape
    tm = min(tm, n)
    return pl.pallas_call(
        functools.partial(_router_kernel, tm=tm),
        grid=(n // tm,),
        in_specs=[pl.BlockSpec((tm, d), lambda i: (i, 0)),
                  pl.BlockSpec((1, d), lambda i: (0, 0)),
                  pl.BlockSpec((d, 2 * ROUTER_LANES), lambda i: (0, 0)),
                  pl.BlockSpec((1, ROUTER_LANES), lambda i: (0, 0))],
        out_specs=[pl.BlockSpec((tm, d), lambda i: (i, 0)),
                   pl.BlockSpec((tm, ROUTER_LANES), lambda i: (i, 0)),
                   pl.BlockSpec((tm, ROUTER_LANES), lambda i: (i, 0))],
        out_shape=[jax.ShapeDtypeStruct((n, d), F32),
                   jax.ShapeDtypeStruct((n, ROUTER_LANES), jnp.int32),
                   jax.ShapeDtypeStruct((n, ROUTER_LANES), F32)],
        compiler_params=_params("arbitrary"),
        name="moe_router",
    )(x, gain.reshape(1, d), w_router, b_router)


def _expert_kernel(blk_e_ref, n_used_ref, tok0_ref, tok_ref, h_hbm, wg_ref, wu_ref, wd_ref, o_ref,
                   xbuf0, xbuf1, wgb, wub, wdb, sem_in, *, rows):
    i = pl.program_id(0)
    n_used = n_used_ref[0]
    xbufs = (xbuf0, xbuf1)

    def gather(idx_ref, slot):
        for r in range(rows):
            pltpu.make_async_copy(h_hbm.at[pl.ds(idx_ref[0, 0, r], 1), :],
                                  xbufs[slot].at[pl.ds(r, 1), :],
                                  sem_in.at[slot]).start(priority=r % DMA_THREADS)

    def wait_gather(slot):
        pltpu.make_async_copy(h_hbm.at[pl.ds(0, rows), :], xbufs[slot], sem_in.at[slot]).wait()

    @pl.when(i == 0)
    def _():
        gather(tok0_ref, 0)

    @pl.when((i == 0) | (blk_e_ref[i] != blk_e_ref[jnp.maximum(i - 1, 0)]))
    def _():
        wgb[...] = wg_ref[...].astype(BF16)
        wub[...] = wu_ref[...].astype(BF16)
        wdb[...] = wd_ref[...].astype(BF16)

    def step(slot):
        wait_gather(slot)
        gather(tok_ref, 1 - slot)
        x = xbufs[slot][...].astype(BF16)
        g = jnp.dot(x, wgb[...], preferred_element_type=F32)
        u = jnp.dot(x, wub[...], preferred_element_type=F32)
        mid = (jax.nn.silu(g) * u).astype(BF16)
        o_ref[...] = jnp.dot(mid, wdb[...], preferred_element_type=F32)

    for slot in range(2):
        @pl.when((i < n_used) & (i % 2 == slot))
        def _(slot=slot):
            step(slot)

    @pl.when(i == n_used - 1)
    def _():
        for slot in range(2):
            @pl.when(i % 2 == slot)
            def _(slot=slot):
                wait_gather(1 - slot)

    @pl.when(i >= n_used)
    def _():
        o_ref[...] = jnp.zeros_like(o_ref)


def expert_mlp(h, blk_e, n_used, tok, w_gate, w_up, w_down, layer):
    n, d = h.shape
    n_blk, r = tok.shape[0] - 1, tok.shape[2]
    ff = w_gate.shape[3]

    def smem_rows(index_map):
        return pl.BlockSpec((1, 1, r), index_map, memory_space=pltpu.SMEM)

    def weight(shape):
        return pl.BlockSpec((None, None) + shape, lambda i, be, nu: (layer, be[i], 0, 0))

    grid_spec = pltpu.PrefetchScalarGridSpec(
        num_scalar_prefetch=2,
        grid=(n_blk,),
        in_specs=[
            smem_rows(lambda i, be, nu: (0, 0, 0)),
            smem_rows(lambda i, be, nu: (i + 1, 0, 0)),
            pl.BlockSpec(memory_space=pl.ANY),
            weight((d, ff)), weight((d, ff)), weight((ff, d)),
        ],
        out_specs=pl.BlockSpec((r, d), lambda i, be, nu: (i, 0)),
        scratch_shapes=[
            pltpu.VMEM((r, d), F32), pltpu.VMEM((r, d), F32),
            pltpu.VMEM((d, ff), BF16),
            pltpu.VMEM((d, ff), BF16),
            pltpu.VMEM((ff, d), BF16),
            pltpu.SemaphoreType.DMA((2,)),
        ],
    )
    return pl.pallas_call(
        functools.partial(_expert_kernel, rows=r),
        grid_spec=grid_spec,
        out_shape=jax.ShapeDtypeStruct((n_blk * r, d), F32),
        compiler_params=_params("arbitrary"),
        name="expert_mlp",
    )(blk_e, n_used, tok, tok, h, w_gate, w_up, w_down)


def _combine_kernel(row0_ref, row_ref, x_ref, w_ref, y_hbm, o_ref, ybuf0, ybuf1, sem, *, tm, n_tiles):
    i = pl.program_id(0)
    ybufs = (ybuf0, ybuf1)

    def gather(idx_ref, slot):
        for r in range(TOP_K_IN_GROUP * tm):
            pltpu.make_async_copy(y_hbm.at[pl.ds(idx_ref[0, 0, r], 1), :],
                                  ybufs[slot].at[pl.ds(r, 1), :],
                                  sem.at[slot]).start(priority=r % DMA_THREADS)

    def wait_gather(slot):
        pltpu.make_async_copy(y_hbm.at[pl.ds(0, TOP_K_IN_GROUP * tm), :], ybufs[slot],
                              sem.at[slot]).wait()

    @pl.when(i == 0)
    def _():
        gather(row0_ref, 0)

    def step(slot):
        wait_gather(slot)
        gather(row_ref, 1 - slot)
        w = w_ref[...]
        o_ref[...] = x_ref[...] + (w[:, 0:1] * ybufs[slot][0:tm, :]
                                   + w[:, 1:2] * ybufs[slot][tm:2 * tm, :])

    for slot in range(2):
        @pl.when(i % 2 == slot)
        def _(slot=slot):
            step(slot)

    @pl.when(i == n_tiles - 1)
    def _():
        wait_gather(n_tiles % 2)


def moe_combine(x, y, rows, wts, *, tm=COMBINE_ROWS):
    n, d = x.shape
    nt = n // tm

    def smem_rows(index_map):
        return pl.BlockSpec((1, 1, TOP_K_IN_GROUP * tm), index_map, memory_space=pltpu.SMEM)

    return pl.pallas_call(
        functools.partial(_combine_kernel, tm=tm, n_tiles=nt),
        grid=(nt,),
        in_specs=[smem_rows(lambda i: (0, 0, 0)),
                  smem_rows(lambda i: (jnp.minimum(i + 1, nt - 1), 0, 0)),
                  pl.BlockSpec((tm, d), lambda i: (i, 0)),
                  pl.BlockSpec((tm, ROUTER_LANES), lambda i: (i, 0)),
                  pl.BlockSpec(memory_space=pl.ANY)],
        out_specs=pl.BlockSpec((tm, d), lambda i: (i, 0)),
        out_shape=jax.ShapeDtypeStruct((n, d), F32),
        scratch_shapes=[pltpu.VMEM((TOP_K_IN_GROUP * tm, d), F32),
                        pltpu.VMEM((TOP_K_IN_GROUP * tm, d), F32),
                        pltpu.SemaphoreType.DMA((2,))],
        compiler_params=_params("arbitrary"),
        name="moe_combine",
    )(rows, rows, x, wts, y)


def _expert_row_index(expert_id, rows_per_block, combine_rows):
    n, k = expert_id.shape
    a_total = n * k
    r = rows_per_block
    flat_e = expert_id.reshape(-1)
    order = jnp.argsort(flat_e).astype(jnp.int32)
    rank = jnp.argsort(order).astype(jnp.int32)
    experts = jnp.arange(N_EXPERTS, dtype=jnp.int32)
    counts = jnp.sum((flat_e[:, None] == experts[None, :]).astype(jnp.int32), axis=0)
    padded = (counts + r - 1) // r * r
    pad_end = jnp.cumsum(padded)
    pad_start = pad_end - padded
    start = jnp.cumsum(counts) - counts
    n_blk = (a_total + N_EXPERTS * (r - 1) + r - 1) // r
    n_used = (pad_end[-1] // r).astype(jnp.int32)
    blk = jnp.arange(n_blk, dtype=jnp.int32)
    blk_row0 = blk * r
    blk_e = jnp.minimum(jnp.sum((pad_end[None, :] <= blk_row0[:, None]).astype(jnp.int32), axis=1),
                        N_EXPERTS - 1).astype(jnp.int32)
    in_e = blk_row0 - pad_start[blk_e]
    nvalid = jnp.clip(counts[blk_e] - in_e, 0, r)
    lane = jnp.arange(r, dtype=jnp.int32)[None, :]
    valid = (lane < nvalid[:, None]) & (blk[:, None] < n_used)
    idx = (start[blk_e] + in_e)[:, None] + lane
    src = order[jnp.clip(idx, 0, a_total - 1)]
    tok = jnp.where(valid, src // k, 0)
    tok = jnp.concatenate([tok, jnp.zeros((1, r), jnp.int32)], axis=0)
    row = (pad_start[flat_e] + rank - start[flat_e]).astype(jnp.int32).reshape(n, k)
    row = row.reshape(n // combine_rows, combine_rows, k).transpose(0, 2, 1)
    return (blk_e, n_used.reshape(1), tok.reshape(n_blk + 1, 1, r),
            row.reshape(n // combine_rows, 1, k * combine_rows))


def hier_moe(x, gain, w_group, b_group, w_expert, b_expert, w_gate, w_up, w_down, layer):
    n, d = x.shape
    pad = ROUTER_LANES - N_GROUPS - N_EXPERTS
    w_router = jnp.concatenate([w_group, w_expert, jnp.zeros((d, pad), F32)], axis=1)
    w_hi = w_router.astype(BF16)
    w_lo = (w_router - w_hi.astype(F32)).astype(BF16)
    b_router = jnp.concatenate([b_group, b_expert, jnp.zeros((pad,), F32)]).reshape(1, ROUTER_LANES)
    h, ids, wts = moe_router(x, gain, jnp.concatenate([w_hi, w_lo], axis=1), b_router)
    tm = min(COMBINE_ROWS, n)
    blk_e, n_used, tok, rows = _expert_row_index(ids[:, :TOP_K_IN_GROUP], EXPERT_ROWS, tm)
    y = expert_mlp(h, blk_e, n_used, tok, w_gate, w_up, w_down, layer)
    return moe_combine(x, y, rows, wts, tm=tm)


def _rope_tables(t):
    inv_freq = ROPE_THETA ** (-jnp.arange(0, HEAD_DIM, 2, dtype=F32) / HEAD_DIM)
    ang = jnp.arange(t, dtype=F32)[:, None] * inv_freq[None, :]
    cos, sin = jnp.cos(ang), jnp.sin(ang)
    return jnp.concatenate([cos, cos], axis=1), jnp.concatenate([-sin, sin], axis=1)


def kernel(x, norm_mix, norm_ffn, a_w_in, a_lb_logits, a_o_gain, a_w_out, kv_norm, kv_w, k_norm,
           b_w_q, b_q_norm, b_w_out, moe_w_group, moe_b_group, moe_w_expert, moe_b_expert,
           moe_w_gate, moe_w_up, moe_w_down):
    b, t, d = x.shape
    n = b * t
    depth = norm_mix.shape[0]
    n_a = a_w_in.shape[0]
    lower_bounds = jnp.cumsum(jax.nn.softmax(a_lb_logits.astype(F32), axis=0), axis=0)
    cos, sin = _rope_tables(t)
    xf = x.reshape(n, d)
    kv = None
    for l in range(depth):
        if l < n_a:
            z = rms_matmul(xf, norm_mix[l], a_w_in.astype(BF16), l, F32)
            o = hgrn2_scan(z.reshape(b, t, 4 * d), lower_bounds[l], a_o_gain[l])
            xf = matmul_residual(o.reshape(n, d), a_w_out.astype(BF16), l, xf)
        else:
            if l == n_a:
                kv = rms_matmul(xf, kv_norm, kv_w[None].astype(BF16), 0, BF16,
                                rope=(k_norm, cos, sin, d), seq_len=t)
            lb_ = l - n_a
            q = rms_matmul(xf, norm_mix[l], b_w_q.astype(BF16), lb_, BF16,
                           rope=(b_q_norm[lb_], cos, sin, d), seq_len=t)
            o = moba_attention(q.reshape(b, t, d), kv.reshape(b, t, 2 * d))
            xf = matmul_residual(o.reshape(n, d), b_w_out.astype(BF16), lb_, xf)
        xf = hier_moe(xf, norm_ffn[l], moe_w_group[l], moe_b_group[l], moe_w_expert[l],
                      moe_b_expert[l], moe_w_gate, moe_w_up, moe_w_down, l)
    return xf.reshape(b, t, d)
```

```python
import functools

import jax
import jax.numpy as jnp
from jax import lax
from jax.experimental import pallas as pl
from jax.experimental.pallas import tpu as pltpu

HEAD_DIM = 128
HGRN_CHUNK = 64
HGRN_SUB = 8
HGRN_UNROLL = 8
MOBA_BLOCK = 256
MOBA_TOP_K = 3
ROPE_THETA = 10000.0
N_GROUPS = 4
EXPERTS_PER_GROUP = 8
N_EXPERTS = N_GROUPS * EXPERTS_PER_GROUP
TOP_K_IN_GROUP = 2
RMS_EPS = 1e-6
ROUTER_LANES = 128
EXPERT_ROWS = 256
COMBINE_ROWS = 256
DMA_THREADS = 2
ROW_DMA_PRIORITY = 1
ROPE_ROW_PARTS = 4

V7X_VMEM_LIMIT = 56 * 1024 * 1024

F32 = jnp.float32
BF16 = jnp.bfloat16
HIGHEST = lax.Precision.HIGHEST
NT_DIMS = (((1,), (1,)), ((), ()))


def _params(*sem):
    return pltpu.CompilerParams(dimension_semantics=sem, vmem_limit_bytes=V7X_VMEM_LIMIT)


def _rms_rows_to(x_ref, g_ref, out_ref, rows, chunk):
    gain = g_ref[...]

    def body(c, carry):
        sl = pl.ds(pl.multiple_of(c * chunk, chunk), chunk)
        x = x_ref[sl, :]
        ms = jnp.mean(x * x, axis=-1, keepdims=True)
        out_ref[sl, :] = (x * lax.rsqrt(ms + RMS_EPS) * gain).astype(out_ref.dtype)
        return carry

    lax.fori_loop(0, rows // chunk, body, 0)


def _head_norm_rope(y, hg, cos, sin):
    ms = jnp.mean(y * y, axis=-1, keepdims=True)
    yn = y * lax.rsqrt(ms + RMS_EPS) * hg
    return yn * cos + pltpu.roll(yn, HEAD_DIM // 2, axis=1) * sin


def _rms_matmul_kernel(*refs, tm, tn, n_rope_tiles):
    if n_rope_tiles:
        x_ref, g_ref, w_ref, hg_ref, cos_ref, sin_ref, o_ref, hn_ref = refs
    else:
        x_ref, g_ref, w_ref, o_ref, hn_ref = refs
    j = pl.program_id(1)

    @pl.when(j == 0)
    def _():
        _rms_rows_to(x_ref, g_ref, hn_ref, tm, 64)

    def plain():
        o_ref[...] = jnp.dot(hn_ref[...], w_ref[...],
                             preferred_element_type=F32).astype(o_ref.dtype)

    if not n_rope_tiles:
        plain()
        return

    @pl.when(j < n_rope_tiles)
    def _():
        w = w_ref[...]
        hg = hg_ref[...]
        part = tm // ROPE_ROW_PARTS
        for p in range(ROPE_ROW_PARTS):
            rows = slice(p * part, (p + 1) * part)
            y = jnp.dot(hn_ref[rows, :], w, preferred_element_type=F32)
            for h in range(tn // HEAD_DIM):
                sl = slice(h * HEAD_DIM, (h + 1) * HEAD_DIM)
                o_ref[rows, sl] = _head_norm_rope(y[:, sl], hg, cos_ref[rows, :],
                                                  sin_ref[rows, :]).astype(o_ref.dtype)

    pl.when(j >= n_rope_tiles)(plain)


def rms_matmul(x, gain, w, layer, out_dtype, *, rope=None, seq_len=None, tm=1024, tn=512):
    n, d = x.shape
    e = w.shape[2]
    tm = min(tm, n)
    tn = min(tn, e)
    in_specs = [
        pl.BlockSpec((tm, d), lambda i, j: (i, 0)),
        pl.BlockSpec((1, d), lambda i, j: (0, 0)),
        pl.BlockSpec((None, d, tn), lambda i, j: (layer, 0, j)),
    ]
    args = [x, gain.reshape(1, d), w]
    n_rope_tiles = 0
    if rope is not None:
        hg, cos, sin, n_rope_cols = rope
        tm = min(tm, seq_len)
        tn = min(tn, n_rope_cols)
        in_specs[0] = pl.BlockSpec((tm, d), lambda i, j: (i, 0))
        in_specs[2] = pl.BlockSpec((None, d, tn), lambda i, j: (layer, 0, j))
        n_rope_tiles = n_rope_cols // tn
        pos_tiles = seq_len // tm
        in_specs += [
            pl.BlockSpec((1, HEAD_DIM), lambda i, j: (0, 0)),
            pl.BlockSpec((tm, HEAD_DIM), lambda i, j: (i % pos_tiles, 0)),
            pl.BlockSpec((tm, HEAD_DIM), lambda i, j: (i % pos_tiles, 0)),
        ]
        args += [hg.reshape(1, HEAD_DIM), cos, sin]
    return pl.pallas_call(
        functools.partial(_rms_matmul_kernel, tm=tm, tn=tn, n_rope_tiles=n_rope_tiles),
        grid=(n // tm, e // tn),
        in_specs=in_specs,
        out_specs=pl.BlockSpec((tm, tn), lambda i, j: (i, j)),
        out_shape=jax.ShapeDtypeStruct((n, e), out_dtype),
        scratch_shapes=[pltpu.VMEM((tm, d), BF16)],
        compiler_params=_params("arbitrary", "arbitrary"),
        name="rms_matmul",
    )(*args)


def _matmul_res_kernel(a_ref, w_ref, r_ref, o_ref):
    o_ref[...] = r_ref[...] + jnp.dot(a_ref[...], w_ref[...],
                                      preferred_element_type=F32)


def matmul_residual(a, w, layer, res, *, tm=1024, tn=512):
    n, d = a.shape
    e = w.shape[2]
    tm = min(tm, n)
    tn = min(tn, e)
    return pl.pallas_call(
        _matmul_res_kernel,
        grid=(n // tm, e // tn),
        in_specs=[
            pl.BlockSpec((tm, d), lambda i, j: (i, 0)),
            pl.BlockSpec((None, d, tn), lambda i, j: (layer, 0, j)),
            pl.BlockSpec((tm, tn), lambda i, j: (i, j)),
        ],
        out_specs=pl.BlockSpec((tm, tn), lambda i, j: (i, j)),
        out_shape=jax.ShapeDtypeStruct((n, e), F32),
        compiler_params=_params("arbitrary", "arbitrary"),
        name="matmul_residual",
    )(a, w, res)


def _hgrn_kernel(zq_ref, zf_ref, zi_ref, zg_ref, lb_ref, og_ref, o_ref,
                 st_ref, g_scr, h_scr, a_scr, u_scr, *, n_groups):
    C = HGRN_CHUNK
    SB = HGRN_SUB
    NB = C // SB

    @pl.when(pl.program_id(2) == 0)
    def _():
        st_ref[...] = jnp.zeros_like(st_ref)

    lb = lb_ref[...]
    og = og_ref[...]
    row_c = lax.broadcasted_iota(jnp.int32, (C, C), 0)
    col_c = lax.broadcasted_iota(jnp.int32, (C, C), 1)
    causal = row_c >= col_c
    tril = causal.astype(F32)
    lane_s = lax.broadcasted_iota(jnp.int32, (SB, C), 1)
    zero_blk = jnp.zeros((SB, HEAD_DIM), F32)
    halves = [h for h in (32, 16, 8) if SB <= h < C]
    level_mask = {half: ((row_c // (2 * half) == col_c // (2 * half))
                         & ((row_c // half) % 2 == 1) & ((col_c // half) % 2 == 0))
                  for half in halves}

    def blk(x, b):
        return x[b * SB:(b + 1) * SB, :]

    def gates(rows, u):
        f = lb + (1.0 - lb) * jax.nn.sigmoid(zf_ref[0, rows, :])
        G2 = jnp.dot(tril, jnp.log2(f), precision=HIGHEST, preferred_element_type=F32)
        g_scr[u] = G2
        h_scr[u] = G2 - jnp.log2(1.0 - f)

    def scores(rows, u):
        q = zq_ref[0, rows, :]
        G2 = g_scr[u]
        H2 = h_scr[u]
        A = jnp.zeros((C, C), F32)
        for half in halves:
            per_half = half // SB
            qa, kb = [], []
            for b in range(NB):
                grp = b // per_half
                if grp % 2 == 1:
                    r = grp * half - 1
                    qa.append(blk(q, b) * jnp.exp2(blk(G2, b) - g_scr[u, r:r + 1, :]))
                    kb.append(zero_blk)
                else:
                    r = (grp + 1) * half - 1
                    kb.append(jnp.exp2(g_scr[u, r:r + 1, :] - blk(H2, b)))
                    qa.append(zero_blk)
            sq = lax.dot_general(jnp.concatenate(qa, axis=0).astype(BF16),
                                 jnp.concatenate(kb, axis=0).astype(BF16), NT_DIMS,
                                 preferred_element_type=F32)
            A = A + jnp.where(level_mask[half], sq, 0.0)
        a_parts = []
        for b in range(NB):
            Gt = blk(G2, b)
            qt = blk(q, b)
            a_b = jnp.zeros((SB, C), F32)
            for s in range(b * SB, (b + 1) * SB):
                e = jnp.exp2(Gt - h_scr[u, s:s + 1, :])
                a_b = jnp.where(lane_s == s, jnp.sum(qt * e, axis=-1, keepdims=True), a_b)
            a_parts.append(a_b)
        A = jnp.where(causal, A + jnp.concatenate(a_parts, axis=0), 0.0)
        a_scr[u] = A.astype(BF16)

    def state_input(rows, u):
        G2_end = g_scr[u, C - 1:C, :]
        kd = jnp.exp2(G2_end - h_scr[u])
        u_scr[u] = jnp.dot(zi_ref[0, rows, :].T.astype(BF16), kd.astype(BF16),
                           preferred_element_type=F32)

    def output(rows, u):
        G2 = g_scr[u]
        st = st_ref[...]
        qg = (zq_ref[0, rows, :] * jnp.exp2(G2)).astype(BF16)
        o = (lax.dot_general(qg, st.astype(BF16), NT_DIMS, preferred_element_type=F32)
             + jnp.dot(a_scr[u], zi_ref[0, rows, :].astype(BF16), preferred_element_type=F32))
        st_ref[...] = jnp.exp2(G2[C - 1:C, :]) * st + u_scr[u]
        ms = jnp.mean(o * o, axis=-1, keepdims=True)
        zg = zg_ref[0, rows, :]
        o_ref[0, rows, :] = (o * lax.rsqrt(ms + RMS_EPS) * og * jax.nn.silu(zg)).astype(o_ref.dtype)

    def group(gi, carry):
        base = pl.multiple_of(gi * (HGRN_UNROLL * C), HGRN_UNROLL * C)
        for phase in (gates, scores, state_input, output):
            for u in range(HGRN_UNROLL):
                phase(pl.ds(base + u * C, C), u)
        return carry

    lax.fori_loop(0, n_groups, group, 0)


def hgrn2_scan(z, lb, o_gain, *, tt=4096):
    b, t, d4 = z.shape
    d = d4 // 4
    nh = d // HEAD_DIM
    tt = min(tt, t)

    def zspec(part):
        return pl.BlockSpec((1, tt, HEAD_DIM), lambda bi, h, ti: (bi, ti, part * nh + h))

    return pl.pallas_call(
        functools.partial(_hgrn_kernel, n_groups=tt // (HGRN_CHUNK * HGRN_UNROLL)),
        grid=(b, nh, t // tt),
        in_specs=[zspec(0), zspec(1), zspec(2), zspec(3),
                  pl.BlockSpec((1, HEAD_DIM), lambda bi, h, ti: (0, h)),
                  pl.BlockSpec((1, HEAD_DIM), lambda bi, h, ti: (0, 0))],
        out_specs=pl.BlockSpec((1, tt, HEAD_DIM), lambda bi, h, ti: (bi, ti, h)),
        out_shape=jax.ShapeDtypeStruct((b, t, d), BF16),
        scratch_shapes=[pltpu.VMEM((HEAD_DIM, HEAD_DIM), F32),
                        pltpu.VMEM((HGRN_UNROLL, HGRN_CHUNK, HEAD_DIM), F32),
                        pltpu.VMEM((HGRN_UNROLL, HGRN_CHUNK, HEAD_DIM), F32),
                        pltpu.VMEM((HGRN_UNROLL, HGRN_CHUNK, HGRN_CHUNK), BF16),
                        pltpu.VMEM((HGRN_UNROLL, HEAD_DIM, HEAD_DIM), F32)],
        compiler_params=_params("arbitrary", "arbitrary", "arbitrary"),
        name="hgrn2_scan",
    )(z, z, z, z, lb.reshape(1, d), o_gain.reshape(1, HEAD_DIM))


def _moba_kernel(q_ref, k_ref, v_ref, o_ref, km_ref, vt_ref, qt_ref, bias_ref, m_ref, l_ref, acc_ref,
                 *, n_blocks, heads):
    BLK = MOBA_BLOCK
    qi = pl.program_id(2)
    exp2_scale = HEAD_DIM ** -0.5 * 1.4426950408889634

    def head(h):
        return slice(h * HEAD_DIM, (h + 1) * HEAD_DIM)

    eye = (lax.broadcasted_iota(jnp.int32, (HEAD_DIM, HEAD_DIM), 0)
           == lax.broadcasted_iota(jnp.int32, (HEAD_DIM, HEAD_DIM), 1)).astype(BF16)

    @pl.when(qi == 0)
    def _():
        for h in range(heads):
            for j in range(n_blocks):
                rows = slice(j * BLK, (j + 1) * BLK)
                kb = k_ref[0, rows, head(h)].astype(F32)
                km_ref[h, j:j + 1, :] = jnp.mean(kb, axis=0, keepdims=True)
                vt_ref[h, :, rows] = lax.dot_general(eye, v_ref[0, rows, head(h)], NT_DIMS,
                                                     preferred_element_type=F32).astype(BF16)

    blk_id = lax.broadcasted_iota(jnp.int32, (n_blocks, BLK), 0)
    past = blk_id < qi
    key_i = lax.broadcasted_iota(jnp.int32, (BLK, BLK), 0)
    qry_i = lax.broadcasted_iota(jnp.int32, (BLK, BLK), 1)
    own = pl.ds(pl.multiple_of(qi * BLK, BLK), BLK)

    def scores(rows):
        return [jnp.dot(k_ref[0, rows, head(h)], qt_ref[h], preferred_element_type=F32)
                for h in range(heads)]

    for h in range(heads):
        qt_ref[h] = lax.dot_general(eye, q_ref[0, :, head(h)], NT_DIMS,
                                    preferred_element_type=F32).astype(BF16)
    for h in range(heads):
        gate = jnp.dot(km_ref[h], qt_ref[h].astype(F32), precision=HIGHEST,
                       preferred_element_type=F32)
        gm = jnp.where(past, gate, -jnp.inf)
        cnt = jnp.zeros((n_blocks, BLK), jnp.int32)
        for m in range(n_blocks):
            gmm = gm[m:m + 1, :]
            ahead = (gmm > gm) | ((gmm == gm) & (blk_id > m))
            cnt = cnt + ahead.astype(jnp.int32)
        bias_ref[h] = jnp.where(past & (cnt < MOBA_TOP_K), 0.0, -jnp.inf).astype(F32)

    probs = []
    for h, s in enumerate(scores(own)):
        s = jnp.where(key_i <= qry_i, s, -jnp.inf)
        m0 = jnp.max(s, axis=0, keepdims=True)
        p = jnp.exp2((s - m0) * exp2_scale)
        m_ref[h] = m0
        l_ref[h] = jnp.sum(p, axis=0, keepdims=True)
        probs.append(p.astype(BF16))
    for h in range(heads):
        acc_ref[h] = jnp.dot(vt_ref[h, :, own], probs[h], preferred_element_type=F32)

    def past_blocks(js):
        rows = [slice(j * BLK, (j + 1) * BLK) for j in js]
        s_all = [scores(r) for r in rows]
        for j, r, s_heads in zip(js, rows, s_all):
            probs, alphas = [], []
            for h, sj in enumerate(s_heads):
                sj = sj + bias_ref[h, j:j + 1, :]
                m_old = m_ref[h]
                m_new = jnp.maximum(m_old, jnp.max(sj, axis=0, keepdims=True))
                alpha = jnp.exp2((m_old - m_new) * exp2_scale)
                pj = jnp.exp2((sj - m_new) * exp2_scale)
                m_ref[h] = m_new
                l_ref[h] = alpha * l_ref[h] + jnp.sum(pj, axis=0, keepdims=True)
                probs.append(pj.astype(BF16))
                alphas.append(alpha)
            for h in range(heads):
                acc_ref[h] = alphas[h] * acc_ref[h] + jnp.dot(vt_ref[h, :, r], probs[h],
                                                              preferred_element_type=F32)

    for j in range(0, n_blocks - 1, 2):
        if j + 1 < n_blocks - 1:
            @pl.when(j + 1 < qi)
            def _(j=j):
                past_blocks((j, j + 1))

        @pl.when(qi == j + 1)
        def _(j=j):
            past_blocks((j,))

    for h in range(heads):
        o_ref[0, :, head(h)] = (acc_ref[h] / l_ref[h]).T.astype(o_ref.dtype)


def moba_attention(q, kv, *, heads=4):
    b, t, d = q.shape
    nh = d // HEAD_DIM
    heads = min(heads, nh)
    ng = nh // heads
    nb = t // MOBA_BLOCK
    w = heads * HEAD_DIM
    return pl.pallas_call(
        functools.partial(_moba_kernel, n_blocks=nb, heads=heads),
        grid=(b, ng, nb),
        in_specs=[pl.BlockSpec((1, MOBA_BLOCK, w), lambda bi, g, qi: (bi, qi, g)),
                  pl.BlockSpec((1, t, w), lambda bi, g, qi: (bi, 0, g)),
                  pl.BlockSpec((1, t, w), lambda bi, g, qi: (bi, 0, ng + g))],
        out_specs=pl.BlockSpec((1, MOBA_BLOCK, w), lambda bi, g, qi: (bi, qi, g)),
        out_shape=jax.ShapeDtypeStruct((b, t, d), BF16),
        scratch_shapes=[pltpu.VMEM((heads, nb, HEAD_DIM), F32),
                        pltpu.VMEM((heads, HEAD_DIM, t), BF16),
                        pltpu.VMEM((heads, HEAD_DIM, MOBA_BLOCK), BF16),
                        pltpu.VMEM((heads, nb, MOBA_BLOCK), F32),
                        pltpu.VMEM((heads, 1, MOBA_BLOCK), F32),
                        pltpu.VMEM((heads, 1, MOBA_BLOCK), F32),
                        pltpu.VMEM((heads, HEAD_DIM, MOBA_BLOCK), F32)],
        compiler_params=_params("arbitrary", "arbitrary", "arbitrary"),
        name="moba_attention",
    )(q, kv, kv)


def _router_kernel(x_ref, g_ref, wr_ref, br_ref, h_ref, ids_ref, wts_ref, *, tm):
    _rms_rows_to(x_ref, g_ref, h_ref, tm, 64)
    h = h_ref[...]
    hi = h.astype(BF16)
    lo = (h - hi.astype(F32)).astype(BF16)
    w = wr_ref[...]
    hw = jnp.dot(hi, w, preferred_element_type=F32)
    logits = (hw[:, :ROUTER_LANES] + hw[:, ROUTER_LANES:]
              + jnp.dot(lo, w[:, :ROUTER_LANES], preferred_element_type=F32)
              + br_ref[...])
    lane = lax.broadcasted_iota(jnp.int32, logits.shape, 1)
    neg = -jnp.inf
    big = ROUTER_LANES

    gl = jnp.where(lane < N_GROUPS, logits, neg)
    gmax = jnp.max(gl, axis=-1, keepdims=True)
    gsum = jnp.sum(jnp.exp(gl - gmax), axis=-1, keepdims=True)
    g_top_p = 1.0 / gsum
    g_top = jnp.min(jnp.where(gl == gmax, lane, big), axis=-1, keepdims=True)

    lo = N_GROUPS + EXPERTS_PER_GROUP * g_top
    el = jnp.where((lane >= lo) & (lane < lo + EXPERTS_PER_GROUP), logits, neg)
    emax = jnp.max(el, axis=-1, keepdims=True)
    esum = jnp.sum(jnp.exp(el - emax), axis=-1, keepdims=True)
    i1 = jnp.min(jnp.where(el == emax, lane, big), axis=-1, keepdims=True)
    el2 = jnp.where(lane == i1, neg, el)
    emax2 = jnp.max(el2, axis=-1, keepdims=True)
    i2 = jnp.min(jnp.where(el2 == emax2, lane, big), axis=-1, keepdims=True)
    p1 = 1.0 / esum
    p2 = jnp.exp(emax2 - emax) / esum
    psum = p1 + p2
    w1 = g_top_p * p1 / psum
    w2 = g_top_p * p2 / psum

    ids_ref[...] = jnp.where(lane == 0, i1 - N_GROUPS, jnp.where(lane == 1, i2 - N_GROUPS, 0))
    wts_ref[...] = jnp.where(lane == 0, w1, jnp.where(lane == 1, w2, 0.0))


def moe_router(x, gain, w_router, b_router, *, tm=512):
    n, d = x.shape
    tm = min(tm, n)
    return pl.pallas_call(
        functools.partial(_router_kernel, tm=tm),
        grid=(n // tm,),
        in_specs=[pl.BlockSpec((tm, d), lambda i: (i, 0)),
                  pl.BlockSpec((1, d), lambda i: (0, 0)),
                  pl.BlockSpec((d, 2 * ROUTER_LANES), lambda i: (0, 0)),
                  pl.BlockSpec((1, ROUTER_LANES), lambda i: (0, 0))],
        out_specs=[pl.BlockSpec((tm, d), lambda i: (i, 0)),
                   pl.BlockSpec((tm, ROUTER_LANES), lambda i: (i, 0)),
                   pl.BlockSpec((tm, ROUTER_LANES), lambda i: (i, 0))],
        out_shape=[jax.ShapeDtypeStruct((n, d), F32),
                   jax.ShapeDtypeStruct((n, ROUTER_LANES), jnp.int32),
                   jax.ShapeDtypeStruct((n, ROUTER_LANES), F32)],
        compiler_params=_params("arbitrary"),
        name="moe_router",
    )(x, gain.reshape(1, d), w_router, b_router)


def _expert_kernel(blk_e_ref, n_used_ref, tok0_ref, tok1_ref, tok_ref, h_hbm, wg_ref, wu_ref, wd_ref,
                   o_ref, xbuf0, xbuf1, xbuf2, wgb, wub, wdb, sem_in, *, rows):
    i = pl.program_id(0)
    n_used = n_used_ref[0]
    xbufs = (xbuf0, xbuf1, xbuf2)
    n_buf = len(xbufs)

    def gather(idx_ref, slot):
        for r in range(rows):
            pltpu.make_async_copy(h_hbm.at[pl.ds(idx_ref[0, 0, r], 1), :],
                                  xbufs[slot].at[pl.ds(r, 1), :],
                                  sem_in.at[slot]).start(priority=ROW_DMA_PRIORITY)

    def wait_gather(slot):
        pltpu.make_async_copy(h_hbm.at[pl.ds(0, rows), :], xbufs[slot], sem_in.at[slot]).wait()

    @pl.when(i == 0)
    def _():
        gather(tok0_ref, 0)
        gather(tok1_ref, 1)

    @pl.when((i == 0) | (blk_e_ref[i] != blk_e_ref[jnp.maximum(i - 1, 0)]))
    def _():
        wgb[...] = wg_ref[...].astype(BF16)
        wub[...] = wu_ref[...].astype(BF16)
        wdb[...] = wd_ref[...].astype(BF16)

    def step(slot):
        wait_gather(slot)
        gather(tok_ref, (slot + 2) % n_buf)
        x = xbufs[slot][...].astype(BF16)
        g = jnp.dot(x, wgb[...], preferred_element_type=F32)
        u = jnp.dot(x, wub[...], preferred_element_type=F32)
        mid = (jax.nn.silu(g) * u).astype(BF16)
        o_ref[...] = jnp.dot(mid, wdb[...], preferred_element_type=F32)

    for slot in range(n_buf):
        @pl.when((i < n_used) & (i % n_buf == slot))
        def _(slot=slot):
            step(slot)

    @pl.when(i == n_used - 1)
    def _():
        for slot in range(n_buf):
            @pl.when(i % n_buf == slot)
            def _(slot=slot):
                wait_gather((slot + 1) % n_buf)
                wait_gather((slot + 2) % n_buf)

    @pl.when(i >= n_used)
    def _():
        o_ref[...] = jnp.zeros_like(o_ref)


def expert_mlp(h, blk_e, n_used, tok, w_gate, w_up, w_down, layer):
    n, d = h.shape
    n_blk, r = tok.shape[0] - 2, tok.shape[2]
    ff = w_gate.shape[3]

    def smem_rows(index_map):
        return pl.BlockSpec((1, 1, r), index_map, memory_space=pltpu.SMEM)

    def weight(shape):
        return pl.BlockSpec((None, None) + shape, lambda i, be, nu: (layer, be[i], 0, 0))

    grid_spec = pltpu.PrefetchScalarGridSpec(
        num_scalar_prefetch=2,
        grid=(n_blk,),
        in_specs=[
            smem_rows(lambda i, be, nu: (0, 0, 0)),
            smem_rows(lambda i, be, nu: (1, 0, 0)),
            smem_rows(lambda i, be, nu: (i + 2, 0, 0)),
            pl.BlockSpec(memory_space=pl.ANY),
            weight((d, ff)), weight((d, ff)), weight((ff, d)),
        ],
        out_specs=pl.BlockSpec((r, d), lambda i, be, nu: (i, 0)),
        scratch_shapes=[
            pltpu.VMEM((r, d), F32), pltpu.VMEM((r, d), F32), pltpu.VMEM((r, d), F32),
            pltpu.VMEM((d, ff), BF16),
            pltpu.VMEM((d, ff), BF16),
            pltpu.VMEM((ff, d), BF16),
            pltpu.SemaphoreType.DMA((3,)),
        ],
    )
    return pl.pallas_call(
        functools.partial(_expert_kernel, rows=r),
        grid_spec=grid_spec,
        out_shape=jax.ShapeDtypeStruct((n_blk * r, d), F32),
        compiler_params=_params("arbitrary"),
        name="expert_mlp",
    )(blk_e, n_used, tok, tok, tok, h, w_gate, w_up, w_down)


def _combine_kernel(row0_ref, row_ref, x_ref, w_ref, y_hbm, o_ref, ybuf0, ybuf1, sem, *, tm, n_tiles):
    i = pl.program_id(0)
    ybufs = (ybuf0, ybuf1)

    def gather(idx_ref, slot):
        for r in range(TOP_K_IN_GROUP * tm):
            pltpu.make_async_copy(y_hbm.at[pl.ds(idx_ref[0, 0, r], 1), :],
                                  ybufs[slot].at[pl.ds(r, 1), :],
                                  sem.at[slot]).start(priority=r % DMA_THREADS)

    def wait_gather(slot):
        pltpu.make_async_copy(y_hbm.at[pl.ds(0, TOP_K_IN_GROUP * tm), :], ybufs[slot],
                              sem.at[slot]).wait()

    @pl.when(i == 0)
    def _():
        gather(row0_ref, 0)

    def step(slot):
        wait_gather(slot)
        gather(row_ref, 1 - slot)
        w = w_ref[...]
        o_ref[...] = x_ref[...] + (w[:, 0:1] * ybufs[slot][0:tm, :]
                                   + w[:, 1:2] * ybufs[slot][tm:2 * tm, :])

    for slot in range(2):
        @pl.when(i % 2 == slot)
        def _(slot=slot):
            step(slot)

    @pl.when(i == n_tiles - 1)
    def _():
        wait_gather(n_tiles % 2)


def moe_combine(x, y, rows, wts, *, tm=COMBINE_ROWS):
    n, d = x.shape
    nt = n // tm

    def smem_rows(index_map):
        return pl.BlockSpec((1, 1, TOP_K_IN_GROUP * tm), index_map, memory_space=pltpu.SMEM)

    return pl.pallas_call(
        functools.partial(_combine_kernel, tm=tm, n_tiles=nt),
        grid=(nt,),
        in_specs=[smem_rows(lambda i: (0, 0, 0)),
                  smem_rows(lambda i: (jnp.minimum(i + 1, nt - 1), 0, 0)),
                  pl.BlockSpec((tm, d), lambda i: (i, 0)),
                  pl.BlockSpec((tm, ROUTER_LANES), lambda i: (i, 0)),
                  pl.BlockSpec(memory_space=pl.ANY)],
        out_specs=pl.BlockSpec((tm, d), lambda i: (i, 0)),
        out_shape=jax.ShapeDtypeStruct((n, d), F32),
        scratch_shapes=[pltpu.VMEM((TOP_K_IN_GROUP * tm, d), F32),
                        pltpu.VMEM((TOP_K_IN_GROUP * tm, d), F32),
                        pltpu.SemaphoreType.DMA((2,))],
        compiler_params=_params("arbitrary"),
        name="moe_combine",
    )(rows, rows, x, wts, y)


def _expert_row_index(expert_id, rows_per_block, combine_rows):
    n, k = expert_id.shape
    a_total = n * k
    r = rows_per_block
    flat_e = expert_id.reshape(-1)
    order = jnp.argsort(flat_e).astype(jnp.int32)
    rank = jnp.argsort(order).astype(jnp.int32)
    experts = jnp.arange(N_EXPERTS, dtype=jnp.int32)
    counts = jnp.sum((flat_e[:, None] == experts[None, :]).astype(jnp.int32), axis=0)
    padded = (counts + r - 1) // r * r
    pad_end = jnp.cumsum(padded)
    pad_start = pad_end - padded
    start = jnp.cumsum(counts) - counts
    n_blk = (a_total + N_EXPERTS * (r - 1) + r - 1) // r
    n_used = (pad_end[-1] // r).astype(jnp.int32)
    blk = jnp.arange(n_blk, dtype=jnp.int32)
    blk_row0 = blk * r
    blk_e = jnp.minimum(jnp.sum((pad_end[None, :] <= blk_row0[:, None]).astype(jnp.int32), axis=1),
                        N_EXPERTS - 1).astype(jnp.int32)
    in_e = blk_row0 - pad_start[blk_e]
    nvalid = jnp.clip(counts[blk_e] - in_e, 0, r)
    lane = jnp.arange(r, dtype=jnp.int32)[None, :]
    valid = (lane < nvalid[:, None]) & (blk[:, None] < n_used)
    idx = (start[blk_e] + in_e)[:, None] + lane
    src = order[jnp.clip(idx, 0, a_total - 1)]
    tok = jnp.where(valid, src // k, 0)
    tok = jnp.concatenate([tok, jnp.zeros((2, r), jnp.int32)], axis=0)
    shift = jnp.sum(jnp.where(flat_e[:, None] == experts[None, :], (pad_start - start)[None, :], 0),
                    axis=1)
    row = (rank + shift).astype(jnp.int32).reshape(n, k)
    row = row.reshape(n // combine_rows, combine_rows, k).transpose(0, 2, 1)
    return (blk_e, n_used.reshape(1), tok.reshape(n_blk + 2, 1, r),
            row.reshape(n // combine_rows, 1, k * combine_rows))


def hier_moe(x, gain, w_group, b_group, w_expert, b_expert, w_gate, w_up, w_down, layer):
    n, d = x.shape
    pad = ROUTER_LANES - N_GROUPS - N_EXPERTS
    w_router = jnp.concatenate([w_group, w_expert, jnp.zeros((d, pad), F32)], axis=1)
    w_hi = w_router.astype(BF16)
    w_lo = (w_router - w_hi.astype(F32)).astype(BF16)
    b_router = jnp.concatenate([b_group, b_expert, jnp.zeros((pad,), F32)]).reshape(1, ROUTER_LANES)
    h, ids, wts = moe_router(x, gain, jnp.concatenate([w_hi, w_lo], axis=1), b_router)
    tm = min(COMBINE_ROWS, n)
    blk_e, n_used, tok, rows = _expert_row_index(ids[:, :TOP_K_IN_GROUP], EXPERT_ROWS, tm)
    y = expert_mlp(h, blk_e, n_used, tok, w_gate, w_up, w_down, layer)
    return moe_combine(x, y, rows, wts, tm=tm)


def _rope_tables(t):
    inv_freq = ROPE_THETA ** (-jnp.arange(0, HEAD_DIM, 2, dtype=F32) / HEAD_DIM)
    ang = jnp.arange(t, dtype=F32)[:, None] * inv_freq[None, :]
    cos, sin = jnp.cos(ang), jnp.sin(ang)
    return jnp.concatenate([cos, cos], axis=1), jnp.concatenate([-sin, sin], axis=1)


def kernel(x, norm_mix, norm_ffn, a_w_in, a_lb_logits, a_o_gain, a_w_out, kv_norm, kv_w, k_norm,
           b_w_q, b_q_norm, b_w_out, moe_w_group, moe_b_group, moe_w_expert, moe_b_expert,
           moe_w_gate, moe_w_up, moe_w_down):
    b, t, d = x.shape
    n = b * t
    depth = norm_mix.shape[0]
    n_a = a_w_in.shape[0]
    lower_bounds = jnp.cumsum(jax.nn.softmax(a_lb_logits.astype(F32), axis=0), axis=0)
    cos, sin = _rope_tables(t)
    xf = x.reshape(n, d)
    kv = None
    for l in range(depth):
        if l < n_a:
            z = rms_matmul(xf, norm_mix[l], a_w_in.astype(BF16), l, F32)
            o = hgrn2_scan(z.reshape(b, t, 4 * d), lower_bounds[l], a_o_gain[l])
            xf = matmul_residual(o.reshape(n, d), a_w_out.astype(BF16), l, xf)
        else:
            if l == n_a:
                kv = rms_matmul(xf, kv_norm, kv_w[None].astype(BF16), 0, BF16,
                                rope=(k_norm, cos, sin, d), seq_len=t)
            lb_ = l - n_a
            q = rms_matmul(xf, norm_mix[l], b_w_q.astype(BF16), lb_, BF16,
                           rope=(b_q_norm[lb_], cos, sin, d), seq_len=t)
            o = moba_attention(q.reshape(b, t, d), kv.reshape(b, t, 2 * d))
            xf = matmul_residual(o.reshape(n, d), b_w_out.astype(BF16), lb_, xf)
        xf = hier_moe(xf, norm_ffn[l], moe_w_group[l], moe_b_group[l], moe_w_expert[l],
                      moe_b_expert[l], moe_w_gate, moe_w_up, moe_w_down, l)
    return xf.reshape(b, t, d)
```

```python
import functools

import jax
import jax.numpy as jnp
from jax import lax
from jax.experimental import pallas as pl
from jax.experimental.pallas import tpu as pltpu

HEAD_DIM = 128
HGRN_CHUNK = 64
HGRN_SUB = 8
HGRN_UNROLL = 8
MOBA_BLOCK = 256
MOBA_TOP_K = 3
ROPE_THETA = 10000.0
N_GROUPS = 4
EXPERTS_PER_GROUP = 8
N_EXPERTS = N_GROUPS * EXPERTS_PER_GROUP
TOP_K_IN_GROUP = 2
RMS_EPS = 1e-6
ROUTER_LANES = 128
EXPERT_ROWS = 256
COMBINE_ROWS = 256
DMA_THREADS = 2
ROW_DMA_PRIORITY = 1
ROPE_ROW_PARTS = 4

V7X_VMEM_LIMIT = 56 * 1024 * 1024

F32 = jnp.float32
BF16 = jnp.bfloat16
HIGHEST = lax.Precision.HIGHEST
NT_DIMS = (((1,), (1,)), ((), ()))


def _params(*sem):
    return pltpu.CompilerParams(dimension_semantics=sem, vmem_limit_bytes=V7X_VMEM_LIMIT)


def _rms_rows_to(x_ref, g_ref, out_ref, rows, chunk):
    gain = g_ref[...]

    def body(c, carry):
        sl = pl.ds(pl.multiple_of(c * chunk, chunk), chunk)
        x = x_ref[sl, :]
        ms = jnp.mean(x * x, axis=-1, keepdims=True)
        out_ref[sl, :] = (x * lax.rsqrt(ms + RMS_EPS) * gain).astype(out_ref.dtype)
        return carry

    lax.fori_loop(0, rows // chunk, body, 0)


def _head_norm_rope(y, hg, cos, sin):
    ms = jnp.mean(y * y, axis=-1, keepdims=True)
    yn = y * lax.rsqrt(ms + RMS_EPS) * hg
    return yn * cos + pltpu.roll(yn, HEAD_DIM // 2, axis=1) * sin


def _rms_matmul_kernel(*refs, tm, tn, n_rope_tiles):
    if n_rope_tiles:
        x_ref, g_ref, w_ref, hg_ref, cos_ref, sin_ref, o_ref, hn_ref = refs
    else:
        x_ref, g_ref, w_ref, o_ref, hn_ref = refs
    j = pl.program_id(1)

    @pl.when(j == 0)
    def _():
        _rms_rows_to(x_ref, g_ref, hn_ref, tm, 64)

    def plain():
        o_ref[...] = jnp.dot(hn_ref[...], w_ref[...],
                             preferred_element_type=F32).astype(o_ref.dtype)

    if not n_rope_tiles:
        plain()
        return

    @pl.when(j < n_rope_tiles)
    def _():
        w = w_ref[...]
        hg = hg_ref[...]
        part = tm // ROPE_ROW_PARTS
        for p in range(ROPE_ROW_PARTS):
            rows = slice(p * part, (p + 1) * part)
            y = jnp.dot(hn_ref[rows, :], w, preferred_element_type=F32)
            for h in range(tn // HEAD_DIM):
                sl = slice(h * HEAD_DIM, (h + 1) * HEAD_DIM)
                o_ref[rows, sl] = _head_norm_rope(y[:, sl], hg, cos_ref[rows, :],
                                                  sin_ref[rows, :]).astype(o_ref.dtype)

    pl.when(j >= n_rope_tiles)(plain)


def rms_matmul(x, gain, w, layer, out_dtype, *, rope=None, seq_len=None, tm=1024, tn=1024):
    n, d = x.shape
    e = w.shape[2]
    tm = min(tm, n)
    tn = min(tn, e)
    in_specs = [
        pl.BlockSpec((tm, d), lambda i, j: (i, 0)),
        pl.BlockSpec((1, d), lambda i, j: (0, 0)),
        pl.BlockSpec((None, d, tn), lambda i, j: (layer, 0, j)),
    ]
    args = [x, gain.reshape(1, d), w]
    n_rope_tiles = 0
    if rope is not None:
        hg, cos, sin, n_rope_cols = rope
        tm = min(tm, seq_len)
        tn = min(tn, n_rope_cols)
        in_specs[0] = pl.BlockSpec((tm, d), lambda i, j: (i, 0))
        in_specs[2] = pl.BlockSpec((None, d, tn), lambda i, j: (layer, 0, j))
        n_rope_tiles = n_rope_cols // tn
        pos_tiles = seq_len // tm
        in_specs += [
            pl.BlockSpec((1, HEAD_DIM), lambda i, j: (0, 0)),
            pl.BlockSpec((tm, HEAD_DIM), lambda i, j: (i % pos_tiles, 0)),
            pl.BlockSpec((tm, HEAD_DIM), lambda i, j: (i % pos_tiles, 0)),
        ]
        args += [hg.reshape(1, HEAD_DIM), cos, sin]
    return pl.pallas_call(
        functools.partial(_rms_matmul_kernel, tm=tm, tn=tn, n_rope_tiles=n_rope_tiles),
        grid=(n // tm, e // tn),
        in_specs=in_specs,
        out_specs=pl.BlockSpec((tm, tn), lambda i, j: (i, j)),
        out_shape=jax.ShapeDtypeStruct((n, e), out_dtype),
        scratch_shapes=[pltpu.VMEM((tm, d), BF16)],
        compiler_params=_params("arbitrary", "arbitrary"),
        name="rms_matmul",
    )(*args)


def _matmul_res_kernel(a_ref, w_ref, r_ref, o_ref):
    o_ref[...] = r_ref[...] + jnp.dot(a_ref[...], w_ref[...],
                                      preferred_element_type=F32)


def matmul_residual(a, w, layer, res, *, tm=1024, tn=1024):
    n, d = a.shape
    e = w.shape[2]
    tm = min(tm, n)
    tn = min(tn, e)
    return pl.pallas_call(
        _matmul_res_kernel,
        grid=(n // tm, e // tn),
        in_specs=[
            pl.BlockSpec((tm, d), lambda i, j: (i, 0)),
            pl.BlockSpec((None, d, tn), lambda i, j: (layer, 0, j)),
            pl.BlockSpec((tm, tn), lambda i, j: (i, j)),
        ],
        out_specs=pl.BlockSpec((tm, tn), lambda i, j: (i, j)),
        out_shape=jax.ShapeDtypeStruct((n, e), F32),
        compiler_params=_params("arbitrary", "arbitrary"),
        name="matmul_residual",
    )(a, w, res)


def _hgrn_kernel(zq_ref, zf_ref, zi_ref, zg_ref, lb_ref, og_ref, o_ref,
                 st_ref, g_scr, h_scr, a_scr, u_scr, *, n_groups):
    C = HGRN_CHUNK
    SB = HGRN_SUB
    NB = C // SB

    @pl.when(pl.program_id(2) == 0)
    def _():
        st_ref[...] = jnp.zeros_like(st_ref)

    lb = lb_ref[...]
    og = og_ref[...]
    row_c = lax.broadcasted_iota(jnp.int32, (C, C), 0)
    col_c = lax.broadcasted_iota(jnp.int32, (C, C), 1)
    causal = row_c >= col_c
    tril = causal.astype(F32)
    lane_s = lax.broadcasted_iota(jnp.int32, (SB, C), 1)
    zero_blk = jnp.zeros((SB, HEAD_DIM), F32)
    halves = [h for h in (32, 16, 8) if SB <= h < C]
    level_mask = {half: ((row_c // (2 * half) == col_c // (2 * half))
                         & ((row_c // half) % 2 == 1) & ((col_c // half) % 2 == 0))
                  for half in halves}

    def blk(x, b):
        return x[b * SB:(b + 1) * SB, :]

    def gates(rows, u):
        f = lb + (1.0 - lb) * jax.nn.sigmoid(zf_ref[0, rows, :])
        G2 = jnp.dot(tril, jnp.log2(f), precision=HIGHEST, preferred_element_type=F32)
        g_scr[u] = G2
        h_scr[u] = G2 - jnp.log2(1.0 - f)

    def scores(rows, u):
        q = zq_ref[0, rows, :]
        G2 = g_scr[u]
        H2 = h_scr[u]
        A = jnp.zeros((C, C), F32)
        for half in halves:
            per_half = half // SB
            qa, kb = [], []
            for b in range(NB):
                grp = b // per_half
                if grp % 2 == 1:
                    r = grp * half - 1
                    qa.append(blk(q, b) * jnp.exp2(blk(G2, b) - g_scr[u, r:r + 1, :]))
                    kb.append(zero_blk)
                else:
                    r = (grp + 1) * half - 1
                    kb.append(jnp.exp2(g_scr[u, r:r + 1, :] - blk(H2, b)))
                    qa.append(zero_blk)
            sq = lax.dot_general(jnp.concatenate(qa, axis=0).astype(BF16),
                                 jnp.concatenate(kb, axis=0).astype(BF16), NT_DIMS,
                                 preferred_element_type=F32)
            A = A + jnp.where(level_mask[half], sq, 0.0)
        a_parts = []
        for b in range(NB):
            Gt = blk(G2, b)
            qt = blk(q, b)
            a_b = jnp.zeros((SB, C), F32)
            for s in range(b * SB, (b + 1) * SB):
                e = jnp.exp2(Gt - h_scr[u, s:s + 1, :])
                a_b = jnp.where(lane_s == s, jnp.sum(qt * e, axis=-1, keepdims=True), a_b)
            a_parts.append(a_b)
        A = jnp.where(causal, A + jnp.concatenate(a_parts, axis=0), 0.0)
        a_scr[u] = A.astype(BF16)

    def state_input(rows, u):
        G2_end = g_scr[u, C - 1:C, :]
        kd = jnp.exp2(G2_end - h_scr[u])
        u_scr[u] = jnp.dot(zi_ref[0, rows, :].T.astype(BF16), kd.astype(BF16),
                           preferred_element_type=F32)

    def output(rows, u):
        G2 = g_scr[u]
        st = st_ref[...]
        qg = (zq_ref[0, rows, :] * jnp.exp2(G2)).astype(BF16)
        o = (lax.dot_general(qg, st.astype(BF16), NT_DIMS, preferred_element_type=F32)
             + jnp.dot(a_scr[u], zi_ref[0, rows, :].astype(BF16), preferred_element_type=F32))
        st_ref[...] = jnp.exp2(G2[C - 1:C, :]) * st + u_scr[u]
        ms = jnp.mean(o * o, axis=-1, keepdims=True)
        zg = zg_ref[0, rows, :]
        o_ref[0, rows, :] = (o * lax.rsqrt(ms + RMS_EPS) * og * jax.nn.silu(zg)).astype(o_ref.dtype)

    def group(gi, carry):
        base = pl.multiple_of(gi * (HGRN_UNROLL * C), HGRN_UNROLL * C)
        for phase in (gates, scores, state_input, output):
            for u in range(HGRN_UNROLL):
                phase(pl.ds(base + u * C, C), u)
        return carry

    lax.fori_loop(0, n_groups, group, 0)


def hgrn2_scan(z, lb, o_gain, *, tt=4096):
    b, t, d4 = z.shape
    d = d4 // 4
    nh = d // HEAD_DIM
    tt = min(tt, t)

    def zspec(part):
        return pl.BlockSpec((1, tt, HEAD_DIM), lambda bi, h, ti: (bi, ti, part * nh + h))

    return pl.pallas_call(
        functools.partial(_hgrn_kernel, n_groups=tt // (HGRN_CHUNK * HGRN_UNROLL)),
        grid=(b, nh, t // tt),
        in_specs=[zspec(0), zspec(1), zspec(2), zspec(3),
                  pl.BlockSpec((1, HEAD_DIM), lambda bi, h, ti: (0, h)),
                  pl.BlockSpec((1, HEAD_DIM), lambda bi, h, ti: (0, 0))],
        out_specs=pl.BlockSpec((1, tt, HEAD_DIM), lambda bi, h, ti: (bi, ti, h)),
        out_shape=jax.ShapeDtypeStruct((b, t, d), BF16),
        scratch_shapes=[pltpu.VMEM((HEAD_DIM, HEAD_DIM), F32),
                        pltpu.VMEM((HGRN_UNROLL, HGRN_CHUNK, HEAD_DIM), F32),
                        pltpu.VMEM((HGRN_UNROLL, HGRN_CHUNK, HEAD_DIM), F32),
                        pltpu.VMEM((HGRN_UNROLL, HGRN_CHUNK, HGRN_CHUNK), BF16),
                        pltpu.VMEM((HGRN_UNROLL, HEAD_DIM, HEAD_DIM), F32)],
        compiler_params=_params("arbitrary", "arbitrary", "arbitrary"),
        name="hgrn2_scan",
    )(z, z, z, z, lb.reshape(1, d), o_gain.reshape(1, HEAD_DIM))


def _moba_kernel(q_ref, k_ref, v_ref, o_ref, km_ref, vt_ref, qt_ref, bias_ref, m_ref, l_ref, acc_ref,
                 *, n_blocks, heads):
    BLK = MOBA_BLOCK
    qi = pl.program_id(2)
    exp2_scale = HEAD_DIM ** -0.5 * 1.4426950408889634

    def head(h):
        return slice(h * HEAD_DIM, (h + 1) * HEAD_DIM)

    eye = (lax.broadcasted_iota(jnp.int32, (HEAD_DIM, HEAD_DIM), 0)
           == lax.broadcasted_iota(jnp.int32, (HEAD_DIM, HEAD_DIM), 1)).astype(BF16)

    @pl.when(qi == 0)
    def _():
        for h in range(heads):
            for j in range(n_blocks):
                rows = slice(j * BLK, (j + 1) * BLK)
                kb = k_ref[0, rows, head(h)].astype(F32)
                km_ref[h, j:j + 1, :] = jnp.mean(kb, axis=0, keepdims=True)
                vt_ref[h, :, rows] = lax.dot_general(eye, v_ref[0, rows, head(h)], NT_DIMS,
                                                     preferred_element_type=F32).astype(BF16)

    blk_id = lax.broadcasted_iota(jnp.int32, (n_blocks, BLK), 0)
    past = blk_id < qi
    key_i = lax.broadcasted_iota(jnp.int32, (BLK, BLK), 0)
    qry_i = lax.broadcasted_iota(jnp.int32, (BLK, BLK), 1)
    own = pl.ds(pl.multiple_of(qi * BLK, BLK), BLK)

    def scores(rows):
        return [jnp.dot(k_ref[0, rows, head(h)], qt_ref[h], preferred_element_type=F32)
                for h in range(heads)]

    for h in range(heads):
        qt_ref[h] = lax.dot_general(eye, q_ref[0, :, head(h)], NT_DIMS,
                                    preferred_element_type=F32).astype(BF16)
    for h in range(heads):
        gate = jnp.dot(km_ref[h], qt_ref[h].astype(F32), precision=HIGHEST,
                       preferred_element_type=F32)
        gm = jnp.where(past, gate, -jnp.inf)
        cnt = jnp.zeros((n_blocks, BLK), jnp.int32)
        for m in range(n_blocks):
            gmm = gm[m:m + 1, :]
            ahead = (gmm > gm) | ((gmm == gm) & (blk_id > m))
            cnt = cnt + ahead.astype(jnp.int32)
        bias_ref[h] = jnp.where(past & (cnt < MOBA_TOP_K), 0.0, -jnp.inf).astype(F32)

    probs = []
    for h, s in enumerate(scores(own)):
        s = jnp.where(key_i <= qry_i, s, -jnp.inf)
        m0 = jnp.max(s, axis=0, keepdims=True)
        p = jnp.exp2((s - m0) * exp2_scale)
        m_ref[h] = m0
        l_ref[h] = jnp.sum(p, axis=0, keepdims=True)
        probs.append(p.astype(BF16))
    for h in range(heads):
        acc_ref[h] = jnp.dot(vt_ref[h, :, own], probs[h], preferred_element_type=F32)

    def past_blocks(js):
        rows = [slice(j * BLK, (j + 1) * BLK) for j in js]
        s_all = [scores(r) for r in rows]
        for j, r, s_heads in zip(js, rows, s_all):
            probs, alphas = [], []
            for h, sj in enumerate(s_heads):
                sj = sj + bias_ref[h, j:j + 1, :]
                m_old = m_ref[h]
                m_new = jnp.maximum(m_old, jnp.max(sj, axis=0, keepdims=True))
                alpha = jnp.exp2((m_old - m_new) * exp2_scale)
                pj = jnp.exp2((sj - m_new) * exp2_scale)
                m_ref[h] = m_new
                l_ref[h] = alpha * l_ref[h] + jnp.sum(pj, axis=0, keepdims=True)
                probs.append(pj.astype(BF16))
                alphas.append(alpha)
            for h in range(heads):
                acc_ref[h] = alphas[h] * acc_ref[h] + jnp.dot(vt_ref[h, :, r], probs[h],
                                                              preferred_element_type=F32)

    for j in range(0, n_blocks - 1, 2):
        if j + 1 < n_blocks - 1:
            @pl.when(j + 1 < qi)
            def _(j=j):
                past_blocks((j, j + 1))

        @pl.when(qi == j + 1)
        def _(j=j):
            past_blocks((j,))

    for h in range(heads):
        o_ref[0, :, head(h)] = (acc_ref[h] / l_ref[h]).T.astype(o_ref.dtype)


def moba_attention(q, kv, *, heads=4):
    b, t, d = q.shape
    nh = d // HEAD_DIM
    heads = min(heads, nh)
    ng = nh // heads
    nb = t // MOBA_BLOCK
    w = heads * HEAD_DIM
    return pl.pallas_call(
        functools.partial(_moba_kernel, n_blocks=nb, heads=heads),
        grid=(b, ng, nb),
        in_specs=[pl.BlockSpec((1, MOBA_BLOCK, w), lambda bi, g, qi: (bi, qi, g)),
                  pl.BlockSpec((1, t, w), lambda bi, g, qi: (bi, 0, g)),
                  pl.BlockSpec((1, t, w), lambda bi, g, qi: (bi, 0, ng + g))],
        out_specs=pl.BlockSpec((1, MOBA_BLOCK, w), lambda bi, g, qi: (bi, qi, g)),
        out_shape=jax.ShapeDtypeStruct((b, t, d), BF16),
        scratch_shapes=[pltpu.VMEM((heads, nb, HEAD_DIM), F32),
                        pltpu.VMEM((heads, HEAD_DIM, t), BF16),
                        pltpu.VMEM((heads, HEAD_DIM, MOBA_BLOCK), BF16),
                        pltpu.VMEM((heads, nb, MOBA_BLOCK), F32),
                        pltpu.VMEM((heads, 1, MOBA_BLOCK), F32),
                        pltpu.VMEM((heads, 1, MOBA_BLOCK), F32),
                        pltpu.VMEM((heads, HEAD_DIM, MOBA_BLOCK), F32)],
        compiler_params=_params("arbitrary", "arbitrary", "arbitrary"),
        name="moba_attention",
    )(q, kv, kv)


def _router_kernel(x_ref, g_ref, wr_ref, br_ref, h_ref, ids_ref, wts_ref, *, tm):
    _rms_rows_to(x_ref, g_ref, h_ref, tm, 64)
    h = h_ref[...]
    hi = h.astype(BF16)
    lo = (h - hi.astype(F32)).astype(BF16)
    w = wr_ref[...]
    hw = jnp.dot(hi, w, preferred_element_type=F32)
    logits = (hw[:, :ROUTER_LANES] + hw[:, ROUTER_LANES:]
              + jnp.dot(lo, w[:, :ROUTER_LANES], preferred_element_type=F32)
              + br_ref[...])
    lane = lax.broadcasted_iota(jnp.int32, logits.shape, 1)
    neg = -jnp.inf
    big = ROUTER_LANES

    gl = jnp.where(lane < N_GROUPS, logits, neg)
    gmax = jnp.max(gl, axis=-1, keepdims=True)
    gsum = jnp.sum(jnp.exp(gl - gmax), axis=-1, keepdims=True)
    g_top_p = 1.0 / gsum
    g_top = jnp.min(jnp.where(gl == gmax, lane, big), axis=-1, keepdims=True)

    lo = N_GROUPS + EXPERTS_PER_GROUP * g_top
    el = jnp.where((lane >= lo) & (lane < lo + EXPERTS_PER_GROUP), logits, neg)
    emax = jnp.max(el, axis=-1, keepdims=True)
    esum = jnp.sum(jnp.exp(el - emax), axis=-1, keepdims=True)
    i1 = jnp.min(jnp.where(el == emax, lane, big), axis=-1, keepdims=True)
    el2 = jnp.where(lane == i1, neg, el)
    emax2 = jnp.max(el2, axis=-1, keepdims=True)
    i2 = jnp.min(jnp.where(el2 == emax2, lane, big), axis=-1, keepdims=True)
    p1 = 1.0 / esum
    p2 = jnp.exp(emax2 - emax) / esum
    psum = p1 + p2
    w1 = g_top_p * p1 / psum
    w2 = g_top_p * p2 / psum

    ids_ref[...] = jnp.where(lane == 0, i1 - N_GROUPS, jnp.where(lane == 1, i2 - N_GROUPS, 0))
    wts_ref[...] = jnp.where(lane == 0, w1, jnp.where(lane == 1, w2, 0.0))


def moe_router(x, gain, w_router, b_router, *, tm=512):
    n, d = x.shape
    tm = min(tm, n)
    return pl.pallas_call(
        functools.partial(_router_kernel, tm=tm),
        grid=(n // tm,),
        in_specs=[pl.BlockSpec((tm, d), lambda i: (i, 0)),
                  pl.BlockSpec((1, d), lambda i: (0, 0)),
                  pl.BlockSpec((d, 2 * ROUTER_LANES), lambda i: (0, 0)),
                  pl.BlockSpec((1, ROUTER_LANES), lambda i: (0, 0))],
        out_specs=[pl.BlockSpec((tm, d), lambda i: (i, 0)),
                   pl.BlockSpec((tm, ROUTER_LANES), lambda i: (i, 0)),
                   pl.BlockSpec((tm, ROUTER_LANES), lambda i: (i, 0))],
        out_shape=[jax.ShapeDtypeStruct((n, d), F32),
                   jax.ShapeDtypeStruct((n, ROUTER_LANES), jnp.int32),
                   jax.ShapeDtypeStruct((n, ROUTER_LANES), F32)],
        compiler_params=_params("arbitrary"),
        name="moe_router",
    )(x, gain.reshape(1, d), w_router, b_router)


def _expert_kernel(blk_e_ref, n_used_ref, seg_ref, tok0_ref, tok1_ref, tok_ref, h_hbm, wg_hbm, wu_hbm,
                   wd_hbm, o_ref, xbuf0, xbuf1, xbuf2, wgf, wuf, wdf, wgb, wub, wdb, sem_in, sem_w,
                   *, rows, layer):
    i = pl.program_id(0)
    n_used = n_used_ref[0]
    xbufs = (xbuf0, xbuf1, xbuf2)
    n_buf = len(xbufs)

    def gather(idx_ref, slot):
        for r in range(rows):
            pltpu.make_async_copy(h_hbm.at[pl.ds(idx_ref[0, 0, r], 1), :],
                                  xbufs[slot].at[pl.ds(r, 1), :],
                                  sem_in.at[slot]).start(priority=ROW_DMA_PRIORITY)

    def wait_gather(slot):
        pltpu.make_async_copy(h_hbm.at[pl.ds(0, rows), :], xbufs[slot], sem_in.at[slot]).wait()

    def weight_copies(e, slot):
        return [pltpu.make_async_copy(w_hbm.at[layer, e], buf.at[slot], sem_w.at[slot])
                for w_hbm, buf in ((wg_hbm, wgf), (wu_hbm, wuf), (wd_hbm, wdf))]

    @pl.when(i == 0)
    def _():
        gather(tok0_ref, 0)
        gather(tok1_ref, 1)
        for cp in weight_copies(blk_e_ref[0], 0):
            cp.start()

    @pl.when((i < n_used) & (seg_ref[0, i] == 1))
    def _():
        slot = seg_ref[1, i]
        for cp in weight_copies(blk_e_ref[i], slot):
            cp.wait()
        wgb[...] = wgf[slot].astype(BF16)
        wub[...] = wuf[slot].astype(BF16)
        wdb[...] = wdf[slot].astype(BF16)

        @pl.when(seg_ref[2, i] == 1)
        def _():
            for cp in weight_copies(seg_ref[3, i], 1 - slot):
                cp.start()

    def step(slot):
        wait_gather(slot)
        gather(tok_ref, (slot + 2) % n_buf)
        x = xbufs[slot][...].astype(BF16)
        g = jnp.dot(x, wgb[...], preferred_element_type=F32)
        u = jnp.dot(x, wub[...], preferred_element_type=F32)
        mid = (jax.nn.silu(g) * u).astype(BF16)
        o_ref[...] = jnp.dot(mid, wdb[...], preferred_element_type=F32)

    for slot in range(n_buf):
        @pl.when((i < n_used) & (i % n_buf == slot))
        def _(slot=slot):
            step(slot)

    @pl.when(i == n_used - 1)
    def _():
        for slot in range(n_buf):
            @pl.when(i % n_buf == slot)
            def _(slot=slot):
                wait_gather((slot + 1) % n_buf)
                wait_gather((slot + 2) % n_buf)

    @pl.when(i >= n_used)
    def _():
        o_ref[...] = jnp.zeros_like(o_ref)


def expert_mlp(h, blk_e, n_used, seg, tok, w_gate, w_up, w_down, layer):
    n, d = h.shape
    n_blk, r = tok.shape[0] - 2, tok.shape[2]
    ff = w_gate.shape[3]

    def smem_rows(index_map):
        return pl.BlockSpec((1, 1, r), index_map, memory_space=pltpu.SMEM)

    grid_spec = pltpu.PrefetchScalarGridSpec(
        num_scalar_prefetch=3,
        grid=(n_blk,),
        in_specs=[
            smem_rows(lambda i, be, nu, sg: (0, 0, 0)),
            smem_rows(lambda i, be, nu, sg: (1, 0, 0)),
            smem_rows(lambda i, be, nu, sg: (i + 2, 0, 0)),
            pl.BlockSpec(memory_space=pl.ANY),
            pl.BlockSpec(memory_space=pl.ANY),
            pl.BlockSpec(memory_space=pl.ANY),
            pl.BlockSpec(memory_space=pl.ANY),
        ],
        out_specs=pl.BlockSpec((r, d), lambda i, be, nu, sg: (i, 0)),
        scratch_shapes=[
            pltpu.VMEM((r, d), F32), pltpu.VMEM((r, d), F32), pltpu.VMEM((r, d), F32),
            pltpu.VMEM((2, d, ff), F32), pltpu.VMEM((2, d, ff), F32), pltpu.VMEM((2, ff, d), F32),
            pltpu.VMEM((d, ff), BF16),
            pltpu.VMEM((d, ff), BF16),
            pltpu.VMEM((ff, d), BF16),
            pltpu.SemaphoreType.DMA((3,)),
            pltpu.SemaphoreType.DMA((2,)),
        ],
    )
    return pl.pallas_call(
        functools.partial(_expert_kernel, rows=r, layer=layer),
        grid_spec=grid_spec,
        out_shape=jax.ShapeDtypeStruct((n_blk * r, d), F32),
        compiler_params=_params("arbitrary"),
        name="expert_mlp",
    )(blk_e, n_used, seg, tok, tok, tok, h, w_gate, w_up, w_down)


def _combine_kernel(row0_ref, row_ref, x_ref, w_ref, y_hbm, o_ref, ybuf0, ybuf1, sem, *, tm, n_tiles):
    i = pl.program_id(0)
    ybufs = (ybuf0, ybuf1)

    def gather(idx_ref, slot):
        for r in range(TOP_K_IN_GROUP * tm):
            pltpu.make_async_copy(y_hbm.at[pl.ds(idx_ref[0, 0, r], 1), :],
                                  ybufs[slot].at[pl.ds(r, 1), :],
                                  sem.at[slot]).start(priority=r % DMA_THREADS)

    def wait_gather(slot):
        pltpu.make_async_copy(y_hbm.at[pl.ds(0, TOP_K_IN_GROUP * tm), :], ybufs[slot],
                              sem.at[slot]).wait()

    @pl.when(i == 0)
    def _():
        gather(row0_ref, 0)

    def step(slot):
        wait_gather(slot)
        gather(row_ref, 1 - slot)
        w = w_ref[...]
        o_ref[...] = x_ref[...] + (w[:, 0:1] * ybufs[slot][0:tm, :]
                                   + w[:, 1:2] * ybufs[slot][tm:2 * tm, :])

    for slot in range(2):
        @pl.when(i % 2 == slot)
        def _(slot=slot):
            step(slot)

    @pl.when(i == n_tiles - 1)
    def _():
        wait_gather(n_tiles % 2)


def moe_combine(x, y, rows, wts, *, tm=COMBINE_ROWS):
    n, d = x.shape
    nt = n // tm

    def smem_rows(index_map):
        return pl.BlockSpec((1, 1, TOP_K_IN_GROUP * tm), index_map, memory_space=pltpu.SMEM)

    return pl.pallas_call(
        functools.partial(_combine_kernel, tm=tm, n_tiles=nt),
        grid=(nt,),
        in_specs=[smem_rows(lambda i: (0, 0, 0)),
                  smem_rows(lambda i: (jnp.minimum(i + 1, nt - 1), 0, 0)),
                  pl.BlockSpec((tm, d), lambda i: (i, 0)),
                  pl.BlockSpec((tm, ROUTER_LANES), lambda i: (i, 0)),
                  pl.BlockSpec(memory_space=pl.ANY)],
        out_specs=pl.BlockSpec((tm, d), lambda i: (i, 0)),
        out_shape=jax.ShapeDtypeStruct((n, d), F32),
        scratch_shapes=[pltpu.VMEM((TOP_K_IN_GROUP * tm, d), F32),
                        pltpu.VMEM((TOP_K_IN_GROUP * tm, d), F32),
                        pltpu.SemaphoreType.DMA((2,))],
        compiler_params=_params("arbitrary"),
        name="moe_combine",
    )(rows, rows, x, wts, y)


def _expert_row_index(expert_id, rows_per_block, combine_rows):
    n, k = expert_id.shape
    a_total = n * k
    r = rows_per_block
    flat_e = expert_id.reshape(-1)
    order = jnp.argsort(flat_e).astype(jnp.int32)
    rank = jnp.argsort(order).astype(jnp.int32)
    experts = jnp.arange(N_EXPERTS, dtype=jnp.int32)
    counts = jnp.sum((flat_e[:, None] == experts[None, :]).astype(jnp.int32), axis=0)
    padded = (counts + r - 1) // r * r
    pad_end = jnp.cumsum(padded)
    pad_start = pad_end - padded
    start = jnp.cumsum(counts) - counts
    n_blk = (a_total + N_EXPERTS * (r - 1) + r - 1) // r
    n_used = (pad_end[-1] // r).astype(jnp.int32)
    blk = jnp.arange(n_blk, dtype=jnp.int32)
    blk_row0 = blk * r
    blk_e = jnp.minimum(jnp.sum((pad_end[None, :] <= blk_row0[:, None]).astype(jnp.int32), axis=1),
                        N_EXPERTS - 1).astype(jnp.int32)
    in_e = blk_row0 - pad_start[blk_e]
    nvalid = jnp.clip(counts[blk_e] - in_e, 0, r)
    lane = jnp.arange(r, dtype=jnp.int32)[None, :]
    valid = (lane < nvalid[:, None]) & (blk[:, None] < n_used)
    idx = (start[blk_e] + in_e)[:, None] + lane
    src = order[jnp.clip(idx, 0, a_total - 1)]
    tok = jnp.where(valid, src // k, 0)
    tok = jnp.concatenate([tok, jnp.zeros((2, r), jnp.int32)], axis=0)
    shift = jnp.sum(jnp.where(flat_e[:, None] == experts[None, :], (pad_start - start)[None, :], 0),
                    axis=1)
    row = (rank + shift).astype(jnp.int32).reshape(n, k)
    row = row.reshape(n // combine_rows, combine_rows, k).transpose(0, 2, 1)
    used = blk < n_used
    first = used & ((blk == 0) | (blk_e != jnp.roll(blk_e, 1)))
    parity = (jnp.cumsum(first.astype(jnp.int32)) - 1) % 2
    next_blk = pad_end[blk_e] // r
    has_next = used & (next_blk < n_used)
    next_e = blk_e[jnp.minimum(next_blk, n_blk - 1)]
    seg = jnp.stack([first.astype(jnp.int32), parity, has_next.astype(jnp.int32), next_e])
    return (blk_e, n_used.reshape(1), seg.astype(jnp.int32), tok.reshape(n_blk + 2, 1, r),
            row.reshape(n // combine_rows, 1, k * combine_rows))


def hier_moe(x, gain, w_group, b_group, w_expert, b_expert, w_gate, w_up, w_down, layer):
    n, d = x.shape
    pad = ROUTER_LANES - N_GROUPS - N_EXPERTS
    w_router = jnp.concatenate([w_group, w_expert, jnp.zeros((d, pad), F32)], axis=1)
    w_hi = w_router.astype(BF16)
    w_lo = (w_router - w_hi.astype(F32)).astype(BF16)
    b_router = jnp.concatenate([b_group, b_expert, jnp.zeros((pad,), F32)]).reshape(1, ROUTER_LANES)
    h, ids, wts = moe_router(x, gain, jnp.concatenate([w_hi, w_lo], axis=1), b_router)
    tm = min(COMBINE_ROWS, n)
    blk_e, n_used, seg, tok, rows = _expert_row_index(ids[:, :TOP_K_IN_GROUP], EXPERT_ROWS, tm)
    y = expert_mlp(h, blk_e, n_used, seg, tok, w_gate, w_up, w_down, layer)
    return moe_combine(x, y, rows, wts, tm=tm)


def _rope_tables(t):
    inv_freq = ROPE_THETA ** (-jnp.arange(0, HEAD_DIM, 2, dtype=F32) / HEAD_DIM)
    ang = jnp.arange(t, dtype=F32)[:, None] * inv_freq[None, :]
    cos, sin = jnp.cos(ang), jnp.sin(ang)
    return jnp.concatenate([cos, cos], axis=1), jnp.concatenate([-sin, sin], axis=1)


def kernel(x, norm_mix, norm_ffn, a_w_in, a_lb_logits, a_o_gain, a_w_out, kv_norm, kv_w, k_norm,
           b_w_q, b_q_norm, b_w_out, moe_w_group, moe_b_group, moe_w_expert, moe_b_expert,
           moe_w_gate, moe_w_up, moe_w_down):
    b, t, d = x.shape
    n = b * t
    depth = norm_mix.shape[0]
    n_a = a_w_in.shape[0]
    lower_bounds = jnp.cumsum(jax.nn.softmax(a_lb_logits.astype(F32), axis=0), axis=0)
    cos, sin = _rope_tables(t)
    xf = x.reshape(n, d)
    kv = None
    for l in range(depth):
        if l < n_a:
            z = rms_matmul(xf, norm_mix[l], a_w_in.astype(BF16), l, F32)
            o = hgrn2_scan(z.reshape(b, t, 4 * d), lower_bounds[l], a_o_gain[l])
            xf = matmul_residual(o.reshape(n, d), a_w_out.astype(BF16), l, xf)
        else:
            if l == n_a:
                kv = rms_matmul(xf, kv_norm, kv_w[None].astype(BF16), 0, BF16,
                                rope=(k_norm, cos, sin, d), seq_len=t)
            lb_ = l - n_a
            q = rms_matmul(xf, norm_mix[l], b_w_q.astype(BF16), lb_, BF16,
                           rope=(b_q_norm[lb_], cos, sin, d), seq_len=t)
            o = moba_attention(q.reshape(b, t, d), kv.reshape(b, t, 2 * d))
            xf = matmul_residual(o.reshape(n, d), b_w_out.astype(BF16), lb_, xf)
        xf = hier_moe(xf, norm_ffn[l], moe_w_group[l], moe_b_group[l], moe_w_expert[l],
                      moe_b_expert[l], moe_w_gate, moe_w_up, moe_w_down, l)
    return xf.reshape(b, t, d)
```

```python
import functools

import jax
import jax.numpy as jnp
from jax import lax
from jax.experimental import pallas as pl
from jax.experimental.pallas import tpu as pltpu

HEAD_DIM = 128
HGRN_CHUNK = 64
HGRN_SUB = 8
HGRN_UNROLL = 8
MOBA_BLOCK = 256
MOBA_TOP_K = 3
ROPE_THETA = 10000.0
N_GROUPS = 4
EXPERTS_PER_GROUP = 8
N_EXPERTS = N_GROUPS * EXPERTS_PER_GROUP
TOP_K_IN_GROUP = 2
RMS_EPS = 1e-6
ROUTER_LANES = 128
EXPERT_ROWS = 512
EXPERT_ROW_BUFS = 2
COMBINE_ROWS = 256
DMA_THREADS = 2
ROW_DMA_PRIORITY = 1
ROPE_ROW_PARTS = 4

V7X_VMEM_LIMIT = 56 * 1024 * 1024

F32 = jnp.float32
BF16 = jnp.bfloat16
HIGHEST = lax.Precision.HIGHEST
NT_DIMS = (((1,), (1,)), ((), ()))


def _params(*sem):
    return pltpu.CompilerParams(dimension_semantics=sem, vmem_limit_bytes=V7X_VMEM_LIMIT)


def _rms_rows_to(x_ref, g_ref, out_ref, rows, chunk):
    gain = g_ref[...]

    def body(c, carry):
        sl = pl.ds(pl.multiple_of(c * chunk, chunk), chunk)
        x = x_ref[sl, :]
        ms = jnp.mean(x * x, axis=-1, keepdims=True)
        out_ref[sl, :] = (x * lax.rsqrt(ms + RMS_EPS) * gain).astype(out_ref.dtype)
        return carry

    lax.fori_loop(0, rows // chunk, body, 0)


def _head_norm_rope(y, hg, cos, sin):
    ms = jnp.mean(y * y, axis=-1, keepdims=True)
    yn = y * lax.rsqrt(ms + RMS_EPS) * hg
    return yn * cos + pltpu.roll(yn, HEAD_DIM // 2, axis=1) * sin


def _rms_matmul_kernel(*refs, tm, tn, n_rope_tiles):
    if n_rope_tiles:
        x_ref, g_ref, w_ref, hg_ref, cos_ref, sin_ref, o_ref, hn_ref = refs
    else:
        x_ref, g_ref, w_ref, o_ref, hn_ref = refs
    j = pl.program_id(1)

    @pl.when(j == 0)
    def _():
        _rms_rows_to(x_ref, g_ref, hn_ref, tm, 64)

    def plain():
        o_ref[...] = jnp.dot(hn_ref[...], w_ref[...],
                             preferred_element_type=F32).astype(o_ref.dtype)

    if not n_rope_tiles:
        plain()
        return

    @pl.when(j < n_rope_tiles)
    def _():
        w = w_ref[...]
        hg = hg_ref[...]
        part = tm // ROPE_ROW_PARTS
        for p in range(ROPE_ROW_PARTS):
            rows = slice(p * part, (p + 1) * part)
            y = jnp.dot(hn_ref[rows, :], w, preferred_element_type=F32)
            for h in range(tn // HEAD_DIM):
                sl = slice(h * HEAD_DIM, (h + 1) * HEAD_DIM)
                o_ref[rows, sl] = _head_norm_rope(y[:, sl], hg, cos_ref[rows, :],
                                                  sin_ref[rows, :]).astype(o_ref.dtype)

    pl.when(j >= n_rope_tiles)(plain)


def rms_matmul(x, gain, w, layer, out_dtype, *, rope=None, seq_len=None, tm=1024, tn=1024):
    n, d = x.shape
    e = w.shape[2]
    tm = min(tm, n)
    tn = min(tn, e)
    in_specs = [
        pl.BlockSpec((tm, d), lambda i, j: (i, 0)),
        pl.BlockSpec((1, d), lambda i, j: (0, 0)),
        pl.BlockSpec((None, d, tn), lambda i, j: (layer, 0, j)),
    ]
    args = [x, gain.reshape(1, d), w]
    n_rope_tiles = 0
    if rope is not None:
        hg, cos, sin, n_rope_cols = rope
        tm = min(tm, seq_len)
        tn = min(tn, n_rope_cols)
        in_specs[0] = pl.BlockSpec((tm, d), lambda i, j: (i, 0))
        in_specs[2] = pl.BlockSpec((None, d, tn), lambda i, j: (layer, 0, j))
        n_rope_tiles = n_rope_cols // tn
        pos_tiles = seq_len // tm
        in_specs += [
            pl.BlockSpec((1, HEAD_DIM), lambda i, j: (0, 0)),
            pl.BlockSpec((tm, HEAD_DIM), lambda i, j: (i % pos_tiles, 0)),
            pl.BlockSpec((tm, HEAD_DIM), lambda i, j: (i % pos_tiles, 0)),
        ]
        args += [hg.reshape(1, HEAD_DIM), cos, sin]
    return pl.pallas_call(
        functools.partial(_rms_matmul_kernel, tm=tm, tn=tn, n_rope_tiles=n_rope_tiles),
        grid=(n // tm, e // tn),
        in_specs=in_specs,
        out_specs=pl.BlockSpec((tm, tn), lambda i, j: (i, j)),
        out_shape=jax.ShapeDtypeStruct((n, e), out_dtype),
        scratch_shapes=[pltpu.VMEM((tm, d), BF16)],
        compiler_params=_params("arbitrary", "arbitrary"),
        name="rms_matmul",
    )(*args)


def _matmul_res_kernel(a_ref, w_ref, r_ref, o_ref):
    o_ref[...] = r_ref[...] + jnp.dot(a_ref[...], w_ref[...],
                                      preferred_element_type=F32)


def matmul_residual(a, w, layer, res, *, tm=1024, tn=1024):
    n, d = a.shape
    e = w.shape[2]
    tm = min(tm, n)
    tn = min(tn, e)
    return pl.pallas_call(
        _matmul_res_kernel,
        grid=(n // tm, e // tn),
        in_specs=[
            pl.BlockSpec((tm, d), lambda i, j: (i, 0)),
            pl.BlockSpec((None, d, tn), lambda i, j: (layer, 0, j)),
            pl.BlockSpec((tm, tn), lambda i, j: (i, j)),
        ],
        out_specs=pl.BlockSpec((tm, tn), lambda i, j: (i, j)),
        out_shape=jax.ShapeDtypeStruct((n, e), F32),
        compiler_params=_params("arbitrary", "arbitrary"),
        name="matmul_residual",
    )(a, w, res)


def _hgrn_kernel(zq_ref, zf_ref, zi_ref, zg_ref, lb_ref, og_ref, o_ref,
                 st_ref, g_scr, h_scr, a_scr, u_scr, *, n_groups):
    C = HGRN_CHUNK
    SB = HGRN_SUB
    NB = C // SB

    @pl.when(pl.program_id(2) == 0)
    def _():
        st_ref[...] = jnp.zeros_like(st_ref)

    lb = lb_ref[...]
    og = og_ref[...]
    row_c = lax.broadcasted_iota(jnp.int32, (C, C), 0)
    col_c = lax.broadcasted_iota(jnp.int32, (C, C), 1)
    causal = row_c >= col_c
    tril = causal.astype(F32)
    lane_s = lax.broadcasted_iota(jnp.int32, (SB, C), 1)
    zero_blk = jnp.zeros((SB, HEAD_DIM), F32)
    halves = [h for h in (32, 16, 8) if SB <= h < C]
    level_mask = {half: ((row_c // (2 * half) == col_c // (2 * half))
                         & ((row_c // half) % 2 == 1) & ((col_c // half) % 2 == 0))
                  for half in halves}

    def blk(x, b):
        return x[b * SB:(b + 1) * SB, :]

    def gates(rows, u):
        f = lb + (1.0 - lb) * jax.nn.sigmoid(zf_ref[0, rows, :])
        G2 = jnp.dot(tril, jnp.log2(f), precision=HIGHEST, preferred_element_type=F32)
        g_scr[u] = G2
        h_scr[u] = G2 - jnp.log2(1.0 - f)

    def scores(rows, u):
        q = zq_ref[0, rows, :]
        G2 = g_scr[u]
        H2 = h_scr[u]
        A = jnp.zeros((C, C), F32)
        for half in halves:
            per_half = half // SB
            qa, kb = [], []
            for b in range(NB):
                grp = b // per_half
                if grp % 2 == 1:
                    r = grp * half - 1
                    qa.append(blk(q, b) * jnp.exp2(blk(G2, b) - g_scr[u, r:r + 1, :]))
                    kb.append(zero_blk)
                else:
                    r = (grp + 1) * half - 1
                    kb.append(jnp.exp2(g_scr[u, r:r + 1, :] - blk(H2, b)))
                    qa.append(zero_blk)
            sq = lax.dot_general(jnp.concatenate(qa, axis=0).astype(BF16),
                                 jnp.concatenate(kb, axis=0).astype(BF16), NT_DIMS,
                                 preferred_element_type=F32)
            A = A + jnp.where(level_mask[half], sq, 0.0)
        a_parts = []
        for b in range(NB):
            Gt = blk(G2, b)
            qt = blk(q, b)
            a_b = jnp.zeros((SB, C), F32)
            for s in range(b * SB, (b + 1) * SB):
                e = jnp.exp2(Gt - h_scr[u, s:s + 1, :])
                a_b = jnp.where(lane_s == s, jnp.sum(qt * e, axis=-1, keepdims=True), a_b)
            a_parts.append(a_b)
        A = jnp.where(causal, A + jnp.concatenate(a_parts, axis=0), 0.0)
        a_scr[u] = A.astype(BF16)

    def state_input(rows, u):
        G2_end = g_scr[u, C - 1:C, :]
        kd = jnp.exp2(G2_end - h_scr[u])
        u_scr[u] = jnp.dot(zi_ref[0, rows, :].T.astype(BF16), kd.astype(BF16),
                           preferred_element_type=F32)

    def output(rows, u):
        G2 = g_scr[u]
        st = st_ref[...]
        qg = (zq_ref[0, rows, :] * jnp.exp2(G2)).astype(BF16)
        o = (lax.dot_general(qg, st.astype(BF16), NT_DIMS, preferred_element_type=F32)
             + jnp.dot(a_scr[u], zi_ref[0, rows, :].astype(BF16), preferred_element_type=F32))
        st_ref[...] = jnp.exp2(G2[C - 1:C, :]) * st + u_scr[u]
        ms = jnp.mean(o * o, axis=-1, keepdims=True)
        zg = zg_ref[0, rows, :]
        o_ref[0, rows, :] = (o * lax.rsqrt(ms + RMS_EPS) * og * jax.nn.silu(zg)).astype(o_ref.dtype)

    def group(gi, carry):
        base = pl.multiple_of(gi * (HGRN_UNROLL * C), HGRN_UNROLL * C)
        for phase in (gates, scores, state_input, output):
            for u in range(HGRN_UNROLL):
                phase(pl.ds(base + u * C, C), u)
        return carry

    lax.fori_loop(0, n_groups, group, 0)


def hgrn2_scan(z, lb, o_gain, *, tt=4096):
    b, t, d4 = z.shape
    d = d4 // 4
    nh = d // HEAD_DIM
    tt = min(tt, t)

    def zspec(part):
        return pl.BlockSpec((1, tt, HEAD_DIM), lambda bi, h, ti: (bi, ti, part * nh + h))

    return pl.pallas_call(
        functools.partial(_hgrn_kernel, n_groups=tt // (HGRN_CHUNK * HGRN_UNROLL)),
        grid=(b, nh, t // tt),
        in_specs=[zspec(0), zspec(1), zspec(2), zspec(3),
                  pl.BlockSpec((1, HEAD_DIM), lambda bi, h, ti: (0, h)),
                  pl.BlockSpec((1, HEAD_DIM), lambda bi, h, ti: (0, 0))],
        out_specs=pl.BlockSpec((1, tt, HEAD_DIM), lambda bi, h, ti: (bi, ti, h)),
        out_shape=jax.ShapeDtypeStruct((b, t, d), BF16),
        scratch_shapes=[pltpu.VMEM((HEAD_DIM, HEAD_DIM), F32),
                        pltpu.VMEM((HGRN_UNROLL, HGRN_CHUNK, HEAD_DIM), F32),
                        pltpu.VMEM((HGRN_UNROLL, HGRN_CHUNK, HEAD_DIM), F32),
                        pltpu.VMEM((HGRN_UNROLL, HGRN_CHUNK, HGRN_CHUNK), BF16),
                        pltpu.VMEM((HGRN_UNROLL, HEAD_DIM, HEAD_DIM), F32)],
        compiler_params=_params("arbitrary", "arbitrary", "arbitrary"),
        name="hgrn2_scan",
    )(z, z, z, z, lb.reshape(1, d), o_gain.reshape(1, HEAD_DIM))


def _moba_kernel(q_ref, k_ref, v_ref, o_ref, km_ref, vt_ref, qt_ref, bias_ref, m_ref, l_ref, acc_ref,
                 *, n_blocks, heads):
    BLK = MOBA_BLOCK
    qi = pl.program_id(2)
    exp2_scale = HEAD_DIM ** -0.5 * 1.4426950408889634

    def head(h):
        return slice(h * HEAD_DIM, (h + 1) * HEAD_DIM)

    eye = (lax.broadcasted_iota(jnp.int32, (HEAD_DIM, HEAD_DIM), 0)
           == lax.broadcasted_iota(jnp.int32, (HEAD_DIM, HEAD_DIM), 1)).astype(BF16)

    @pl.when(qi == 0)
    def _():
        for h in range(heads):
            for j in range(n_blocks):
                rows = slice(j * BLK, (j + 1) * BLK)
                kb = k_ref[0, rows, head(h)].astype(F32)
                km_ref[h, j:j + 1, :] = jnp.mean(kb, axis=0, keepdims=True)
                vt_ref[h, :, rows] = lax.dot_general(eye, v_ref[0, rows, head(h)], NT_DIMS,
                                                     preferred_element_type=F32).astype(BF16)

    blk_id = lax.broadcasted_iota(jnp.int32, (n_blocks, BLK), 0)
    past = blk_id < qi
    key_i = lax.broadcasted_iota(jnp.int32, (BLK, BLK), 0)
    qry_i = lax.broadcasted_iota(jnp.int32, (BLK, BLK), 1)
    own = pl.ds(pl.multiple_of(qi * BLK, BLK), BLK)

    def scores(rows):
        return [jnp.dot(k_ref[0, rows, head(h)], qt_ref[h], preferred_element_type=F32)
                for h in range(heads)]

    for h in range(heads):
        qt_ref[h] = lax.dot_general(eye, q_ref[0, :, head(h)], NT_DIMS,
                                    preferred_element_type=F32).astype(BF16)
    for h in range(heads):
        gate = jnp.dot(km_ref[h], qt_ref[h].astype(F32), precision=HIGHEST,
                       preferred_element_type=F32)
        gm = jnp.where(past, gate, -jnp.inf)
        cnt = jnp.zeros((n_blocks, BLK), jnp.int32)
        for m in range(n_blocks):
            gmm = gm[m:m + 1, :]
            ahead = (gmm > gm) | ((gmm == gm) & (blk_id > m))
            cnt = cnt + ahead.astype(jnp.int32)
        bias_ref[h] = jnp.where(past & (cnt < MOBA_TOP_K), 0.0, -jnp.inf).astype(F32)

    probs = []
    for h, s in enumerate(scores(own)):
        s = jnp.where(key_i <= qry_i, s, -jnp.inf)
        m0 = jnp.max(s, axis=0, keepdims=True)
        p = jnp.exp2((s - m0) * exp2_scale)
        m_ref[h] = m0
        l_ref[h] = jnp.sum(p, axis=0, keepdims=True)
        probs.append(p.astype(BF16))
    for h in range(heads):
        acc_ref[h] = jnp.dot(vt_ref[h, :, own], probs[h], preferred_element_type=F32)

    def past_blocks(js):
        rows = [slice(j * BLK, (j + 1) * BLK) for j in js]
        s_all = [scores(r) for r in rows]
        for j, r, s_heads in zip(js, rows, s_all):
            probs, alphas = [], []
            for h, sj in enumerate(s_heads):
                sj = sj + bias_ref[h, j:j + 1, :]
                m_old = m_ref[h]
                m_new = jnp.maximum(m_old, jnp.max(sj, axis=0, keepdims=True))
                alpha = jnp.exp2((m_old - m_new) * exp2_scale)
                pj = jnp.exp2((sj - m_new) * exp2_scale)
                m_ref[h] = m_new
                l_ref[h] = alpha * l_ref[h] + jnp.sum(pj, axis=0, keepdims=True)
                probs.append(pj.astype(BF16))
                alphas.append(alpha)
            for h in range(heads):
                acc_ref[h] = alphas[h] * acc_ref[h] + jnp.dot(vt_ref[h, :, r], probs[h],
                                                              preferred_element_type=F32)

    for j in range(0, n_blocks - 1, 2):
        if j + 1 < n_blocks - 1:
            @pl.when(j + 1 < qi)
            def _(j=j):
                past_blocks((j, j + 1))

        @pl.when(qi == j + 1)
        def _(j=j):
            past_blocks((j,))

    for h in range(heads):
        o_ref[0, :, head(h)] = (acc_ref[h] / l_ref[h]).T.astype(o_ref.dtype)


def moba_attention(q, kv, *, heads=4):
    b, t, d = q.shape
    nh = d // HEAD_DIM
    heads = min(heads, nh)
    ng = nh // heads
    nb = t // MOBA_BLOCK
    w = heads * HEAD_DIM
    return pl.pallas_call(
        functools.partial(_moba_kernel, n_blocks=nb, heads=heads),
        grid=(b, ng, nb),
        in_specs=[pl.BlockSpec((1, MOBA_BLOCK, w), lambda bi, g, qi: (bi, qi, g)),
                  pl.BlockSpec((1, t, w), lambda bi, g, qi: (bi, 0, g)),
                  pl.BlockSpec((1, t, w), lambda bi, g, qi: (bi, 0, ng + g))],
        out_specs=pl.BlockSpec((1, MOBA_BLOCK, w), lambda bi, g, qi: (bi, qi, g)),
        out_shape=jax.ShapeDtypeStruct((b, t, d), BF16),
        scratch_shapes=[pltpu.VMEM((heads, nb, HEAD_DIM), F32),
                        pltpu.VMEM((heads, HEAD_DIM, t), BF16),
                        pltpu.VMEM((heads, HEAD_DIM, MOBA_BLOCK), BF16),
                        pltpu.VMEM((heads, nb, MOBA_BLOCK), F32),
                        pltpu.VMEM((heads, 1, MOBA_BLOCK), F32),
                        pltpu.VMEM((heads, 1, MOBA_BLOCK), F32),
                        pltpu.VMEM((heads, HEAD_DIM, MOBA_BLOCK), F32)],
        compiler_params=_params("arbitrary", "arbitrary", "arbitrary"),
        name="moba_attention",
    )(q, kv, kv)


def _router_kernel(x_ref, g_ref, wr_ref, br_ref, h_ref, ids_ref, wts_ref, *, tm):
    _rms_rows_to(x_ref, g_ref, h_ref, tm, 64)
    h = h_ref[...]
    hi = h.astype(BF16)
    lo = (h - hi.astype(F32)).astype(BF16)
    w = wr_ref[...]
    hw = jnp.dot(hi, w, preferred_element_type=F32)
    logits = (hw[:, :ROUTER_LANES] + hw[:, ROUTER_LANES:]
              + jnp.dot(lo, w[:, :ROUTER_LANES], preferred_element_type=F32)
              + br_ref[...])
    lane = lax.broadcasted_iota(jnp.int32, logits.shape, 1)
    neg = -jnp.inf
    big = ROUTER_LANES

    gl = jnp.where(lane < N_GROUPS, logits, neg)
    gmax = jnp.max(gl, axis=-1, keepdims=True)
    gsum = jnp.sum(jnp.exp(gl - gmax), axis=-1, keepdims=True)
    g_top_p = 1.0 / gsum
    g_top = jnp.min(jnp.where(gl == gmax, lane, big), axis=-1, keepdims=True)

    lo = N_GROUPS + EXPERTS_PER_GROUP * g_top
    el = jnp.where((lane >= lo) & (lane < lo + EXPERTS_PER_GROUP), logits, neg)
    emax = jnp.max(el, axis=-1, keepdims=True)
    esum = jnp.sum(jnp.exp(el - emax), axis=-1, keepdims=True)
    i1 = jnp.min(jnp.where(el == emax, lane, big), axis=-1, keepdims=True)
    el2 = jnp.where(lane == i1, neg, el)
    emax2 = jnp.max(el2, axis=-1, keepdims=True)
    i2 = jnp.min(jnp.where(el2 == emax2, lane, big), axis=-1, keepdims=True)
    p1 = 1.0 / esum
    p2 = jnp.exp(emax2 - emax) / esum
    psum = p1 + p2
    w1 = g_top_p * p1 / psum
    w2 = g_top_p * p2 / psum

    ids_ref[...] = jnp.where(lane == 0, i1 - N_GROUPS, jnp.where(lane == 1, i2 - N_GROUPS, 0))
    wts_ref[...] = jnp.where(lane == 0, w1, jnp.where(lane == 1, w2, 0.0))


def moe_router(x, gain, w_router, b_router, *, tm=512):
    n, d = x.shape
    tm = min(tm, n)
    return pl.pallas_call(
        functools.partial(_router_kernel, tm=tm),
        grid=(n // tm,),
        in_specs=[pl.BlockSpec((tm, d), lambda i: (i, 0)),
                  pl.BlockSpec((1, d), lambda i: (0, 0)),
                  pl.BlockSpec((d, 2 * ROUTER_LANES), lambda i: (0, 0)),
                  pl.BlockSpec((1, ROUTER_LANES), lambda i: (0, 0))],
        out_specs=[pl.BlockSpec((tm, d), lambda i: (i, 0)),
                   pl.BlockSpec((tm, ROUTER_LANES), lambda i: (i, 0)),
                   pl.BlockSpec((tm, ROUTER_LANES), lambda i: (i, 0))],
        out_shape=[jax.ShapeDtypeStruct((n, d), F32),
                   jax.ShapeDtypeStruct((n, ROUTER_LANES), jnp.int32),
                   jax.ShapeDtypeStruct((n, ROUTER_LANES), F32)],
        compiler_params=_params("arbitrary"),
        name="moe_router",
    )(x, gain.reshape(1, d), w_router, b_router)


def _expert_kernel(blk_e_ref, n_used_ref, seg_ref, *refs, rows, layer, n_buf):
    ahead = n_buf - 1
    tok_first = refs[:ahead]
    tok_ref, h_hbm, wg_hbm, wu_hbm, wd_hbm, o_ref = refs[ahead:ahead + 6]
    xbufs = refs[ahead + 6:ahead + 6 + n_buf]
    wgf, wuf, wdf, wgb, wub, wdb, sem_in, sem_w = refs[ahead + 6 + n_buf:]
    i = pl.program_id(0)
    n_used = n_used_ref[0]

    def gather(idx_ref, slot):
        for r in range(rows):
            pltpu.make_async_copy(h_hbm.at[pl.ds(idx_ref[0, 0, r], 1), :],
                                  xbufs[slot].at[pl.ds(r, 1), :],
                                  sem_in.at[slot]).start(priority=ROW_DMA_PRIORITY)

    def wait_gather(slot):
        pltpu.make_async_copy(h_hbm.at[pl.ds(0, rows), :], xbufs[slot], sem_in.at[slot]).wait()

    def weight_copies(e, slot):
        return [pltpu.make_async_copy(w_hbm.at[layer, e], buf.at[slot], sem_w.at[slot])
                for w_hbm, buf in ((wg_hbm, wgf), (wu_hbm, wuf), (wd_hbm, wdf))]

    @pl.when(i == 0)
    def _():
        for b in range(ahead):
            gather(tok_first[b], b)
        for cp in weight_copies(blk_e_ref[0], 0):
            cp.start()

    @pl.when((i < n_used) & (seg_ref[0, i] == 1))
    def _():
        slot = seg_ref[1, i]
        for cp in weight_copies(blk_e_ref[i], slot):
            cp.wait()
        wgb[...] = wgf[slot].astype(BF16)
        wub[...] = wuf[slot].astype(BF16)
        wdb[...] = wdf[slot].astype(BF16)

        @pl.when(seg_ref[2, i] == 1)
        def _():
            for cp in weight_copies(seg_ref[3, i], 1 - slot):
                cp.start()

    def step(slot):
        wait_gather(slot)
        gather(tok_ref, (slot + ahead) % n_buf)
        x = xbufs[slot][...].astype(BF16)
        g = jnp.dot(x, wgb[...], preferred_element_type=F32)
        u = jnp.dot(x, wub[...], preferred_element_type=F32)
        mid = (jax.nn.silu(g) * u).astype(BF16)
        o_ref[...] = jnp.dot(mid, wdb[...], preferred_element_type=F32)

    for slot in range(n_buf):
        @pl.when((i < n_used) & (i % n_buf == slot))
        def _(slot=slot):
            step(slot)

    @pl.when(i == n_used - 1)
    def _():
        for slot in range(n_buf):
            @pl.when(i % n_buf == slot)
            def _(slot=slot):
                for b in range(1, n_buf):
                    wait_gather((slot + b) % n_buf)

    @pl.when(i >= n_used)
    def _():
        o_ref[...] = jnp.zeros_like(o_ref)


def expert_mlp(h, blk_e, n_used, seg, tok, w_gate, w_up, w_down, layer):
    n, d = h.shape
    ahead = EXPERT_ROW_BUFS - 1
    n_blk, r = tok.shape[0] - ahead, tok.shape[2]
    ff = w_gate.shape[3]

    def smem_rows(index_map):
        return pl.BlockSpec((1, 1, r), index_map, memory_space=pltpu.SMEM)

    grid_spec = pltpu.PrefetchScalarGridSpec(
        num_scalar_prefetch=3,
        grid=(n_blk,),
        in_specs=[
            *[smem_rows(lambda i, be, nu, sg, b=b: (b, 0, 0)) for b in range(ahead)],
            smem_rows(lambda i, be, nu, sg: (i + ahead, 0, 0)),
            pl.BlockSpec(memory_space=pl.ANY),
            pl.BlockSpec(memory_space=pl.ANY),
            pl.BlockSpec(memory_space=pl.ANY),
            pl.BlockSpec(memory_space=pl.ANY),
        ],
        out_specs=pl.BlockSpec((r, d), lambda i, be, nu, sg: (i, 0)),
        scratch_shapes=[
            *[pltpu.VMEM((r, d), F32) for _ in range(EXPERT_ROW_BUFS)],
            pltpu.VMEM((2, d, ff), F32), pltpu.VMEM((2, d, ff), F32), pltpu.VMEM((2, ff, d), F32),
            pltpu.VMEM((d, ff), BF16),
            pltpu.VMEM((d, ff), BF16),
            pltpu.VMEM((ff, d), BF16),
            pltpu.SemaphoreType.DMA((EXPERT_ROW_BUFS,)),
            pltpu.SemaphoreType.DMA((2,)),
        ],
    )
    return pl.pallas_call(
        functools.partial(_expert_kernel, rows=r, layer=layer, n_buf=EXPERT_ROW_BUFS),
        grid_spec=grid_spec,
        out_shape=jax.ShapeDtypeStruct((n_blk * r, d), F32),
        compiler_params=_params("arbitrary"),
        name="expert_mlp",
    )(blk_e, n_used, seg, *([tok] * EXPERT_ROW_BUFS), h, w_gate, w_up, w_down)


def _combine_kernel(row0_ref, row_ref, x_ref, w_ref, y_hbm, o_ref, ybuf0, ybuf1, sem, *, tm, n_tiles):
    i = pl.program_id(0)
    ybufs = (ybuf0, ybuf1)

    def gather(idx_ref, slot):
        for r in range(TOP_K_IN_GROUP * tm):
            pltpu.make_async_copy(y_hbm.at[pl.ds(idx_ref[0, 0, r], 1), :],
                                  ybufs[slot].at[pl.ds(r, 1), :],
                                  sem.at[slot]).start(priority=r % DMA_THREADS)

    def wait_gather(slot):
        pltpu.make_async_copy(y_hbm.at[pl.ds(0, TOP_K_IN_GROUP * tm), :], ybufs[slot],
                              sem.at[slot]).wait()

    @pl.when(i == 0)
    def _():
        gather(row0_ref, 0)

    def step(slot):
        wait_gather(slot)
        gather(row_ref, 1 - slot)
        w = w_ref[...]
        o_ref[...] = x_ref[...] + (w[:, 0:1] * ybufs[slot][0:tm, :]
                                   + w[:, 1:2] * ybufs[slot][tm:2 * tm, :])

    for slot in range(2):
        @pl.when(i % 2 == slot)
        def _(slot=slot):
            step(slot)

    @pl.when(i == n_tiles - 1)
    def _():
        wait_gather(n_tiles % 2)


def moe_combine(x, y, rows, wts, *, tm=COMBINE_ROWS):
    n, d = x.shape
    nt = n // tm

    def smem_rows(index_map):
        return pl.BlockSpec((1, 1, TOP_K_IN_GROUP * tm), index_map, memory_space=pltpu.SMEM)

    return pl.pallas_call(
        functools.partial(_combine_kernel, tm=tm, n_tiles=nt),
        grid=(nt,),
        in_specs=[smem_rows(lambda i: (0, 0, 0)),
                  smem_rows(lambda i: (jnp.minimum(i + 1, nt - 1), 0, 0)),
                  pl.BlockSpec((tm, d), lambda i: (i, 0)),
                  pl.BlockSpec((tm, ROUTER_LANES), lambda i: (i, 0)),
                  pl.BlockSpec(memory_space=pl.ANY)],
        out_specs=pl.BlockSpec((tm, d), lambda i: (i, 0)),
        out_shape=jax.ShapeDtypeStruct((n, d), F32),
        scratch_shapes=[pltpu.VMEM((TOP_K_IN_GROUP * tm, d), F32),
                        pltpu.VMEM((TOP_K_IN_GROUP * tm, d), F32),
                        pltpu.SemaphoreType.DMA((2,))],
        compiler_params=_params("arbitrary"),
        name="moe_combine",
    )(rows, rows, x, wts, y)


def _expert_row_index(expert_id, rows_per_block, combine_rows):
    n, k = expert_id.shape
    a_total = n * k
    r = rows_per_block
    flat_e = expert_id.reshape(-1)
    order = jnp.argsort(flat_e).astype(jnp.int32)
    rank = jnp.argsort(order).astype(jnp.int32)
    experts = jnp.arange(N_EXPERTS, dtype=jnp.int32)
    onehot = experts[:, None] == flat_e[None, :]
    counts = jnp.sum(onehot.astype(jnp.int32), axis=1)
    padded = (counts + r - 1) // r * r
    pad_end = jnp.cumsum(padded)
    pad_start = pad_end - padded
    start = jnp.cumsum(counts) - counts
    n_blk = (a_total + N_EXPERTS * (r - 1) + r - 1) // r
    n_used = (pad_end[-1] // r).astype(jnp.int32)
    blk = jnp.arange(n_blk, dtype=jnp.int32)
    blk_row0 = blk * r
    blk_e = jnp.minimum(jnp.sum((pad_end[None, :] <= blk_row0[:, None]).astype(jnp.int32), axis=1),
                        N_EXPERTS - 1).astype(jnp.int32)
    in_e = blk_row0 - pad_start[blk_e]
    nvalid = jnp.clip(counts[blk_e] - in_e, 0, r)
    lane = jnp.arange(r, dtype=jnp.int32)[None, :]
    valid = (lane < nvalid[:, None]) & (blk[:, None] < n_used)
    idx = (start[blk_e] + in_e)[:, None] + lane
    src = order[jnp.clip(idx, 0, a_total - 1)]
    tok = jnp.where(valid, src // k, 0)
    tok = jnp.concatenate([tok, jnp.zeros((EXPERT_ROW_BUFS - 1, r), jnp.int32)], axis=0)
    shift = jnp.sum(jnp.where(onehot, (pad_start - start)[:, None], 0), axis=0)
    row = (rank + shift).astype(jnp.int32).reshape(n, k)
    row = row.reshape(n // combine_rows, combine_rows, k).transpose(0, 2, 1)
    used = blk < n_used
    first = used & ((blk == 0) | (blk_e != jnp.roll(blk_e, 1)))
    parity = (jnp.cumsum(first.astype(jnp.int32)) - 1) % 2
    next_blk = pad_end[blk_e] // r
    has_next = used & (next_blk < n_used)
    next_e = blk_e[jnp.minimum(next_blk, n_blk - 1)]
    seg = jnp.stack([first.astype(jnp.int32), parity, has_next.astype(jnp.int32), next_e])
    return (blk_e, n_used.reshape(1), seg.astype(jnp.int32),
            tok.reshape(n_blk + EXPERT_ROW_BUFS - 1, 1, r),
            row.reshape(n // combine_rows, 1, k * combine_rows))


def hier_moe(x, gain, w_group, b_group, w_expert, b_expert, w_gate, w_up, w_down, layer):
    n, d = x.shape
    pad = ROUTER_LANES - N_GROUPS - N_EXPERTS
    w_router = jnp.concatenate([w_group, w_expert, jnp.zeros((d, pad), F32)], axis=1)
    w_hi = w_router.astype(BF16)
    w_lo = (w_router - w_hi.astype(F32)).astype(BF16)
    b_router = jnp.concatenate([b_group, b_expert, jnp.zeros((pad,), F32)]).reshape(1, ROUTER_LANES)
    h, ids, wts = moe_router(x, gain, jnp.concatenate([w_hi, w_lo], axis=1), b_router)
    tm = min(COMBINE_ROWS, n)
    blk_e, n_used, seg, tok, rows = _expert_row_index(ids[:, :TOP_K_IN_GROUP], EXPERT_ROWS, tm)
    y = expert_mlp(h, blk_e, n_used, seg, tok, w_gate, w_up, w_down, layer)
    return moe_combine(x, y, rows, wts, tm=tm)


def _rope_tables(t):
    inv_freq = ROPE_THETA ** (-jnp.arange(0, HEAD_DIM, 2, dtype=F32) / HEAD_DIM)
    ang = jnp.arange(t, dtype=F32)[:, None] * inv_freq[None, :]
    cos, sin = jnp.cos(ang), jnp.sin(ang)
    return jnp.concatenate([cos, cos], axis=1), jnp.concatenate([-sin, sin], axis=1)


def kernel(x, norm_mix, norm_ffn, a_w_in, a_lb_logits, a_o_gain, a_w_out, kv_norm, kv_w, k_norm,
           b_w_q, b_q_norm, b_w_out, moe_w_group, moe_b_group, moe_w_expert, moe_b_expert,
           moe_w_gate, moe_w_up, moe_w_down):
    b, t, d = x.shape
    n = b * t
    depth = norm_mix.shape[0]
    n_a = a_w_in.shape[0]
    lower_bounds = jnp.cumsum(jax.nn.softmax(a_lb_logits.astype(F32), axis=0), axis=0)
    cos, sin = _rope_tables(t)
    xf = x.reshape(n, d)
    kv = None
    for l in range(depth):
        if l < n_a:
            z = rms_matmul(xf, norm_mix[l], a_w_in.astype(BF16), l, F32)
            o = hgrn2_scan(z.reshape(b, t, 4 * d), lower_bounds[l], a_o_gain[l])
            xf = matmul_residual(o.reshape(n, d), a_w_out.astype(BF16), l, xf)
        else:
            if l == n_a:
                kv = rms_matmul(xf, kv_norm, kv_w[None].astype(BF16), 0, BF16,
                                rope=(k_norm, cos, sin, d), seq_len=t)
            lb_ = l - n_a
            q = rms_matmul(xf, norm_mix[l], b_w_q.astype(BF16), lb_, BF16,
                           rope=(b_q_norm[lb_], cos, sin, d), seq_len=t)
            o = moba_attention(q.reshape(b, t, d), kv.reshape(b, t, 2 * d))
            xf = matmul_residual(o.reshape(n, d), b_w_out.astype(BF16), lb_, xf)
        xf = hier_moe(xf, norm_ffn[l], moe_w_group[l], moe_b_group[l], moe_w_expert[l],
                      moe_b_expert[l], moe_w_gate, moe_w_up, moe_w_down, l)
    return xf.reshape(b, t, d)
```

```python
import functools

import jax
import jax.numpy as jnp
from jax import lax
from jax.experimental import pallas as pl
from jax.experimental.pallas import tpu as pltpu

HEAD_DIM = 128
HGRN_CHUNK = 64
HGRN_SUB = 8
HGRN_UNROLL = 8
MOBA_BLOCK = 256
MOBA_TOP_K = 3
ROPE_THETA = 10000.0
N_GROUPS = 4
EXPERTS_PER_GROUP = 8
N_EXPERTS = N_GROUPS * EXPERTS_PER_GROUP
TOP_K_IN_GROUP = 2
RMS_EPS = 1e-6
ROUTER_LANES = 128
EXPERT_ROWS = 256
COMBINE_ROWS = 256
DMA_THREADS = 2
ROPE_ROW_PARTS = 4

V7X_VMEM_LIMIT = 56 * 1024 * 1024

F32 = jnp.float32
BF16 = jnp.bfloat16
HIGHEST = lax.Precision.HIGHEST
NT_DIMS = (((1,), (1,)), ((), ()))


def _params(*sem):
    return pltpu.CompilerParams(dimension_semantics=sem, vmem_limit_bytes=V7X_VMEM_LIMIT)


def _rms_rows_to(x_ref, g_ref, out_ref, rows, chunk):
    gain = g_ref[...]

    def body(c, carry):
        sl = pl.ds(pl.multiple_of(c * chunk, chunk), chunk)
        x = x_ref[sl, :]
        ms = jnp.mean(x * x, axis=-1, keepdims=True)
        out_ref[sl, :] = (x * lax.rsqrt(ms + RMS_EPS) * gain).astype(out_ref.dtype)
        return carry

    lax.fori_loop(0, rows // chunk, body, 0)


def _head_norm_rope(y, hg, cos, sin):
    ms = jnp.mean(y * y, axis=-1, keepdims=True)
    yn = y * lax.rsqrt(ms + RMS_EPS) * hg
    return yn * cos + pltpu.roll(yn, HEAD_DIM // 2, axis=1) * sin


def _rms_matmul_kernel(*refs, tm, tn, n_rope_tiles):
    if n_rope_tiles:
        x_ref, g_ref, w_ref, hg_ref, cos_ref, sin_ref, o_ref, hn_ref = refs
    else:
        x_ref, g_ref, w_ref, o_ref, hn_ref = refs
    j = pl.program_id(1)

    @pl.when(j == 0)
    def _():
        _rms_rows_to(x_ref, g_ref, hn_ref, tm, 64)

    def plain():
        o_ref[...] = jnp.dot(hn_ref[...], w_ref[...],
                             preferred_element_type=F32).astype(o_ref.dtype)

    if not n_rope_tiles:
        plain()
        return

    @pl.when(j < n_rope_tiles)
    def _():
        w = w_ref[...]
        hg = hg_ref[...]
        part = tm // ROPE_ROW_PARTS
        for p in range(ROPE_ROW_PARTS):
            rows = slice(p * part, (p + 1) * part)
            y = jnp.dot(hn_ref[rows, :], w, preferred_element_type=F32)
            for h in range(tn // HEAD_DIM):
                sl = slice(h * HEAD_DIM, (h + 1) * HEAD_DIM)
                o_ref[rows, sl] = _head_norm_rope(y[:, sl], hg, cos_ref[rows, :],
                                                  sin_ref[rows, :]).astype(o_ref.dtype)

    pl.when(j >= n_rope_tiles)(plain)


def rms_matmul(x, gain, w, layer, out_dtype, *, rope=None, seq_len=None, tm=1024, tn=1024):
    n, d = x.shape
    e = w.shape[2]
    tm = min(tm, n)
    tn = min(tn, e)
    in_specs = [
        pl.BlockSpec((tm, d), lambda i, j: (i, 0)),
        pl.BlockSpec((1, d), lambda i, j: (0, 0)),
        pl.BlockSpec((None, d, tn), lambda i, j: (layer, 0, j)),
    ]
    args = [x, gain.reshape(1, d), w]
    n_rope_tiles = 0
    if rope is not None:
        hg, cos, sin, n_rope_cols = rope
        tm = min(tm, seq_len)
        tn = min(tn, n_rope_cols)
        in_specs[0] = pl.BlockSpec((tm, d), lambda i, j: (i, 0))
        in_specs[2] = pl.BlockSpec((None, d, tn), lambda i, j: (layer, 0, j))
        n_rope_tiles = n_rope_cols // tn
        pos_tiles = seq_len // tm
        in_specs += [
            pl.BlockSpec((1, HEAD_DIM), lambda i, j: (0, 0)),
            pl.BlockSpec((tm, HEAD_DIM), lambda i, j: (i % pos_tiles, 0)),
            pl.BlockSpec((tm, HEAD_DIM), lambda i, j: (i % pos_tiles, 0)),
        ]
        args += [hg.reshape(1, HEAD_DIM), cos, sin]
    return pl.pallas_call(
        functools.partial(_rms_matmul_kernel, tm=tm, tn=tn, n_rope_tiles=n_rope_tiles),
        grid=(n // tm, e // tn),
        in_specs=in_specs,
        out_specs=pl.BlockSpec((tm, tn), lambda i, j: (i, j)),
        out_shape=jax.ShapeDtypeStruct((n, e), out_dtype),
        scratch_shapes=[pltpu.VMEM((tm, d), BF16)],
        compiler_params=_params("arbitrary", "arbitrary"),
        name="rms_matmul",
    )(*args)


def _matmul_res_kernel(a_ref, w_ref, r_ref, o_ref):
    o_ref[...] = r_ref[...] + jnp.dot(a_ref[...], w_ref[...],
                                      preferred_element_type=F32)


def matmul_residual(a, w, layer, res, *, tm=1024, tn=1024):
    n, d = a.shape
    e = w.shape[2]
    tm = min(tm, n)
    tn = min(tn, e)
    return pl.pallas_call(
        _matmul_res_kernel,
        grid=(n // tm, e // tn),
        in_specs=[
            pl.BlockSpec((tm, d), lambda i, j: (i, 0)),
            pl.BlockSpec((None, d, tn), lambda i, j: (layer, 0, j)),
            pl.BlockSpec((tm, tn), lambda i, j: (i, j)),
        ],
        out_specs=pl.BlockSpec((tm, tn), lambda i, j: (i, j)),
        out_shape=jax.ShapeDtypeStruct((n, e), F32),
        compiler_params=_params("arbitrary", "arbitrary"),
        name="matmul_residual",
    )(a, w, res)


def _hgrn_kernel(zq_ref, zf_ref, zi_ref, zg_ref, lb_ref, og_ref, o_ref,
                 st_ref, g_scr, h_scr, a_scr, u_scr, *, n_groups):
    C = HGRN_CHUNK
    SB = HGRN_SUB
    NB = C // SB

    @pl.when(pl.program_id(2) == 0)
    def _():
        st_ref[...] = jnp.zeros_like(st_ref)

    lb = lb_ref[...]
    og = og_ref[...]
    row_c = lax.broadcasted_iota(jnp.int32, (C, C), 0)
    col_c = lax.broadcasted_iota(jnp.int32, (C, C), 1)
    causal = row_c >= col_c
    tril = causal.astype(F32)
    lane_s = lax.broadcasted_iota(jnp.int32, (SB, C), 1)
    zero_blk = jnp.zeros((SB, HEAD_DIM), F32)
    halves = [h for h in (32, 16, 8) if SB <= h < C]
    level_mask = {half: ((row_c // (2 * half) == col_c // (2 * half))
                         & ((row_c // half) % 2 == 1) & ((col_c // half) % 2 == 0))
                  for half in halves}

    def blk(x, b):
        return x[b * SB:(b + 1) * SB, :]

    def gates(rows, u):
        f = lb + (1.0 - lb) * jax.nn.sigmoid(zf_ref[0, rows, :])
        G2 = jnp.dot(tril, jnp.log2(f), precision=HIGHEST, preferred_element_type=F32)
        g_scr[u] = G2
        h_scr[u] = G2 - jnp.log2(1.0 - f)

    def scores(rows, u):
        q = zq_ref[0, rows, :]
        G2 = g_scr[u]
        H2 = h_scr[u]
        A = jnp.zeros((C, C), F32)
        for half in halves:
            per_half = half // SB
            qa, kb = [], []
            for b in range(NB):
                grp = b // per_half
                if grp % 2 == 1:
                    r = grp * half - 1
                    qa.append(blk(q, b) * jnp.exp2(blk(G2, b) - g_scr[u, r:r + 1, :]))
                    kb.append(zero_blk)
                else:
                    r = (grp + 1) * half - 1
                    kb.append(jnp.exp2(g_scr[u, r:r + 1, :] - blk(H2, b)))
                    qa.append(zero_blk)
            sq = lax.dot_general(jnp.concatenate(qa, axis=0).astype(BF16),
                                 jnp.concatenate(kb, axis=0).astype(BF16), NT_DIMS,
                                 preferred_element_type=F32)
            A = A + jnp.where(level_mask[half], sq, 0.0)
        a_parts = []
        for b in range(NB):
            Gt = blk(G2, b)
            qt = blk(q, b)
            a_b = jnp.zeros((SB, C), F32)
            for s in range(b * SB, (b + 1) * SB):
                e = jnp.exp2(Gt - h_scr[u, s:s + 1, :])
                a_b = jnp.where(lane_s == s, jnp.sum(qt * e, axis=-1, keepdims=True), a_b)
            a_parts.append(a_b)
        A = jnp.where(causal, A + jnp.concatenate(a_parts, axis=0), 0.0)
        a_scr[u] = A.astype(BF16)

    def state_input(rows, u):
        G2_end = g_scr[u, C - 1:C, :]
        kd = jnp.exp2(G2_end - h_scr[u])
        u_scr[u] = jnp.dot(zi_ref[0, rows, :].T.astype(BF16), kd.astype(BF16),
                           preferred_element_type=F32)

    def output(rows, u):
        G2 = g_scr[u]
        st = st_ref[...]
        qg = (zq_ref[0, rows, :] * jnp.exp2(G2)).astype(BF16)
        o = (lax.dot_general(qg, st.astype(BF16), NT_DIMS, preferred_element_type=F32)
             + jnp.dot(a_scr[u], zi_ref[0, rows, :].astype(BF16), preferred_element_type=F32))
        st_ref[...] = jnp.exp2(G2[C - 1:C, :]) * st + u_scr[u]
        ms = jnp.mean(o * o, axis=-1, keepdims=True)
        zg = zg_ref[0, rows, :]
        o_ref[0, rows, :] = (o * lax.rsqrt(ms + RMS_EPS) * og * jax.nn.silu(zg)).astype(o_ref.dtype)

    def group(gi, carry):
        base = pl.multiple_of(gi * (HGRN_UNROLL * C), HGRN_UNROLL * C)
        for phase in (gates, scores, state_input, output):
            for u in range(HGRN_UNROLL):
                phase(pl.ds(base + u * C, C), u)
        return carry

    lax.fori_loop(0, n_groups, group, 0)


def hgrn2_scan(z, lb, o_gain, *, tt=4096):
    b, t, d4 = z.shape
    d = d4 // 4
    nh = d // HEAD_DIM
    tt = min(tt, t)

    def zspec(part):
        return pl.BlockSpec((1, tt, HEAD_DIM), lambda bi, h, ti: (bi, ti, part * nh + h))

    return pl.pallas_call(
        functools.partial(_hgrn_kernel, n_groups=tt // (HGRN_CHUNK * HGRN_UNROLL)),
        grid=(b, nh, t // tt),
        in_specs=[zspec(0), zspec(1), zspec(2), zspec(3),
                  pl.BlockSpec((1, HEAD_DIM), lambda bi, h, ti: (0, h)),
                  pl.BlockSpec((1, HEAD_DIM), lambda bi, h, ti: (0, 0))],
        out_specs=pl.BlockSpec((1, tt, HEAD_DIM), lambda bi, h, ti: (bi, ti, h)),
        out_shape=jax.ShapeDtypeStruct((b, t, d), BF16),
        scratch_shapes=[pltpu.VMEM((HEAD_DIM, HEAD_DIM), F32),
                        pltpu.VMEM((HGRN_UNROLL, HGRN_CHUNK, HEAD_DIM), F32),
                        pltpu.VMEM((HGRN_UNROLL, HGRN_CHUNK, HEAD_DIM), F32),
                        pltpu.VMEM((HGRN_UNROLL, HGRN_CHUNK, HGRN_CHUNK), BF16),
                        pltpu.VMEM((HGRN_UNROLL, HEAD_DIM, HEAD_DIM), F32)],
        compiler_params=_params("arbitrary", "arbitrary", "arbitrary"),
        name="hgrn2_scan",
    )(z, z, z, z, lb.reshape(1, d), o_gain.reshape(1, HEAD_DIM))


def _moba_kernel(q_ref, k_ref, v_ref, o_ref, km_ref, vt_ref, qt_ref, bias_ref, m_ref, l_ref, acc_ref,
                 *, n_blocks, heads):
    BLK = MOBA_BLOCK
    qi = pl.program_id(2)
    exp2_scale = HEAD_DIM ** -0.5 * 1.4426950408889634

    def head(h):
        return slice(h * HEAD_DIM, (h + 1) * HEAD_DIM)

    eye = (lax.broadcasted_iota(jnp.int32, (HEAD_DIM, HEAD_DIM), 0)
           == lax.broadcasted_iota(jnp.int32, (HEAD_DIM, HEAD_DIM), 1)).astype(BF16)

    @pl.when(qi == 0)
    def _():
        for h in range(heads):
            for j in range(n_blocks):
                rows = slice(j * BLK, (j + 1) * BLK)
                kb = k_ref[0, rows, head(h)].astype(F32)
                km_ref[h, j:j + 1, :] = jnp.mean(kb, axis=0, keepdims=True)
                vt_ref[h, :, rows] = lax.dot_general(eye, v_ref[0, rows, head(h)], NT_DIMS,
                                                     preferred_element_type=F32).astype(BF16)

    blk_id = lax.broadcasted_iota(jnp.int32, (n_blocks, BLK), 0)
    past = blk_id < qi
    key_i = lax.broadcasted_iota(jnp.int32, (BLK, BLK), 0)
    qry_i = lax.broadcasted_iota(jnp.int32, (BLK, BLK), 1)
    own = pl.ds(pl.multiple_of(qi * BLK, BLK), BLK)

    def scores(rows):
        return [jnp.dot(k_ref[0, rows, head(h)], qt_ref[h], preferred_element_type=F32)
                for h in range(heads)]

    for h in range(heads):
        qt_ref[h] = lax.dot_general(eye, q_ref[0, :, head(h)], NT_DIMS,
                                    preferred_element_type=F32).astype(BF16)
    for h in range(heads):
        gate = jnp.dot(km_ref[h], qt_ref[h].astype(F32), precision=HIGHEST,
                       preferred_element_type=F32)
        gm = jnp.where(past, gate, -jnp.inf)
        cnt = jnp.zeros((n_blocks, BLK), jnp.int32)
        for m in range(n_blocks):
            gmm = gm[m:m + 1, :]
            ahead = (gmm > gm) | ((gmm == gm) & (blk_id > m))
            cnt = cnt + ahead.astype(jnp.int32)
        bias_ref[h] = jnp.where(past & (cnt < MOBA_TOP_K), 0.0, -jnp.inf).astype(F32)

    probs = []
    for h, s in enumerate(scores(own)):
        s = jnp.where(key_i <= qry_i, s, -jnp.inf)
        m0 = jnp.max(s, axis=0, keepdims=True)
        p = jnp.exp2((s - m0) * exp2_scale)
        m_ref[h] = m0
        l_ref[h] = jnp.sum(p, axis=0, keepdims=True)
        probs.append(p.astype(BF16))
    for h in range(heads):
        acc_ref[h] = jnp.dot(vt_ref[h, :, own], probs[h], preferred_element_type=F32)

    def past_blocks(js):
        rows = [slice(j * BLK, (j + 1) * BLK) for j in js]
        s_all = [scores(r) for r in rows]
        for j, r, s_heads in zip(js, rows, s_all):
            probs, alphas = [], []
            for h, sj in enumerate(s_heads):
                sj = sj + bias_ref[h, j:j + 1, :]
                m_old = m_ref[h]
                m_new = jnp.maximum(m_old, jnp.max(sj, axis=0, keepdims=True))
                alpha = jnp.exp2((m_old - m_new) * exp2_scale)
                pj = jnp.exp2((sj - m_new) * exp2_scale)
                m_ref[h] = m_new
                l_ref[h] = alpha * l_ref[h] + jnp.sum(pj, axis=0, keepdims=True)
                probs.append(pj.astype(BF16))
                alphas.append(alpha)
            for h in range(heads):
                acc_ref[h] = alphas[h] * acc_ref[h] + jnp.dot(vt_ref[h, :, r], probs[h],
                                                              preferred_element_type=F32)

    for j in range(0, n_blocks - 1, 2):
        if j + 1 < n_blocks - 1:
            @pl.when(j + 1 < qi)
            def _(j=j):
                past_blocks((j, j + 1))

        @pl.when(qi == j + 1)
        def _(j=j):
            past_blocks((j,))

    for h in range(heads):
        o_ref[0, :, head(h)] = (acc_ref[h] / l_ref[h]).T.astype(o_ref.dtype)


def moba_attention(q, kv, *, heads=4):
    b, t, d = q.shape
    nh = d // HEAD_DIM
    heads = min(heads, nh)
    ng = nh // heads
    nb = t // MOBA_BLOCK
    w = heads * HEAD_DIM
    return pl.pallas_call(
        functools.partial(_moba_kernel, n_blocks=nb, heads=heads),
        grid=(b, ng, nb),
        in_specs=[pl.BlockSpec((1, MOBA_BLOCK, w), lambda bi, g, qi: (bi, qi, g)),
                  pl.BlockSpec((1, t, w), lambda bi, g, qi: (bi, 0, g)),
                  pl.BlockSpec((1, t, w), lambda bi, g, qi: (bi, 0, ng + g))],
        out_specs=pl.BlockSpec((1, MOBA_BLOCK, w), lambda bi, g, qi: (bi, qi, g)),
        out_shape=jax.ShapeDtypeStruct((b, t, d), BF16),
        scratch_shapes=[pltpu.VMEM((heads, nb, HEAD_DIM), F32),
                        pltpu.VMEM((heads, HEAD_DIM, t), BF16),
                        pltpu.VMEM((heads, HEAD_DIM, MOBA_BLOCK), BF16),
                        pltpu.VMEM((heads, nb, MOBA_BLOCK), F32),
                        pltpu.VMEM((heads, 1, MOBA_BLOCK), F32),
                        pltpu.VMEM((heads, 1, MOBA_BLOCK), F32),
                        pltpu.VMEM((heads, HEAD_DIM, MOBA_BLOCK), F32)],
        compiler_params=_params("arbitrary", "arbitrary", "arbitrary"),
        name="moba_attention",
    )(q, kv, kv)


def _router_kernel(x_ref, g_ref, wr_ref, br_ref, h_ref, ids_ref, wts_ref, *, tm):
    _rms_rows_to(x_ref, g_ref, h_ref, tm, 64)
    h = h_ref[...]
    hi = h.astype(BF16)
    lo = (h - hi.astype(F32)).astype(BF16)
    w = wr_ref[...]
    hw = jnp.dot(hi, w, preferred_element_type=F32)
    logits = (hw[:, :ROUTER_LANES] + hw[:, ROUTER_LANES:]
              + jnp.dot(lo, w[:, :ROUTER_LANES], preferred_element_type=F32)
              + br_ref[...])
    lane = lax.broadcasted_iota(jnp.int32, logits.shape, 1)
    neg = -jnp.inf
    big = ROUTER_LANES

    gl = jnp.where(lane < N_GROUPS, logits, neg)
    gmax = jnp.max(gl, axis=-1, keepdims=True)
    gsum = jnp.sum(jnp.exp(gl - gmax), axis=-1, keepdims=True)
    g_top_p = 1.0 / gsum
    g_top = jnp.min(jnp.where(gl == gmax, lane, big), axis=-1, keepdims=True)

    lo = N_GROUPS + EXPERTS_PER_GROUP * g_top
    el = jnp.where((lane >= lo) & (lane < lo + EXPERTS_PER_GROUP), logits, neg)
    emax = jnp.max(el, axis=-1, keepdims=True)
    esum = jnp.sum(jnp.exp(el - emax), axis=-1, keepdims=True)
    i1 = jnp.min(jnp.where(el == emax, lane, big), axis=-1, keepdims=True)
    el2 = jnp.where(lane == i1, neg, el)
    emax2 = jnp.max(el2, axis=-1, keepdims=True)
    i2 = jnp.min(jnp.where(el2 == emax2, lane, big), axis=-1, keepdims=True)
    p1 = 1.0 / esum
    p2 = jnp.exp(emax2 - emax) / esum
    psum = p1 + p2
    w1 = g_top_p * p1 / psum
    w2 = g_top_p * p2 / psum

    ids_ref[...] = jnp.where(lane == 0, i1 - N_GROUPS, jnp.where(lane == 1, i2 - N_GROUPS, 0))
    wts_ref[...] = jnp.where(lane == 0, w1, jnp.where(lane == 1, w2, 0.0))


def moe_router(x, gain, w_router, b_router, *, tm=512):
    n, d = x.shape
    tm = min(tm, n)
    return pl.pallas_call(
        functools.partial(_router_kernel, tm=tm),
        grid=(n // tm,),
        in_specs=[pl.BlockSpec((tm, d), lambda i: (i, 0)),
                  pl.BlockSpec((1, d), lambda i: (0, 0)),
                  pl.BlockSpec((d, 2 * ROUTER_LANES), lambda i: (0, 0)),
                  pl.BlockSpec((1, ROUTER_LANES), lambda i: (0, 0))],
        out_specs=[pl.BlockSpec((tm, d), lambda i: (i, 0)),
                   pl.BlockSpec((tm, ROUTER_LANES), lambda i: (i, 0)),
                   pl.BlockSpec((tm, ROUTER_LANES), lambda i: (i, 0))],
        out_shape=[jax.ShapeDtypeStruct((n, d), F32),
                   jax.ShapeDtypeStruct((n, ROUTER_LANES), jnp.int32),
                   jax.ShapeDtypeStruct((n, ROUTER_LANES), F32)],
        compiler_params=_params("arbitrary"),
        name="moe_router",
    )(x, gain.reshape(1, d), w_router, b_router)


def _dispatch_kernel(n_used_ref, tok0_ref, tok_ref, h_hbm, o_ref, buf0, buf1, sem, *, rows):
    i = pl.program_id(0)
    n_used = n_used_ref[0]
    bufs = (buf0, buf1)

    def gather(idx_ref, slot):
        for r in range(rows):
            pltpu.make_async_copy(h_hbm.at[pl.ds(idx_ref[0, 0, r], 1), :],
                                  bufs[slot].at[pl.ds(r, 1), :],
                                  sem.at[slot]).start(priority=r % DMA_THREADS)

    def wait_gather(slot):
        pltpu.make_async_copy(h_hbm.at[pl.ds(0, rows), :], bufs[slot], sem.at[slot]).wait()

    @pl.when(i == 0)
    def _():
        gather(tok0_ref, 0)

    def step(slot):
        wait_gather(slot)
        gather(tok_ref, 1 - slot)
        o_ref[...] = bufs[slot][...]

    for slot in range(2):
        @pl.when((i < n_used) & (i % 2 == slot))
        def _(slot=slot):
            step(slot)

    @pl.when(i == n_used - 1)
    def _():
        for slot in range(2):
            @pl.when(i % 2 == slot)
            def _(slot=slot):
                wait_gather(1 - slot)

    @pl.when(i >= n_used)
    def _():
        o_ref[...] = jnp.zeros_like(o_ref)


def moe_dispatch(h, n_used, tok):
    n, d = h.shape
    n_blk, r = tok.shape[0] - 1, tok.shape[2]

    def smem_rows(index_map):
        return pl.BlockSpec((1, 1, r), index_map, memory_space=pltpu.SMEM)

    grid_spec = pltpu.PrefetchScalarGridSpec(
        num_scalar_prefetch=1,
        grid=(n_blk,),
        in_specs=[smem_rows(lambda i, nu: (0, 0, 0)),
                  smem_rows(lambda i, nu: (i + 1, 0, 0)),
                  pl.BlockSpec(memory_space=pl.ANY)],
        out_specs=pl.BlockSpec((r, d), lambda i, nu: (i, 0)),
        scratch_shapes=[pltpu.VMEM((r, d), F32), pltpu.VMEM((r, d), F32),
                        pltpu.SemaphoreType.DMA((2,))],
    )
    return pl.pallas_call(
        functools.partial(_dispatch_kernel, rows=r),
        grid_spec=grid_spec,
        out_shape=jax.ShapeDtypeStruct((n_blk * r, d), F32),
        compiler_params=_params("arbitrary"),
        name="moe_dispatch",
    )(n_used, tok, tok, h)


def _expert_kernel(blk_e_ref, n_used_ref, seg_ref, x_ref, wg_hbm, wu_hbm, wd_hbm, o_ref,
                   wgf, wuf, wdf, wgb, wub, wdb, sem_w, *, layer):
    i = pl.program_id(0)
    n_used = n_used_ref[0]

    def weight_copies(e, slot):
        return [pltpu.make_async_copy(w_hbm.at[layer, e], buf.at[slot], sem_w.at[slot])
                for w_hbm, buf in ((wg_hbm, wgf), (wu_hbm, wuf), (wd_hbm, wdf))]

    @pl.when(i == 0)
    def _():
        for cp in weight_copies(blk_e_ref[0], 0):
            cp.start()

    @pl.when((i < n_used) & (seg_ref[0, i] == 1))
    def _():
        slot = seg_ref[1, i]
        for cp in weight_copies(blk_e_ref[i], slot):
            cp.wait()
        wgb[...] = wgf[slot].astype(BF16)
        wub[...] = wuf[slot].astype(BF16)
        wdb[...] = wdf[slot].astype(BF16)

        @pl.when(seg_ref[2, i] == 1)
        def _():
            for cp in weight_copies(seg_ref[3, i], 1 - slot):
                cp.start()

    @pl.when(i < n_used)
    def _():
        x = x_ref[...].astype(BF16)
        g = jnp.dot(x, wgb[...], preferred_element_type=F32)
        u = jnp.dot(x, wub[...], preferred_element_type=F32)
        mid = (jax.nn.silu(g) * u).astype(BF16)
        o_ref[...] = jnp.dot(mid, wdb[...], preferred_element_type=F32)

    @pl.when(i >= n_used)
    def _():
        o_ref[...] = jnp.zeros_like(o_ref)


def expert_mlp(xs, blk_e, n_used, seg, w_gate, w_up, w_down, layer, *, rows):
    p, d = xs.shape
    r = rows
    n_blk = p // r
    ff = w_gate.shape[3]
    grid_spec = pltpu.PrefetchScalarGridSpec(
        num_scalar_prefetch=3,
        grid=(n_blk,),
        in_specs=[
            pl.BlockSpec((r, d), lambda i, be, nu, sg: (jnp.minimum(i, nu[0] - 1), 0)),
            pl.BlockSpec(memory_space=pl.ANY),
            pl.BlockSpec(memory_space=pl.ANY),
            pl.BlockSpec(memory_space=pl.ANY),
        ],
        out_specs=pl.BlockSpec((r, d), lambda i, be, nu, sg: (i, 0)),
        scratch_shapes=[
            pltpu.VMEM((2, d, ff), F32), pltpu.VMEM((2, d, ff), F32), pltpu.VMEM((2, ff, d), F32),
            pltpu.VMEM((d, ff), BF16),
            pltpu.VMEM((d, ff), BF16),
            pltpu.VMEM((ff, d), BF16),
            pltpu.SemaphoreType.DMA((2,)),
        ],
    )
    return pl.pallas_call(
        functools.partial(_expert_kernel, layer=layer),
        grid_spec=grid_spec,
        out_shape=jax.ShapeDtypeStruct((n_blk * r, d), F32),
        compiler_params=_params("arbitrary"),
        name="expert_mlp",
    )(blk_e, n_used, seg, xs, w_gate, w_up, w_down)


def _combine_kernel(row0_ref, row_ref, x_ref, w_ref, y_hbm, o_ref, ybuf0, ybuf1, sem, *, tm, n_tiles):
    i = pl.program_id(0)
    ybufs = (ybuf0, ybuf1)

    def gather(idx_ref, slot):
        for r in range(TOP_K_IN_GROUP * tm):
            pltpu.make_async_copy(y_hbm.at[pl.ds(idx_ref[0, 0, r], 1), :],
                                  ybufs[slot].at[pl.ds(r, 1), :],
                                  sem.at[slot]).start(priority=r % DMA_THREADS)

    def wait_gather(slot):
        pltpu.make_async_copy(y_hbm.at[pl.ds(0, TOP_K_IN_GROUP * tm), :], ybufs[slot],
                              sem.at[slot]).wait()

    @pl.when(i == 0)
    def _():
        gather(row0_ref, 0)

    def step(slot):
        wait_gather(slot)
        gather(row_ref, 1 - slot)
        w = w_ref[...]
        o_ref[...] = x_ref[...] + (w[:, 0:1] * ybufs[slot][0:tm, :]
                                   + w[:, 1:2] * ybufs[slot][tm:2 * tm, :])

    for slot in range(2):
        @pl.when(i % 2 == slot)
        def _(slot=slot):
            step(slot)

    @pl.when(i == n_tiles - 1)
    def _():
        wait_gather(n_tiles % 2)


def moe_combine(x, y, rows, wts, *, tm=COMBINE_ROWS):
    n, d = x.shape
    nt = n // tm

    def smem_rows(index_map):
        return pl.BlockSpec((1, 1, TOP_K_IN_GROUP * tm), index_map, memory_space=pltpu.SMEM)

    return pl.pallas_call(
        functools.partial(_combine_kernel, tm=tm, n_tiles=nt),
        grid=(nt,),
        in_specs=[smem_rows(lambda i: (0, 0, 0)),
                  smem_rows(lambda i: (jnp.minimum(i + 1, nt - 1), 0, 0)),
                  pl.BlockSpec((tm, d), lambda i: (i, 0)),
                  pl.BlockSpec((tm, ROUTER_LANES), lambda i: (i, 0)),
                  pl.BlockSpec(memory_space=pl.ANY)],
        out_specs=pl.BlockSpec((tm, d), lambda i: (i, 0)),
        out_shape=jax.ShapeDtypeStruct((n, d), F32),
        scratch_shapes=[pltpu.VMEM((TOP_K_IN_GROUP * tm, d), F32),
                        pltpu.VMEM((TOP_K_IN_GROUP * tm, d), F32),
                        pltpu.SemaphoreType.DMA((2,))],
        compiler_params=_params("arbitrary"),
        name="moe_combine",
    )(rows, rows, x, wts, y)


def _expert_row_index(expert_id, rows_per_block, combine_rows):
    n, k = expert_id.shape
    a_total = n * k
    r = rows_per_block
    flat_e = expert_id.reshape(-1)
    order = jnp.argsort(flat_e).astype(jnp.int32)
    rank = jnp.argsort(order).astype(jnp.int32)
    experts = jnp.arange(N_EXPERTS, dtype=jnp.int32)
    onehot = experts[:, None] == flat_e[None, :]
    counts = jnp.sum(onehot.astype(jnp.int32), axis=1)
    padded = (counts + r - 1) // r * r
    pad_end = jnp.cumsum(padded)
    pad_start = pad_end - padded
    start = jnp.cumsum(counts) - counts
    n_blk = (a_total + N_EXPERTS * (r - 1) + r - 1) // r
    n_used = (pad_end[-1] // r).astype(jnp.int32)
    blk = jnp.arange(n_blk, dtype=jnp.int32)
    blk_row0 = blk * r
    blk_e = jnp.minimum(jnp.sum((pad_end[None, :] <= blk_row0[:, None]).astype(jnp.int32), axis=1),
                        N_EXPERTS - 1).astype(jnp.int32)
    in_e = blk_row0 - pad_start[blk_e]
    nvalid = jnp.clip(counts[blk_e] - in_e, 0, r)
    lane = jnp.arange(r, dtype=jnp.int32)[None, :]
    valid = (lane < nvalid[:, None]) & (blk[:, None] < n_used)
    idx = (start[blk_e] + in_e)[:, None] + lane
    src = order[jnp.clip(idx, 0, a_total - 1)]
    tok = jnp.where(valid, src // k, 0)
    tok = jnp.concatenate([tok, jnp.zeros((1, r), jnp.int32)], axis=0)
    shift = jnp.sum(jnp.where(onehot, (pad_start - start)[:, None], 0), axis=0)
    row = (rank + shift).astype(jnp.int32).reshape(n, k)
    row = row.reshape(n // combine_rows, combine_rows, k).transpose(0, 2, 1)
    used = blk < n_used
    first = used & ((blk == 0) | (blk_e != jnp.roll(blk_e, 1)))
    parity = (jnp.cumsum(first.astype(jnp.int32)) - 1) % 2
    next_blk = pad_end[blk_e] // r
    has_next = used & (next_blk < n_used)
    next_e = blk_e[jnp.minimum(next_blk, n_blk - 1)]
    seg = jnp.stack([first.astype(jnp.int32), parity, has_next.astype(jnp.int32), next_e])
    return (blk_e, n_used.reshape(1), seg.astype(jnp.int32), tok.reshape(n_blk + 1, 1, r),
            row.reshape(n // combine_rows, 1, k * combine_rows))


def hier_moe(x, gain, w_group, b_group, w_expert, b_expert, w_gate, w_up, w_down, layer):
    n, d = x.shape
    pad = ROUTER_LANES - N_GROUPS - N_EXPERTS
    w_router = jnp.concatenate([w_group, w_expert, jnp.zeros((d, pad), F32)], axis=1)
    w_hi = w_router.astype(BF16)
    w_lo = (w_router - w_hi.astype(F32)).astype(BF16)
    b_router = jnp.concatenate([b_group, b_expert, jnp.zeros((pad,), F32)]).reshape(1, ROUTER_LANES)
    h, ids, wts = moe_router(x, gain, jnp.concatenate([w_hi, w_lo], axis=1), b_router)
    tm = min(COMBINE_ROWS, n)
    blk_e, n_used, seg, tok, rows = _expert_row_index(ids[:, :TOP_K_IN_GROUP], EXPERT_ROWS, tm)
    xs = moe_dispatch(h, n_used, tok)
    y = expert_mlp(xs, blk_e, n_used, seg, w_gate, w_up, w_down, layer, rows=EXPERT_ROWS)
    return moe_combine(x, y, rows, wts, tm=tm)


def _rope_tables(t):
    inv_freq = ROPE_THETA ** (-jnp.arange(0, HEAD_DIM, 2, dtype=F32) / HEAD_DIM)
    ang = jnp.arange(t, dtype=F32)[:, None] * inv_freq[None, :]
    cos, sin = jnp.cos(ang), jnp.sin(ang)
    return jnp.concatenate([cos, cos], axis=1), jnp.concatenate([-sin, sin], axis=1)


def kernel(x, norm_mix, norm_ffn, a_w_in, a_lb_logits, a_o_gain, a_w_out, kv_norm, kv_w, k_norm,
           b_w_q, b_q_norm, b_w_out, moe_w_group, moe_b_group, moe_w_expert, moe_b_expert,
           moe_w_gate, moe_w_up, moe_w_down):
    b, t, d = x.shape
    n = b * t
    depth = norm_mix.shape[0]
    n_a = a_w_in.shape[0]
    lower_bounds = jnp.cumsum(jax.nn.softmax(a_lb_logits.astype(F32), axis=0), axis=0)
    cos, sin = _rope_tables(t)
    xf = x.reshape(n, d)
    kv = None
    for l in range(depth):
        if l < n_a:
            z = rms_matmul(xf, norm_mix[l], a_w_in.astype(BF16), l, F32)
            o = hgrn2_scan(z.reshape(b, t, 4 * d), lower_bounds[l], a_o_gain[l])
            xf = matmul_residual(o.reshape(n, d), a_w_out.astype(BF16), l, xf)
        else:
            if l == n_a:
                kv = rms_matmul(xf, kv_norm, kv_w[None].astype(BF16), 0, BF16,
                                rope=(k_norm, cos, sin, d), seq_len=t)
            lb_ = l - n_a
            q = rms_matmul(xf, norm_mix[l], b_w_q.astype(BF16), lb_, BF16,
                           rope=(b_q_norm[lb_], cos, sin, d), seq_len=t)
            o = moba_attention(q.reshape(b, t, d), kv.reshape(b, t, 2 * d))
            xf = matmul_residual(o.reshape(n, d), b_w_out.astype(BF16), lb_, xf)
        xf = hier_moe(xf, norm_ffn[l], moe_w_group[l], moe_b_group[l], moe_w_expert[l],
                      moe_b_expert[l], moe_w_gate, moe_w_up, moe_w_down, l)
    return xf.reshape(b, t, d)
```

```python
import functools

import jax
import jax.numpy as jnp
from jax import lax
from jax.experimental import pallas as pl
from jax.experimental.pallas import tpu as pltpu

HEAD_DIM = 128
HGRN_CHUNK = 64
HGRN_SUB = 8
HGRN_UNROLL = 8
MOBA_BLOCK = 256
MOBA_TOP_K = 3
ROPE_THETA = 10000.0
N_GROUPS = 4
EXPERTS_PER_GROUP = 8
N_EXPERTS = N_GROUPS * EXPERTS_PER_GROUP
TOP_K_IN_GROUP = 2
RMS_EPS = 1e-6
ROUTER_LANES = 128
EXPERT_ROWS = 256
ROW_SPREAD = 97
COMBINE_ROWS = 256
DMA_THREADS = 2
ROW_DMA_PRIORITY = 1
ROPE_ROW_PARTS = 4

V7X_VMEM_LIMIT = 56 * 1024 * 1024

F32 = jnp.float32
BF16 = jnp.bfloat16
HIGHEST = lax.Precision.HIGHEST
NT_DIMS = (((1,), (1,)), ((), ()))


def _params(*sem):
    return pltpu.CompilerParams(dimension_semantics=sem, vmem_limit_bytes=V7X_VMEM_LIMIT)


def _rms_rows_to(x_ref, g_ref, out_ref, rows, chunk):
    gain = g_ref[...]

    def body(c, carry):
        sl = pl.ds(pl.multiple_of(c * chunk, chunk), chunk)
        x = x_ref[sl, :]
        ms = jnp.mean(x * x, axis=-1, keepdims=True)
        out_ref[sl, :] = (x * lax.rsqrt(ms + RMS_EPS) * gain).astype(out_ref.dtype)
        return carry

    lax.fori_loop(0, rows // chunk, body, 0)


def _head_norm_rope(y, hg, cos, sin):
    ms = jnp.mean(y * y, axis=-1, keepdims=True)
    yn = y * lax.rsqrt(ms + RMS_EPS) * hg
    return yn * cos + pltpu.roll(yn, HEAD_DIM // 2, axis=1) * sin


def _rms_matmul_kernel(*refs, tm, tn, n_rope_tiles):
    if n_rope_tiles:
        x_ref, g_ref, w_ref, hg_ref, cos_ref, sin_ref, o_ref, hn_ref = refs
    else:
        x_ref, g_ref, w_ref, o_ref, hn_ref = refs
    j = pl.program_id(1)

    @pl.when(j == 0)
    def _():
        _rms_rows_to(x_ref, g_ref, hn_ref, tm, 64)

    def plain():
        o_ref[...] = jnp.dot(hn_ref[...], w_ref[...],
                             preferred_element_type=F32).astype(o_ref.dtype)

    if not n_rope_tiles:
        plain()
        return

    @pl.when(j < n_rope_tiles)
    def _():
        w = w_ref[...]
        hg = hg_ref[...]
        part = tm // ROPE_ROW_PARTS
        for p in range(ROPE_ROW_PARTS):
            rows = slice(p * part, (p + 1) * part)
            y = jnp.dot(hn_ref[rows, :], w, preferred_element_type=F32)
            for h in range(tn // HEAD_DIM):
                sl = slice(h * HEAD_DIM, (h + 1) * HEAD_DIM)
                o_ref[rows, sl] = _head_norm_rope(y[:, sl], hg, cos_ref[rows, :],
                                                  sin_ref[rows, :]).astype(o_ref.dtype)

    pl.when(j >= n_rope_tiles)(plain)


def rms_matmul(x, gain, w, layer, out_dtype, *, rope=None, seq_len=None, tm=1024, tn=1024):
    n, d = x.shape
    e = w.shape[2]
    tm = min(tm, n)
    tn = min(tn, e)
    in_specs = [
        pl.BlockSpec((tm, d), lambda i, j: (i, 0)),
        pl.BlockSpec((1, d), lambda i, j: (0, 0)),
        pl.BlockSpec((None, d, tn), lambda i, j: (layer, 0, j)),
    ]
    args = [x, gain.reshape(1, d), w]
    n_rope_tiles = 0
    if rope is not None:
        hg, cos, sin, n_rope_cols = rope
        tm = min(tm, seq_len)
        tn = min(tn, n_rope_cols)
        in_specs[0] = pl.BlockSpec((tm, d), lambda i, j: (i, 0))
        in_specs[2] = pl.BlockSpec((None, d, tn), lambda i, j: (layer, 0, j))
        n_rope_tiles = n_rope_cols // tn
        pos_tiles = seq_len // tm
        in_specs += [
            pl.BlockSpec((1, HEAD_DIM), lambda i, j: (0, 0)),
            pl.BlockSpec((tm, HEAD_DIM), lambda i, j: (i % pos_tiles, 0)),
            pl.BlockSpec((tm, HEAD_DIM), lambda i, j: (i % pos_tiles, 0)),
        ]
        args += [hg.reshape(1, HEAD_DIM), cos, sin]
    return pl.pallas_call(
        functools.partial(_rms_matmul_kernel, tm=tm, tn=tn, n_rope_tiles=n_rope_tiles),
        grid=(n // tm, e // tn),
        in_specs=in_specs,
        out_specs=pl.BlockSpec((tm, tn), lambda i, j: (i, j)),
        out_shape=jax.ShapeDtypeStruct((n, e), out_dtype),
        scratch_shapes=[pltpu.VMEM((tm, d), BF16)],
        compiler_params=_params("arbitrary", "arbitrary"),
        name="rms_matmul",
    )(*args)


def _matmul_res_kernel(a_ref, w_ref, r_ref, o_ref):
    o_ref[...] = r_ref[...] + jnp.dot(a_ref[...], w_ref[...],
                                      preferred_element_type=F32)


def matmul_residual(a, w, layer, res, *, tm=1024, tn=1024):
    n, d = a.shape
    e = w.shape[2]
    tm = min(tm, n)
    tn = min(tn, e)
    return pl.pallas_call(
        _matmul_res_kernel,
        grid=(n // tm, e // tn),
        in_specs=[
            pl.BlockSpec((tm, d), lambda i, j: (i, 0)),
            pl.BlockSpec((None, d, tn), lambda i, j: (layer, 0, j)),
            pl.BlockSpec((tm, tn), lambda i, j: (i, j)),
        ],
        out_specs=pl.BlockSpec((tm, tn), lambda i, j: (i, j)),
        out_shape=jax.ShapeDtypeStruct((n, e), F32),
        compiler_params=_params("arbitrary", "arbitrary"),
        name="matmul_residual",
    )(a, w, res)


def _hgrn_kernel(zq_ref, zf_ref, zi_ref, zg_ref, lb_ref, og_ref, o_ref,
                 st_ref, g_scr, h_scr, a_scr, u_scr, *, n_groups):
    C = HGRN_CHUNK
    SB = HGRN_SUB
    NB = C // SB

    @pl.when(pl.program_id(2) == 0)
    def _():
        st_ref[...] = jnp.zeros_like(st_ref)

    lb = lb_ref[...]
    og = og_ref[...]
    row_c = lax.broadcasted_iota(jnp.int32, (C, C), 0)
    col_c = lax.broadcasted_iota(jnp.int32, (C, C), 1)
    causal = row_c >= col_c
    tril = causal.astype(F32)
    lane_s = lax.broadcasted_iota(jnp.int32, (SB, C), 1)
    zero_blk = jnp.zeros((SB, HEAD_DIM), F32)
    halves = [h for h in (32, 16, 8) if SB <= h < C]
    level_mask = {half: ((row_c // (2 * half) == col_c // (2 * half))
                         & ((row_c // half) % 2 == 1) & ((col_c // half) % 2 == 0))
                  for half in halves}

    def blk(x, b):
        return x[b * SB:(b + 1) * SB, :]

    def gates(rows, u):
        f = lb + (1.0 - lb) * jax.nn.sigmoid(zf_ref[0, rows, :])
        G2 = jnp.dot(tril, jnp.log2(f), precision=HIGHEST, preferred_element_type=F32)
        g_scr[u] = G2
        h_scr[u] = G2 - jnp.log2(1.0 - f)

    def scores(rows, u):
        q = zq_ref[0, rows, :]
        G2 = g_scr[u]
        H2 = h_scr[u]
        A = jnp.zeros((C, C), F32)
        for half in halves:
            per_half = half // SB
            qa, kb = [], []
            for b in range(NB):
                grp = b // per_half
                if grp % 2 == 1:
                    r = grp * half - 1
                    qa.append(blk(q, b) * jnp.exp2(blk(G2, b) - g_scr[u, r:r + 1, :]))
                    kb.append(zero_blk)
                else:
                    r = (grp + 1) * half - 1
                    kb.append(jnp.exp2(g_scr[u, r:r + 1, :] - blk(H2, b)))
                    qa.append(zero_blk)
            sq = lax.dot_general(jnp.concatenate(qa, axis=0).astype(BF16),
                                 jnp.concatenate(kb, axis=0).astype(BF16), NT_DIMS,
                                 preferred_element_type=F32)
            A = A + jnp.where(level_mask[half], sq, 0.0)
        a_parts = []
        for b in range(NB):
            Gt = blk(G2, b)
            qt = blk(q, b)
            a_b = jnp.zeros((SB, C), F32)
            for s in range(b * SB, (b + 1) * SB):
                e = jnp.exp2(Gt - h_scr[u, s:s + 1, :])
                a_b = jnp.where(lane_s == s, jnp.sum(qt * e, axis=-1, keepdims=True), a_b)
            a_parts.append(a_b)
        A = jnp.where(causal, A + jnp.concatenate(a_parts, axis=0), 0.0)
        a_scr[u] = A.astype(BF16)

    def state_input(rows, u):
        G2_end = g_scr[u, C - 1:C, :]
        kd = jnp.exp2(G2_end - h_scr[u])
        u_scr[u] = jnp.dot(zi_ref[0, rows, :].T.astype(BF16), kd.astype(BF16),
                           preferred_element_type=F32)

    def output(rows, u):
        G2 = g_scr[u]
        st = st_ref[...]
        qg = (zq_ref[0, rows, :] * jnp.exp2(G2)).astype(BF16)
        o = (lax.dot_general(qg, st.astype(BF16), NT_DIMS, preferred_element_type=F32)
             + jnp.dot(a_scr[u], zi_ref[0, rows, :].astype(BF16), preferred_element_type=F32))
        st_ref[...] = jnp.exp2(G2[C - 1:C, :]) * st + u_scr[u]
        ms = jnp.mean(o * o, axis=-1, keepdims=True)
        zg = zg_ref[0, rows, :]
        o_ref[0, rows, :] = (o * lax.rsqrt(ms + RMS_EPS) * og * jax.nn.silu(zg)).astype(o_ref.dtype)

    def group(gi, carry):
        base = pl.multiple_of(gi * (HGRN_UNROLL * C), HGRN_UNROLL * C)
        for phase in (gates, scores, state_input, output):
            for u in range(HGRN_UNROLL):
                phase(pl.ds(base + u * C, C), u)
        return carry

    lax.fori_loop(0, n_groups, group, 0)


def hgrn2_scan(z, lb, o_gain, *, tt=4096):
    b, t, d4 = z.shape
    d = d4 // 4
    nh = d // HEAD_DIM
    tt = min(tt, t)

    def zspec(part):
        return pl.BlockSpec((1, tt, HEAD_DIM), lambda bi, h, ti: (bi, ti, part * nh + h))

    return pl.pallas_call(
        functools.partial(_hgrn_kernel, n_groups=tt // (HGRN_CHUNK * HGRN_UNROLL)),
        grid=(b, nh, t // tt),
        in_specs=[zspec(0), zspec(1), zspec(2), zspec(3),
                  pl.BlockSpec((1, HEAD_DIM), lambda bi, h, ti: (0, h)),
                  pl.BlockSpec((1, HEAD_DIM), lambda bi, h, ti: (0, 0))],
        out_specs=pl.BlockSpec((1, tt, HEAD_DIM), lambda bi, h, ti: (bi, ti, h)),
        out_shape=jax.ShapeDtypeStruct((b, t, d), BF16),
        scratch_shapes=[pltpu.VMEM((HEAD_DIM, HEAD_DIM), F32),
                        pltpu.VMEM((HGRN_UNROLL, HGRN_CHUNK, HEAD_DIM), F32),
                        pltpu.VMEM((HGRN_UNROLL, HGRN_CHUNK, HEAD_DIM), F32),
                        pltpu.VMEM((HGRN_UNROLL, HGRN_CHUNK, HGRN_CHUNK), BF16),
                        pltpu.VMEM((HGRN_UNROLL, HEAD_DIM, HEAD_DIM), F32)],
        compiler_params=_params("arbitrary", "arbitrary", "arbitrary"),
        name="hgrn2_scan",
    )(z, z, z, z, lb.reshape(1, d), o_gain.reshape(1, HEAD_DIM))


def _moba_kernel(q_ref, k_ref, v_ref, o_ref, km_ref, vt_ref, qt_ref, bias_ref, m_ref, l_ref, acc_ref,
                 *, n_blocks, heads):
    BLK = MOBA_BLOCK
    qi = pl.program_id(2)
    exp2_scale = HEAD_DIM ** -0.5 * 1.4426950408889634

    def head(h):
        return slice(h * HEAD_DIM, (h + 1) * HEAD_DIM)

    eye = (lax.broadcasted_iota(jnp.int32, (HEAD_DIM, HEAD_DIM), 0)
           == lax.broadcasted_iota(jnp.int32, (HEAD_DIM, HEAD_DIM), 1)).astype(BF16)

    @pl.when(qi == 0)
    def _():
        for h in range(heads):
            for j in range(n_blocks):
                rows = slice(j * BLK, (j + 1) * BLK)
                kb = k_ref[0, rows, head(h)].astype(F32)
                km_ref[h, j:j + 1, :] = jnp.mean(kb, axis=0, keepdims=True)
                vt_ref[h, :, rows] = lax.dot_general(eye, v_ref[0, rows, head(h)], NT_DIMS,
                                                     preferred_element_type=F32).astype(BF16)

    blk_id = lax.broadcasted_iota(jnp.int32, (n_blocks, BLK), 0)
    past = blk_id < qi
    key_i = lax.broadcasted_iota(jnp.int32, (BLK, BLK), 0)
    qry_i = lax.broadcasted_iota(jnp.int32, (BLK, BLK), 1)
    own = pl.ds(pl.multiple_of(qi * BLK, BLK), BLK)

    def scores(rows):
        return [jnp.dot(k_ref[0, rows, head(h)], qt_ref[h], preferred_element_type=F32)
                for h in range(heads)]

    for h in range(heads):
        qt_ref[h] = lax.dot_general(eye, q_ref[0, :, head(h)], NT_DIMS,
                                    preferred_element_type=F32).astype(BF16)
    for h in range(heads):
        gate = jnp.dot(km_ref[h], qt_ref[h].astype(F32), precision=HIGHEST,
                       preferred_element_type=F32)
        gm = jnp.where(past, gate, -jnp.inf)
        cnt = jnp.zeros((n_blocks, BLK), jnp.int32)
        for m in range(n_blocks):
            gmm = gm[m:m + 1, :]
            ahead = (gmm > gm) | ((gmm == gm) & (blk_id > m))
            cnt = cnt + ahead.astype(jnp.int32)
        bias_ref[h] = jnp.where(past & (cnt < MOBA_TOP_K), 0.0, -jnp.inf).astype(F32)

    probs = []
    for h, s in enumerate(scores(own)):
        s = jnp.where(key_i <= qry_i, s, -jnp.inf)
        m0 = jnp.max(s, axis=0, keepdims=True)
        p = jnp.exp2((s - m0) * exp2_scale)
        m_ref[h] = m0
        l_ref[h] = jnp.sum(p, axis=0, keepdims=True)
        probs.append(p.astype(BF16))
    for h in range(heads):
        acc_ref[h] = jnp.dot(vt_ref[h, :, own], probs[h], preferred_element_type=F32)

    def past_blocks(js):
        rows = [slice(j * BLK, (j + 1) * BLK) for j in js]
        s_all = [scores(r) for r in rows]
        for j, r, s_heads in zip(js, rows, s_all):
            probs, alphas = [], []
            for h, sj in enumerate(s_heads):
                sj = sj + bias_ref[h, j:j + 1, :]
                m_old = m_ref[h]
                m_new = jnp.maximum(m_old, jnp.max(sj, axis=0, keepdims=True))
                alpha = jnp.exp2((m_old - m_new) * exp2_scale)
                pj = jnp.exp2((sj - m_new) * exp2_scale)
                m_ref[h] = m_new
                l_ref[h] = alpha * l_ref[h] + jnp.sum(pj, axis=0, keepdims=True)
                probs.append(pj.astype(BF16))
                alphas.append(alpha)
            for h in range(heads):
                acc_ref[h] = alphas[h] * acc_ref[h] + jnp.dot(vt_ref[h, :, r], probs[h],
                                                              preferred_element_type=F32)

    for j in range(0, n_blocks - 1, 2):
        if j + 1 < n_blocks - 1:
            @pl.when(j + 1 < qi)
            def _(j=j):
                past_blocks((j, j + 1))

        @pl.when(qi == j + 1)
        def _(j=j):
            past_blocks((j,))

    for h in range(heads):
        o_ref[0, :, head(h)] = (acc_ref[h] / l_ref[h]).T.astype(o_ref.dtype)


def moba_attention(q, kv, *, heads=4):
    b, t, d = q.shape
    nh = d // HEAD_DIM
    heads = min(heads, nh)
    ng = nh // heads
    nb = t // MOBA_BLOCK
    w = heads * HEAD_DIM
    return pl.pallas_call(
        functools.partial(_moba_kernel, n_blocks=nb, heads=heads),
        grid=(b, ng, nb),
        in_specs=[pl.BlockSpec((1, MOBA_BLOCK, w), lambda bi, g, qi: (bi, qi, g)),
                  pl.BlockSpec((1, t, w), lambda bi, g, qi: (bi, 0, g)),
                  pl.BlockSpec((1, t, w), lambda bi, g, qi: (bi, 0, ng + g))],
        out_specs=pl.BlockSpec((1, MOBA_BLOCK, w), lambda bi, g, qi: (bi, qi, g)),
        out_shape=jax.ShapeDtypeStruct((b, t, d), BF16),
        scratch_shapes=[pltpu.VMEM((heads, nb, HEAD_DIM), F32),
                        pltpu.VMEM((heads, HEAD_DIM, t), BF16),
                        pltpu.VMEM((heads, HEAD_DIM, MOBA_BLOCK), BF16),
                        pltpu.VMEM((heads, nb, MOBA_BLOCK), F32),
                        pltpu.VMEM((heads, 1, MOBA_BLOCK), F32),
                        pltpu.VMEM((heads, 1, MOBA_BLOCK), F32),
                        pltpu.VMEM((heads, HEAD_DIM, MOBA_BLOCK), F32)],
        compiler_params=_params("arbitrary", "arbitrary", "arbitrary"),
        name="moba_attention",
    )(q, kv, kv)


def _router_kernel(x_ref, g_ref, wr_ref, br_ref, h_ref, ids_ref, wts_ref, *, tm):
    _rms_rows_to(x_ref, g_ref, h_ref, tm, 64)
    h = h_ref[...]
    hi = h.astype(BF16)
    lo = (h - hi.astype(F32)).astype(BF16)
    w = wr_ref[...]
    hw = jnp.dot(hi, w, preferred_element_type=F32)
    logits = (hw[:, :ROUTER_LANES] + hw[:, ROUTER_LANES:]
              + jnp.dot(lo, w[:, :ROUTER_LANES], preferred_element_type=F32)
              + br_ref[...])
    lane = lax.broadcasted_iota(jnp.int32, logits.shape, 1)
    neg = -jnp.inf
    big = ROUTER_LANES

    gl = jnp.where(lane < N_GROUPS, logits, neg)
    gmax = jnp.max(gl, axis=-1, keepdims=True)
    gsum = jnp.sum(jnp.exp(gl - gmax), axis=-1, keepdims=True)
    g_top_p = 1.0 / gsum
    g_top = jnp.min(jnp.where(gl == gmax, lane, big), axis=-1, keepdims=True)

    lo = N_GROUPS + EXPERTS_PER_GROUP * g_top
    el = jnp.where((lane >= lo) & (lane < lo + EXPERTS_PER_GROUP), logits, neg)
    emax = jnp.max(el, axis=-1, keepdims=True)
    esum = jnp.sum(jnp.exp(el - emax), axis=-1, keepdims=True)
    i1 = jnp.min(jnp.where(el == emax, lane, big), axis=-1, keepdims=True)
    el2 = jnp.where(lane == i1, neg, el)
    emax2 = jnp.max(el2, axis=-1, keepdims=True)
    i2 = jnp.min(jnp.where(el2 == emax2, lane, big), axis=-1, keepdims=True)
    p1 = 1.0 / esum
    p2 = jnp.exp(emax2 - emax) / esum
    psum = p1 + p2
    w1 = g_top_p * p1 / psum
    w2 = g_top_p * p2 / psum

    ids_ref[...] = jnp.where(lane == 0, i1 - N_GROUPS, jnp.where(lane == 1, i2 - N_GROUPS, 0))
    wts_ref[...] = jnp.where(lane == 0, w1, jnp.where(lane == 1, w2, 0.0))


def moe_router(x, gain, w_router, b_router, *, tm=512):
    n, d = x.shape
    tm = min(tm, n)
    return pl.pallas_call(
        functools.partial(_router_kernel, tm=tm),
        grid=(n // tm,),
        in_specs=[pl.BlockSpec((tm, d), lambda i: (i, 0)),
                  pl.BlockSpec((1, d), lambda i: (0, 0)),
                  pl.BlockSpec((d, 2 * ROUTER_LANES), lambda i: (0, 0)),
                  pl.BlockSpec((1, ROUTER_LANES), lambda i: (0, 0))],
        out_specs=[pl.BlockSpec((tm, d), lambda i: (i, 0)),
                   pl.BlockSpec((tm, ROUTER_LANES), lambda i: (i, 0)),
                   pl.BlockSpec((tm, ROUTER_LANES), lambda i: (i, 0))],
        out_shape=[jax.ShapeDtypeStruct((n, d), F32),
                   jax.ShapeDtypeStruct((n, ROUTER_LANES), jnp.int32),
                   jax.ShapeDtypeStruct((n, ROUTER_LANES), F32)],
        compiler_params=_params("arbitrary"),
        name="moe_router",
    )(x, gain.reshape(1, d), w_router, b_router)


def _expert_kernel(blk_e_ref, n_used_ref, seg_ref, tok0_ref, tok1_ref, tok_ref, h_hbm, wg_hbm, wu_hbm,
                   wd_hbm, o_ref, xbuf0, xbuf1, xbuf2, wgf, wuf, wdf, wgb, wub, wdb, sem_in, sem_w,
                   *, rows, layer):
    i = pl.program_id(0)
    n_used = n_used_ref[0]
    xbufs = (xbuf0, xbuf1, xbuf2)
    n_buf = len(xbufs)

    def gather(idx_ref, slot):
        for r in range(rows):
            pltpu.make_async_copy(h_hbm.at[pl.ds(idx_ref[0, 0, r], 1), :],
                                  xbufs[slot].at[pl.ds(r, 1), :],
                                  sem_in.at[slot]).start(priority=ROW_DMA_PRIORITY)

    def wait_gather(slot):
        pltpu.make_async_copy(h_hbm.at[pl.ds(0, rows), :], xbufs[slot], sem_in.at[slot]).wait()

    def weight_copies(e, slot):
        return [pltpu.make_async_copy(w_hbm.at[layer, e], buf.at[slot], sem_w.at[slot])
                for w_hbm, buf in ((wg_hbm, wgf), (wu_hbm, wuf), (wd_hbm, wdf))]

    @pl.when(i == 0)
    def _():
        gather(tok0_ref, 0)
        gather(tok1_ref, 1)
        for cp in weight_copies(blk_e_ref[0], 0):
            cp.start()

    @pl.when((i < n_used) & (seg_ref[0, i] == 1))
    def _():
        slot = seg_ref[1, i]
        for cp in weight_copies(blk_e_ref[i], slot):
            cp.wait()
        wgb[...] = wgf[slot].astype(BF16)
        wub[...] = wuf[slot].astype(BF16)
        wdb[...] = wdf[slot].astype(BF16)

        @pl.when(seg_ref[2, i] == 1)
        def _():
            for cp in weight_copies(seg_ref[3, i], 1 - slot):
                cp.start()

    def step(slot):
        wait_gather(slot)
        gather(tok_ref, (slot + 2) % n_buf)
        x = xbufs[slot][...].astype(BF16)
        g = jnp.dot(x, wgb[...], preferred_element_type=F32)
        u = jnp.dot(x, wub[...], preferred_element_type=F32)
        mid = (jax.nn.silu(g) * u).astype(BF16)
        o_ref[...] = jnp.dot(mid, wdb[...], preferred_element_type=F32)

    for slot in range(n_buf):
        @pl.when((i < n_used) & (i % n_buf == slot))
        def _(slot=slot):
            step(slot)

    @pl.when(i == n_used - 1)
    def _():
        for slot in range(n_buf):
            @pl.when(i % n_buf == slot)
            def _(slot=slot):
                wait_gather((slot + 1) % n_buf)
                wait_gather((slot + 2) % n_buf)

    @pl.when(i >= n_used)
    def _():
        o_ref[...] = jnp.zeros_like(o_ref)


def expert_mlp(h, blk_e, n_used, seg, tok, w_gate, w_up, w_down, layer):
    n, d = h.shape
    n_blk, r = tok.shape[0] - 2, tok.shape[2]
    ff = w_gate.shape[3]

    def smem_rows(index_map):
        return pl.BlockSpec((1, 1, r), index_map, memory_space=pltpu.SMEM)

    grid_spec = pltpu.PrefetchScalarGridSpec(
        num_scalar_prefetch=3,
        grid=(n_blk,),
        in_specs=[
            smem_rows(lambda i, be, nu, sg: (0, 0, 0)),
            smem_rows(lambda i, be, nu, sg: (1, 0, 0)),
            smem_rows(lambda i, be, nu, sg: (i + 2, 0, 0)),
            pl.BlockSpec(memory_space=pl.ANY),
            pl.BlockSpec(memory_space=pl.ANY),
            pl.BlockSpec(memory_space=pl.ANY),
            pl.BlockSpec(memory_space=pl.ANY),
        ],
        out_specs=pl.BlockSpec((r, d), lambda i, be, nu, sg: (i, 0)),
        scratch_shapes=[
            pltpu.VMEM((r, d), F32), pltpu.VMEM((r, d), F32), pltpu.VMEM((r, d), F32),
            pltpu.VMEM((2, d, ff), F32), pltpu.VMEM((2, d, ff), F32), pltpu.VMEM((2, ff, d), F32),
            pltpu.VMEM((d, ff), BF16),
            pltpu.VMEM((d, ff), BF16),
            pltpu.VMEM((ff, d), BF16),
            pltpu.SemaphoreType.DMA((3,)),
            pltpu.SemaphoreType.DMA((2,)),
        ],
    )
    return pl.pallas_call(
        functools.partial(_expert_kernel, rows=r, layer=layer),
        grid_spec=grid_spec,
        out_shape=jax.ShapeDtypeStruct((n_blk * r, d), F32),
        compiler_params=_params("arbitrary"),
        name="expert_mlp",
    )(blk_e, n_used, seg, tok, tok, tok, h, w_gate, w_up, w_down)


def _combine_kernel(row0_ref, row_ref, x_ref, w_ref, y_hbm, o_ref, ybuf0, ybuf1, sem, *, tm, n_tiles):
    i = pl.program_id(0)
    ybufs = (ybuf0, ybuf1)

    def gather(idx_ref, slot):
        for r in range(TOP_K_IN_GROUP * tm):
            pltpu.make_async_copy(y_hbm.at[pl.ds(idx_ref[0, 0, r], 1), :],
                                  ybufs[slot].at[pl.ds(r, 1), :],
                                  sem.at[slot]).start(priority=r % DMA_THREADS)

    def wait_gather(slot):
        pltpu.make_async_copy(y_hbm.at[pl.ds(0, TOP_K_IN_GROUP * tm), :], ybufs[slot],
                              sem.at[slot]).wait()

    @pl.when(i == 0)
    def _():
        gather(row0_ref, 0)

    def step(slot):
        wait_gather(slot)
        gather(row_ref, 1 - slot)
        w = w_ref[...]
        o_ref[...] = x_ref[...] + (w[:, 0:1] * ybufs[slot][0:tm, :]
                                   + w[:, 1:2] * ybufs[slot][tm:2 * tm, :])

    for slot in range(2):
        @pl.when(i % 2 == slot)
        def _(slot=slot):
            step(slot)

    @pl.when(i == n_tiles - 1)
    def _():
        wait_gather(n_tiles % 2)


def moe_combine(x, y, rows, wts, *, tm=COMBINE_ROWS):
    n, d = x.shape
    nt = n // tm

    def smem_rows(index_map):
        return pl.BlockSpec((1, 1, TOP_K_IN_GROUP * tm), index_map, memory_space=pltpu.SMEM)

    return pl.pallas_call(
        functools.partial(_combine_kernel, tm=tm, n_tiles=nt),
        grid=(nt,),
        in_specs=[smem_rows(lambda i: (0, 0, 0)),
                  smem_rows(lambda i: (jnp.minimum(i + 1, nt - 1), 0, 0)),
                  pl.BlockSpec((tm, d), lambda i: (i, 0)),
                  pl.BlockSpec((tm, ROUTER_LANES), lambda i: (i, 0)),
                  pl.BlockSpec(memory_space=pl.ANY)],
        out_specs=pl.BlockSpec((tm, d), lambda i: (i, 0)),
        out_shape=jax.ShapeDtypeStruct((n, d), F32),
        scratch_shapes=[pltpu.VMEM((TOP_K_IN_GROUP * tm, d), F32),
                        pltpu.VMEM((TOP_K_IN_GROUP * tm, d), F32),
                        pltpu.SemaphoreType.DMA((2,))],
        compiler_params=_params("arbitrary"),
        name="moe_combine",
    )(rows, rows, x, wts, y)


def _expert_row_index(expert_id, rows_per_block, combine_rows):
    n, k = expert_id.shape
    a_total = n * k
    r = rows_per_block
    flat_e = expert_id.reshape(-1)
    order = jnp.argsort(flat_e).astype(jnp.int32)
    rank = jnp.argsort(order).astype(jnp.int32)
    experts = jnp.arange(N_EXPERTS, dtype=jnp.int32)
    onehot = experts[:, None] == flat_e[None, :]
    counts = jnp.sum(onehot.astype(jnp.int32), axis=1)
    padded = (counts + r - 1) // r * r
    pad_end = jnp.cumsum(padded)
    pad_start = pad_end - padded
    start = jnp.cumsum(counts) - counts
    n_blk = (a_total + N_EXPERTS * (r - 1) + r - 1) // r
    n_used = (pad_end[-1] // r).astype(jnp.int32)
    blk = jnp.arange(n_blk, dtype=jnp.int32)
    blk_row0 = blk * r
    blk_e = jnp.minimum(jnp.sum((pad_end[None, :] <= blk_row0[:, None]).astype(jnp.int32), axis=1),
                        N_EXPERTS - 1).astype(jnp.int32)
    in_e = blk_row0 - pad_start[blk_e]
    nvalid = jnp.clip(counts[blk_e] - in_e, 0, r)
    spread_inv = pow(ROW_SPREAD, -1, r)
    lane = (jnp.arange(r, dtype=jnp.int32) * ROW_SPREAD % r)[None, :]
    valid = (lane < nvalid[:, None]) & (blk[:, None] < n_used)
    idx = (start[blk_e] + in_e)[:, None] + lane
    src = order[jnp.clip(idx, 0, a_total - 1)]
    tok = jnp.where(valid, src // k, 0)
    tok = jnp.concatenate([tok, jnp.zeros((2, r), jnp.int32)], axis=0)
    shift = jnp.sum(jnp.where(onehot, (pad_start - start)[:, None], 0), axis=0)
    pos = rank + shift
    row = (pos // r * r + (pos % r) * spread_inv % r).astype(jnp.int32).reshape(n, k)
    row = row.reshape(n // combine_rows, combine_rows, k).transpose(0, 2, 1)
    used = blk < n_used
    first = used & ((blk == 0) | (blk_e != jnp.roll(blk_e, 1)))
    parity = (jnp.cumsum(first.astype(jnp.int32)) - 1) % 2
    next_blk = pad_end[blk_e] // r
    has_next = used & (next_blk < n_used)
    next_e = blk_e[jnp.minimum(next_blk, n_blk - 1)]
    seg = jnp.stack([first.astype(jnp.int32), parity, has_next.astype(jnp.int32), next_e])
    return (blk_e, n_used.reshape(1), seg.astype(jnp.int32), tok.reshape(n_blk + 2, 1, r),
            row.reshape(n // combine_rows, 1, k * combine_rows))


def hier_moe(x, gain, w_group, b_group, w_expert, b_expert, w_gate, w_up, w_down, layer):
    n, d = x.shape
    pad = ROUTER_LANES - N_GROUPS - N_EXPERTS
    w_router = jnp.concatenate([w_group, w_expert, jnp.zeros((d, pad), F32)], axis=1)
    w_hi = w_router.astype(BF16)
    w_lo = (w_router - w_hi.astype(F32)).astype(BF16)
    b_router = jnp.concatenate([b_group, b_expert, jnp.zeros((pad,), F32)]).reshape(1, ROUTER_LANES)
    h, ids, wts = moe_router(x, gain, jnp.concatenate([w_hi, w_lo], axis=1), b_router)
    tm = min(COMBINE_ROWS, n)
    blk_e, n_used, seg, tok, rows = _expert_row_index(ids[:, :TOP_K_IN_GROUP], EXPERT_ROWS, tm)
    y = expert_mlp(h, blk_e, n_used, seg, tok, w_gate, w_up, w_down, layer)
    return moe_combine(x, y, rows, wts, tm=tm)


def _rope_tables(t):
    inv_freq = ROPE_THETA ** (-jnp.arange(0, HEAD_DIM, 2, dtype=F32) / HEAD_DIM)
    ang = jnp.arange(t, dtype=F32)[:, None] * inv_freq[None, :]
    cos, sin = jnp.cos(ang), jnp.sin(ang)
    return jnp.concatenate([cos, cos], axis=1), jnp.concatenate([-sin, sin], axis=1)


def kernel(x, norm_mix, norm_ffn, a_w_in, a_lb_logits, a_o_gain, a_w_out, kv_norm, kv_w, k_norm,
           b_w_q, b_q_norm, b_w_out, moe_w_group, moe_b_group, moe_w_expert, moe_b_expert,
           moe_w_gate, moe_w_up, moe_w_down):
    b, t, d = x.shape
    n = b * t
    depth = norm_mix.shape[0]
    n_a = a_w_in.shape[0]
    lower_bounds = jnp.cumsum(jax.nn.softmax(a_lb_logits.astype(F32), axis=0), axis=0)
    cos, sin = _rope_tables(t)
    xf = x.reshape(n, d)
    kv = None
    for l in range(depth):
        if l < n_a:
            z = rms_matmul(xf, norm_mix[l], a_w_in.astype(BF16), l, F32)
            o = hgrn2_scan(z.reshape(b, t, 4 * d), lower_bounds[l], a_o_gain[l])
            xf = matmul_residual(o.reshape(n, d), a_w_out.astype(BF16), l, xf)
        else:
            if l == n_a:
                kv = rms_matmul(xf, kv_norm, kv_w[None].astype(BF16), 0, BF16,
                                rope=(k_norm, cos, sin, d), seq_len=t)
            lb_ = l - n_a
            q = rms_matmul(xf, norm_mix[l], b_w_q.astype(BF16), lb_, BF16,
                           rope=(b_q_norm[lb_], cos, sin, d), seq_len=t)
            o = moba_attention(q.reshape(b, t, d), kv.reshape(b, t, 2 * d))
            xf = matmul_residual(o.reshape(n, d), b_w_out.astype(BF16), lb_, xf)
        xf = hier_moe(xf, norm_ffn[l], moe_w_group[l], moe_b_group[l], moe_w_expert[l],
                      moe_b_expert[l], moe_w_gate, moe_w_up, moe_w_down, l)
    return xf.reshape(b, t, d)
```

```python
import functools

import jax
import jax.numpy as jnp
from jax import lax
from jax.experimental import pallas as pl
from jax.experimental.pallas import tpu as pltpu

HEAD_DIM = 128
HGRN_CHUNK = 64
HGRN_SUB = 8
HGRN_UNROLL = 8
MOBA_BLOCK = 256
MOBA_TOP_K = 3
ROPE_THETA = 10000.0
N_GROUPS = 4
EXPERTS_PER_GROUP = 8
N_EXPERTS = N_GROUPS * EXPERTS_PER_GROUP
TOP_K_IN_GROUP = 2
RMS_EPS = 1e-6
ROUTER_LANES = 128
EXPERT_ROWS = 256
ROW_SPREAD = 97
COMBINE_ROWS = 256
DMA_THREADS = 2
ROW_DMA_PRIORITY = 1
ROPE_ROW_PARTS = 4

V7X_VMEM_LIMIT = 56 * 1024 * 1024

F32 = jnp.float32
BF16 = jnp.bfloat16
HIGHEST = lax.Precision.HIGHEST
NT_DIMS = (((1,), (1,)), ((), ()))


def _params(*sem):
    return pltpu.CompilerParams(dimension_semantics=sem, vmem_limit_bytes=V7X_VMEM_LIMIT)


def _rms_rows_to(x_ref, g_ref, out_ref, rows, chunk):
    gain = g_ref[...]

    def body(c, carry):
        sl = pl.ds(pl.multiple_of(c * chunk, chunk), chunk)
        x = x_ref[sl, :]
        ms = jnp.mean(x * x, axis=-1, keepdims=True)
        out_ref[sl, :] = (x * lax.rsqrt(ms + RMS_EPS) * gain).astype(out_ref.dtype)
        return carry

    lax.fori_loop(0, rows // chunk, body, 0)


def _head_norm_rope(y, hg, cos, sin):
    ms = jnp.mean(y * y, axis=-1, keepdims=True)
    yn = y * lax.rsqrt(ms + RMS_EPS) * hg
    return yn * cos + pltpu.roll(yn, HEAD_DIM // 2, axis=1) * sin


def _rms_matmul_kernel(*refs, tm, tn, n_rope_tiles):
    if n_rope_tiles:
        x_ref, g_ref, w_ref, hg_ref, cos_ref, sin_ref, o_ref, hn_ref = refs
    else:
        x_ref, g_ref, w_ref, o_ref, hn_ref = refs
    j = pl.program_id(1)

    @pl.when(j == 0)
    def _():
        _rms_rows_to(x_ref, g_ref, hn_ref, tm, 64)

    def plain():
        o_ref[...] = jnp.dot(hn_ref[...], w_ref[...],
                             preferred_element_type=F32).astype(o_ref.dtype)

    if not n_rope_tiles:
        plain()
        return

    @pl.when(j < n_rope_tiles)
    def _():
        w = w_ref[...]
        hg = hg_ref[...]
        part = tm // ROPE_ROW_PARTS
        for p in range(ROPE_ROW_PARTS):
            rows = slice(p * part, (p + 1) * part)
            y = jnp.dot(hn_ref[rows, :], w, preferred_element_type=F32)
            for h in range(tn // HEAD_DIM):
                sl = slice(h * HEAD_DIM, (h + 1) * HEAD_DIM)
                o_ref[rows, sl] = _head_norm_rope(y[:, sl], hg, cos_ref[rows, :],
                                                  sin_ref[rows, :]).astype(o_ref.dtype)

    pl.when(j >= n_rope_tiles)(plain)


def rms_matmul(x, gain, w, layer, out_dtype, *, rope=None, seq_len=None, tm=1024, tn=1024):
    n, d = x.shape
    e = w.shape[2]
    tm = min(tm, n)
    tn = min(tn, e)
    in_specs = [
        pl.BlockSpec((tm, d), lambda i, j: (i, 0)),
        pl.BlockSpec((1, d), lambda i, j: (0, 0)),
        pl.BlockSpec((None, d, tn), lambda i, j: (layer, 0, j)),
    ]
    args = [x, gain.reshape(1, d), w]
    n_rope_tiles = 0
    if rope is not None:
        hg, cos, sin, n_rope_cols = rope
        tm = min(tm, seq_len)
        tn = min(tn, n_rope_cols)
        in_specs[0] = pl.BlockSpec((tm, d), lambda i, j: (i, 0))
        in_specs[2] = pl.BlockSpec((None, d, tn), lambda i, j: (layer, 0, j))
        n_rope_tiles = n_rope_cols // tn
        pos_tiles = seq_len // tm
        in_specs += [
            pl.BlockSpec((1, HEAD_DIM), lambda i, j: (0, 0)),
            pl.BlockSpec((tm, HEAD_DIM), lambda i, j: (i % pos_tiles, 0)),
            pl.BlockSpec((tm, HEAD_DIM), lambda i, j: (i % pos_tiles, 0)),
        ]
        args += [hg.reshape(1, HEAD_DIM), cos, sin]
    return pl.pallas_call(
        functools.partial(_rms_matmul_kernel, tm=tm, tn=tn, n_rope_tiles=n_rope_tiles),
        grid=(n // tm, e // tn),
        in_specs=in_specs,
        out_specs=pl.BlockSpec((tm, tn), lambda i, j: (i, j)),
        out_shape=jax.ShapeDtypeStruct((n, e), out_dtype),
        scratch_shapes=[pltpu.VMEM((tm, d), BF16)],
        compiler_params=_params("arbitrary", "arbitrary"),
        name="rms_matmul",
    )(*args)


def _matmul_res_kernel(a_ref, w_ref, r_ref, o_ref):
    o_ref[...] = r_ref[...] + jnp.dot(a_ref[...], w_ref[...],
                                      preferred_element_type=F32)


def matmul_residual(a, w, layer, res, *, tm=1024, tn=1024):
    n, d = a.shape
    e = w.shape[2]
    tm = min(tm, n)
    tn = min(tn, e)
    return pl.pallas_call(
        _matmul_res_kernel,
        grid=(n // tm, e // tn),
        in_specs=[
            pl.BlockSpec((tm, d), lambda i, j: (i, 0)),
            pl.BlockSpec((None, d, tn), lambda i, j: (layer, 0, j)),
            pl.BlockSpec((tm, tn), lambda i, j: (i, j)),
        ],
        out_specs=pl.BlockSpec((tm, tn), lambda i, j: (i, j)),
        out_shape=jax.ShapeDtypeStruct((n, e), F32),
        compiler_params=_params("arbitrary", "arbitrary"),
        name="matmul_residual",
    )(a, w, res)


def _hgrn_kernel(zq_ref, zf_ref, zi_ref, zg_ref, lb_ref, og_ref, o_ref,
                 st_ref, g_scr, h_scr, a_scr, u_scr, *, n_groups):
    C = HGRN_CHUNK
    SB = HGRN_SUB
    NB = C // SB

    @pl.when(pl.program_id(2) == 0)
    def _():
        st_ref[...] = jnp.zeros_like(st_ref)

    lb = lb_ref[...]
    og = og_ref[...]
    row_c = lax.broadcasted_iota(jnp.int32, (C, C), 0)
    col_c = lax.broadcasted_iota(jnp.int32, (C, C), 1)
    causal = row_c >= col_c
    tril = causal.astype(F32)
    lane_s = lax.broadcasted_iota(jnp.int32, (SB, C), 1)
    zero_blk = jnp.zeros((SB, HEAD_DIM), F32)
    halves = [h for h in (32, 16, 8) if SB <= h < C]
    level_mask = {half: ((row_c // (2 * half) == col_c // (2 * half))
                         & ((row_c // half) % 2 == 1) & ((col_c // half) % 2 == 0))
                  for half in halves}

    def blk(x, b):
        return x[b * SB:(b + 1) * SB, :]

    def gates(rows, u):
        f = lb + (1.0 - lb) * jax.nn.sigmoid(zf_ref[0, rows, :])
        G2 = jnp.dot(tril, jnp.log2(f), precision=HIGHEST, preferred_element_type=F32)
        g_scr[u] = G2
        h_scr[u] = G2 - jnp.log2(1.0 - f)

    def scores(rows, u):
        q = zq_ref[0, rows, :]
        G2 = g_scr[u]
        H2 = h_scr[u]
        A = jnp.zeros((C, C), F32)
        for half in halves:
            per_half = half // SB
            qa, kb = [], []
            for b in range(NB):
                grp = b // per_half
                if grp % 2 == 1:
                    r = grp * half - 1
                    qa.append(blk(q, b) * jnp.exp2(blk(G2, b) - g_scr[u, r:r + 1, :]))
                    kb.append(zero_blk)
                else:
                    r = (grp + 1) * half - 1
                    kb.append(jnp.exp2(g_scr[u, r:r + 1, :] - blk(H2, b)))
                    qa.append(zero_blk)
            sq = lax.dot_general(jnp.concatenate(qa, axis=0).astype(BF16),
                                 jnp.concatenate(kb, axis=0).astype(BF16), NT_DIMS,
                                 preferred_element_type=F32)
            A = A + jnp.where(level_mask[half], sq, 0.0)
        a_parts = []
        for b in range(NB):
            Gt = blk(G2, b)
            qt = blk(q, b)
            a_b = jnp.zeros((SB, C), F32)
            for s in range(b * SB, (b + 1) * SB):
                e = jnp.exp2(Gt - h_scr[u, s:s + 1, :])
                a_b = jnp.where(lane_s == s, jnp.sum(qt * e, axis=-1, keepdims=True), a_b)
            a_parts.append(a_b)
        A = jnp.where(causal, A + jnp.concatenate(a_parts, axis=0), 0.0)
        a_scr[u] = A.astype(BF16)

    def state_input(rows, u):
        G2_end = g_scr[u, C - 1:C, :]
        kd = jnp.exp2(G2_end - h_scr[u])
        u_scr[u] = jnp.dot(zi_ref[0, rows, :].T.astype(BF16), kd.astype(BF16),
                           preferred_element_type=F32)

    def output(rows, u):
        G2 = g_scr[u]
        st = st_ref[...]
        qg = (zq_ref[0, rows, :] * jnp.exp2(G2)).astype(BF16)
        o = (lax.dot_general(qg, st.astype(BF16), NT_DIMS, preferred_element_type=F32)
             + jnp.dot(a_scr[u], zi_ref[0, rows, :].astype(BF16), preferred_element_type=F32))
        st_ref[...] = jnp.exp2(G2[C - 1:C, :]) * st + u_scr[u]
        ms = jnp.mean(o * o, axis=-1, keepdims=True)
        zg = zg_ref[0, rows, :]
        o_ref[0, rows, :] = (o * lax.rsqrt(ms + RMS_EPS) * og * jax.nn.silu(zg)).astype(o_ref.dtype)

    def group(gi, carry):
        base = pl.multiple_of(gi * (HGRN_UNROLL * C), HGRN_UNROLL * C)
        for phase in (gates, scores, state_input, output):
            for u in range(HGRN_UNROLL):
                phase(pl.ds(base + u * C, C), u)
        return carry

    lax.fori_loop(0, n_groups, group, 0)


def hgrn2_scan(z, lb, o_gain, *, tt=4096):
    b, t, d4 = z.shape
    d = d4 // 4
    nh = d // HEAD_DIM
    tt = min(tt, t)

    def zspec(part):
        return pl.BlockSpec((1, tt, HEAD_DIM), lambda bi, h, ti: (bi, ti, part * nh + h))

    return pl.pallas_call(
        functools.partial(_hgrn_kernel, n_groups=tt // (HGRN_CHUNK * HGRN_UNROLL)),
        grid=(b, nh, t // tt),
        in_specs=[zspec(0), zspec(1), zspec(2), zspec(3),
                  pl.BlockSpec((1, HEAD_DIM), lambda bi, h, ti: (0, h)),
                  pl.BlockSpec((1, HEAD_DIM), lambda bi, h, ti: (0, 0))],
        out_specs=pl.BlockSpec((1, tt, HEAD_DIM), lambda bi, h, ti: (bi, ti, h)),
        out_shape=jax.ShapeDtypeStruct((b, t, d), BF16),
        scratch_shapes=[pltpu.VMEM((HEAD_DIM, HEAD_DIM), F32),
                        pltpu.VMEM((HGRN_UNROLL, HGRN_CHUNK, HEAD_DIM), F32),
                        pltpu.VMEM((HGRN_UNROLL, HGRN_CHUNK, HEAD_DIM), F32),
                        pltpu.VMEM((HGRN_UNROLL, HGRN_CHUNK, HGRN_CHUNK), BF16),
                        pltpu.VMEM((HGRN_UNROLL, HEAD_DIM, HEAD_DIM), F32)],
        compiler_params=_params("arbitrary", "arbitrary", "arbitrary"),
        name="hgrn2_scan",
    )(z, z, z, z, lb.reshape(1, d), o_gain.reshape(1, HEAD_DIM))


def _moba_kernel(q_ref, k_ref, v_ref, o_ref, km_ref, vt_ref, qt_ref, bias_ref, m_ref, l_ref, acc_ref,
                 *, n_blocks, heads):
    BLK = MOBA_BLOCK
    qi = pl.program_id(2)
    exp2_scale = HEAD_DIM ** -0.5 * 1.4426950408889634

    def head(h):
        return slice(h * HEAD_DIM, (h + 1) * HEAD_DIM)

    eye = (lax.broadcasted_iota(jnp.int32, (HEAD_DIM, HEAD_DIM), 0)
           == lax.broadcasted_iota(jnp.int32, (HEAD_DIM, HEAD_DIM), 1)).astype(BF16)

    @pl.when(qi == 0)
    def _():
        for h in range(heads):
            for j in range(n_blocks):
                rows = slice(j * BLK, (j + 1) * BLK)
                kb = k_ref[0, rows, head(h)].astype(F32)
                km_ref[h, j:j + 1, :] = jnp.mean(kb, axis=0, keepdims=True)
                vt_ref[h, :, rows] = lax.dot_general(eye, v_ref[0, rows, head(h)], NT_DIMS,
                                                     preferred_element_type=F32).astype(BF16)

    blk_id = lax.broadcasted_iota(jnp.int32, (n_blocks, BLK), 0)
    past = blk_id < qi
    key_i = lax.broadcasted_iota(jnp.int32, (BLK, BLK), 0)
    qry_i = lax.broadcasted_iota(jnp.int32, (BLK, BLK), 1)
    own = pl.ds(pl.multiple_of(qi * BLK, BLK), BLK)

    def scores(rows):
        return [jnp.dot(k_ref[0, rows, head(h)], qt_ref[h], preferred_element_type=F32)
                for h in range(heads)]

    for h in range(heads):
        qt_ref[h] = lax.dot_general(eye, q_ref[0, :, head(h)], NT_DIMS,
                                    preferred_element_type=F32).astype(BF16)
    for h in range(heads):
        gate = jnp.dot(km_ref[h], qt_ref[h].astype(F32), precision=HIGHEST,
                       preferred_element_type=F32)
        gm = jnp.where(past, gate, -jnp.inf)
        cnt = jnp.zeros((n_blocks, BLK), jnp.int32)
        for m in range(n_blocks):
            gmm = gm[m:m + 1, :]
            ahead = (gmm > gm) | ((gmm == gm) & (blk_id > m))
            cnt = cnt + ahead.astype(jnp.int32)
        bias_ref[h] = jnp.where(past & (cnt < MOBA_TOP_K), 0.0, -jnp.inf).astype(F32)

    probs = []
    for h, s in enumerate(scores(own)):
        s = jnp.where(key_i <= qry_i, s, -jnp.inf)
        m0 = jnp.max(s, axis=0, keepdims=True)
        p = jnp.exp2((s - m0) * exp2_scale)
        m_ref[h] = m0
        l_ref[h] = jnp.sum(p, axis=0, keepdims=True)
        probs.append(p.astype(BF16))
    for h in range(heads):
        acc_ref[h] = jnp.dot(vt_ref[h, :, own], probs[h], preferred_element_type=F32)

    def past_blocks(js):
        rows = [slice(j * BLK, (j + 1) * BLK) for j in js]
        s_all = [scores(r) for r in rows]
        for j, r, s_heads in zip(js, rows, s_all):
            probs, alphas = [], []
            for h, sj in enumerate(s_heads):
                sj = sj + bias_ref[h, j:j + 1, :]
                m_old = m_ref[h]
                m_new = jnp.maximum(m_old, jnp.max(sj, axis=0, keepdims=True))
                alpha = jnp.exp2((m_old - m_new) * exp2_scale)
                pj = jnp.exp2((sj - m_new) * exp2_scale)
                m_ref[h] = m_new
                l_ref[h] = alpha * l_ref[h] + jnp.sum(pj, axis=0, keepdims=True)
                probs.append(pj.astype(BF16))
                alphas.append(alpha)
            for h in range(heads):
                acc_ref[h] = alphas[h] * acc_ref[h] + jnp.dot(vt_ref[h, :, r], probs[h],
                                                              preferred_element_type=F32)

    for j in range(0, n_blocks - 1, 2):
        if j + 1 < n_blocks - 1:
            @pl.when(j + 1 < qi)
            def _(j=j):
                past_blocks((j, j + 1))

        @pl.when(qi == j + 1)
        def _(j=j):
            past_blocks((j,))

    for h in range(heads):
        o_ref[0, :, head(h)] = (acc_ref[h] / l_ref[h]).T.astype(o_ref.dtype)


def moba_attention(q, kv, *, heads=8):
    b, t, d = q.shape
    nh = d // HEAD_DIM
    heads = min(heads, nh)
    ng = nh // heads
    nb = t // MOBA_BLOCK
    w = heads * HEAD_DIM
    return pl.pallas_call(
        functools.partial(_moba_kernel, n_blocks=nb, heads=heads),
        grid=(b, ng, nb),
        in_specs=[pl.BlockSpec((1, MOBA_BLOCK, w), lambda bi, g, qi: (bi, qi, g)),
                  pl.BlockSpec((1, t, w), lambda bi, g, qi: (bi, 0, g)),
                  pl.BlockSpec((1, t, w), lambda bi, g, qi: (bi, 0, ng + g))],
        out_specs=pl.BlockSpec((1, MOBA_BLOCK, w), lambda bi, g, qi: (bi, qi, g)),
        out_shape=jax.ShapeDtypeStruct((b, t, d), BF16),
        scratch_shapes=[pltpu.VMEM((heads, nb, HEAD_DIM), F32),
                        pltpu.VMEM((heads, HEAD_DIM, t), BF16),
                        pltpu.VMEM((heads, HEAD_DIM, MOBA_BLOCK), BF16),
                        pltpu.VMEM((heads, nb, MOBA_BLOCK), F32),
                        pltpu.VMEM((heads, 1, MOBA_BLOCK), F32),
                        pltpu.VMEM((heads, 1, MOBA_BLOCK), F32),
                        pltpu.VMEM((heads, HEAD_DIM, MOBA_BLOCK), F32)],
        compiler_params=_params("arbitrary", "arbitrary", "arbitrary"),
        name="moba_attention",
    )(q, kv, kv)


def _router_kernel(x_ref, g_ref, wr_ref, br_ref, h_ref, ids_ref, wts_ref, *, tm):
    _rms_rows_to(x_ref, g_ref, h_ref, tm, 64)
    h = h_ref[...]
    hi = h.astype(BF16)
    lo = (h - hi.astype(F32)).astype(BF16)
    w = wr_ref[...]
    hw = jnp.dot(hi, w, preferred_element_type=F32)
    logits = (hw[:, :ROUTER_LANES] + hw[:, ROUTER_LANES:]
              + jnp.dot(lo, w[:, :ROUTER_LANES], preferred_element_type=F32)
              + br_ref[...])
    lane = lax.broadcasted_iota(jnp.int32, logits.shape, 1)
    neg = -jnp.inf
    big = ROUTER_LANES

    gl = jnp.where(lane < N_GROUPS, logits, neg)
    gmax = jnp.max(gl, axis=-1, keepdims=True)
    gsum = jnp.sum(jnp.exp(gl - gmax), axis=-1, keepdims=True)
    g_top_p = 1.0 / gsum
    g_top = jnp.min(jnp.where(gl == gmax, lane, big), axis=-1, keepdims=True)

    lo = N_GROUPS + EXPERTS_PER_GROUP * g_top
    el = jnp.where((lane >= lo) & (lane < lo + EXPERTS_PER_GROUP), logits, neg)
    emax = jnp.max(el, axis=-1, keepdims=True)
    esum = jnp.sum(jnp.exp(el - emax), axis=-1, keepdims=True)
    i1 = jnp.min(jnp.where(el == emax, lane, big), axis=-1, keepdims=True)
    el2 = jnp.where(lane == i1, neg, el)
    emax2 = jnp.max(el2, axis=-1, keepdims=True)
    i2 = jnp.min(jnp.where(el2 == emax2, lane, big), axis=-1, keepdims=True)
    p1 = 1.0 / esum
    p2 = jnp.exp(emax2 - emax) / esum
    psum = p1 + p2
    w1 = g_top_p * p1 / psum
    w2 = g_top_p * p2 / psum

    ids_ref[...] = jnp.where(lane == 0, i1 - N_GROUPS, jnp.where(lane == 1, i2 - N_GROUPS, 0))
    wts_ref[...] = jnp.where(lane == 0, w1, jnp.where(lane == 1, w2, 0.0))


def moe_router(x, gain, w_router, b_router, *, tm=512):
    n, d = x.shape
    tm = min(tm, n)
    return pl.pallas_call(
        functools.partial(_router_kernel, tm=tm),
        grid=(n // tm,),
        in_specs=[pl.BlockSpec((tm, d), lambda i: (i, 0)),
                  pl.BlockSpec((1, d), lambda i: (0, 0)),
                  pl.BlockSpec((d, 2 * ROUTER_LANES), lambda i: (0, 0)),
                  pl.BlockSpec((1, ROUTER_LANES), lambda i: (0, 0))],
        out_specs=[pl.BlockSpec((tm, d), lambda i: (i, 0)),
                   pl.BlockSpec((tm, ROUTER_LANES), lambda i: (i, 0)),
                   pl.BlockSpec((tm, ROUTER_LANES), lambda i: (i, 0))],
        out_shape=[jax.ShapeDtypeStruct((n, d), F32),
                   jax.ShapeDtypeStruct((n, ROUTER_LANES), jnp.int32),
                   jax.ShapeDtypeStruct((n, ROUTER_LANES), F32)],
        compiler_params=_params("arbitrary"),
        name="moe_router",
    )(x, gain.reshape(1, d), w_router, b_router)


def _expert_kernel(blk_e_ref, n_used_ref, seg_ref, tok0_ref, tok1_ref, tok_ref, h_hbm, wg_hbm, wu_hbm,
                   wd_hbm, o_ref, xbuf0, xbuf1, xbuf2, wgf, wuf, wdf, wgb, wub, wdb, sem_in, sem_w,
                   *, rows, layer):
    i = pl.program_id(0)
    n_used = n_used_ref[0]
    xbufs = (xbuf0, xbuf1, xbuf2)
    n_buf = len(xbufs)

    def gather(idx_ref, slot):
        for r in range(rows):
            pltpu.make_async_copy(h_hbm.at[pl.ds(idx_ref[0, 0, r], 1), :],
                                  xbufs[slot].at[pl.ds(r, 1), :],
                                  sem_in.at[slot]).start(priority=ROW_DMA_PRIORITY)

    def wait_gather(slot):
        pltpu.make_async_copy(h_hbm.at[pl.ds(0, rows), :], xbufs[slot], sem_in.at[slot]).wait()

    def weight_copies(e, slot):
        return [pltpu.make_async_copy(w_hbm.at[layer, e], buf.at[slot], sem_w.at[slot])
                for w_hbm, buf in ((wg_hbm, wgf), (wu_hbm, wuf), (wd_hbm, wdf))]

    @pl.when(i == 0)
    def _():
        gather(tok0_ref, 0)
        gather(tok1_ref, 1)
        for cp in weight_copies(blk_e_ref[0], 0):
            cp.start()

    @pl.when((i < n_used) & (seg_ref[0, i] == 1))
    def _():
        slot = seg_ref[1, i]
        for cp in weight_copies(blk_e_ref[i], slot):
            cp.wait()
        wgb[...] = wgf[slot].astype(BF16)
        wub[...] = wuf[slot].astype(BF16)
        wdb[...] = wdf[slot].astype(BF16)

        @pl.when(seg_ref[2, i] == 1)
        def _():
            for cp in weight_copies(seg_ref[3, i], 1 - slot):
                cp.start()

    def step(slot):
        wait_gather(slot)
        gather(tok_ref, (slot + 2) % n_buf)
        x = xbufs[slot][...].astype(BF16)
        g = jnp.dot(x, wgb[...], preferred_element_type=F32)
        u = jnp.dot(x, wub[...], preferred_element_type=F32)
        mid = (jax.nn.silu(g) * u).astype(BF16)
        o_ref[...] = jnp.dot(mid, wdb[...], preferred_element_type=F32)

    for slot in range(n_buf):
        @pl.when((i < n_used) & (i % n_buf == slot))
        def _(slot=slot):
            step(slot)

    @pl.when(i == n_used - 1)
    def _():
        for slot in range(n_buf):
            @pl.when(i % n_buf == slot)
            def _(slot=slot):
                wait_gather((slot + 1) % n_buf)
                wait_gather((slot + 2) % n_buf)

    @pl.when(i >= n_used)
    def _():
        o_ref[...] = jnp.zeros_like(o_ref)


def expert_mlp(h, blk_e, n_used, seg, tok, w_gate, w_up, w_down, layer):
    n, d = h.shape
    n_blk, r = tok.shape[0] - 2, tok.shape[2]
    ff = w_gate.shape[3]

    def smem_rows(index_map):
        return pl.BlockSpec((1, 1, r), index_map, memory_space=pltpu.SMEM)

    grid_spec = pltpu.PrefetchScalarGridSpec(
        num_scalar_prefetch=3,
        grid=(n_blk,),
        in_specs=[
            smem_rows(lambda i, be, nu, sg: (0, 0, 0)),
            smem_rows(lambda i, be, nu, sg: (1, 0, 0)),
            smem_rows(lambda i, be, nu, sg: (i + 2, 0, 0)),
            pl.BlockSpec(memory_space=pl.ANY),
            pl.BlockSpec(memory_space=pl.ANY),
            pl.BlockSpec(memory_space=pl.ANY),
            pl.BlockSpec(memory_space=pl.ANY),
        ],
        out_specs=pl.BlockSpec((r, d), lambda i, be, nu, sg: (i, 0)),
        scratch_shapes=[
            pltpu.VMEM((r, d), F32), pltpu.VMEM((r, d), F32), pltpu.VMEM((r, d), F32),
            pltpu.VMEM((2, d, ff), F32), pltpu.VMEM((2, d, ff), F32), pltpu.VMEM((2, ff, d), F32),
            pltpu.VMEM((d, ff), BF16),
            pltpu.VMEM((d, ff), BF16),
            pltpu.VMEM((ff, d), BF16),
            pltpu.SemaphoreType.DMA((3,)),
            pltpu.SemaphoreType.DMA((2,)),
        ],
    )
    return pl.pallas_call(
        functools.partial(_expert_kernel, rows=r, layer=layer),
        grid_spec=grid_spec,
        out_shape=jax.ShapeDtypeStruct((n_blk * r, d), F32),
        compiler_params=_params("arbitrary"),
        name="expert_mlp",
    )(blk_e, n_used, seg, tok, tok, tok, h, w_gate, w_up, w_down)


def _combine_kernel(row0_ref, row_ref, x_ref, w_ref, y_hbm, o_ref, ybuf0, ybuf1, sem, *, tm, n_tiles):
    i = pl.program_id(0)
    ybufs = (ybuf0, ybuf1)

    def gather(idx_ref, slot):
        for r in range(TOP_K_IN_GROUP * tm):
            pltpu.make_async_copy(y_hbm.at[pl.ds(idx_ref[0, 0, r], 1), :],
                                  ybufs[slot].at[pl.ds(r, 1), :],
                                  sem.at[slot]).start(priority=r % DMA_THREADS)

    def wait_gather(slot):
        pltpu.make_async_copy(y_hbm.at[pl.ds(0, TOP_K_IN_GROUP * tm), :], ybufs[slot],
                              sem.at[slot]).wait()

    @pl.when(i == 0)
    def _():
        gather(row0_ref, 0)

    def step(slot):
        wait_gather(slot)
        gather(row_ref, 1 - slot)
        w = w_ref[...]
        o_ref[...] = x_ref[...] + (w[:, 0:1] * ybufs[slot][0:tm, :]
                                   + w[:, 1:2] * ybufs[slot][tm:2 * tm, :])

    for slot in range(2):
        @pl.when(i % 2 == slot)
        def _(slot=slot):
            step(slot)

    @pl.when(i == n_tiles - 1)
    def _():
        wait_gather(n_tiles % 2)


def moe_combine(x, y, rows, wts, *, tm=COMBINE_ROWS):
    n, d = x.shape
    nt = n // tm

    def smem_rows(index_map):
        return pl.BlockSpec((1, 1, TOP_K_IN_GROUP * tm), index_map, memory_space=pltpu.SMEM)

    return pl.pallas_call(
        functools.partial(_combine_kernel, tm=tm, n_tiles=nt),
        grid=(nt,),
        in_specs=[smem_rows(lambda i: (0, 0, 0)),
                  smem_rows(lambda i: (jnp.minimum(i + 1, nt - 1), 0, 0)),
                  pl.BlockSpec((tm, d), lambda i: (i, 0)),
                  pl.BlockSpec((tm, ROUTER_LANES), lambda i: (i, 0)),
                  pl.BlockSpec(memory_space=pl.ANY)],
        out_specs=pl.BlockSpec((tm, d), lambda i: (i, 0)),
        out_shape=jax.ShapeDtypeStruct((n, d), F32),
        scratch_shapes=[pltpu.VMEM((TOP_K_IN_GROUP * tm, d), F32),
                        pltpu.VMEM((TOP_K_IN_GROUP * tm, d), F32),
                        pltpu.SemaphoreType.DMA((2,))],
        compiler_params=_params("arbitrary"),
        name="moe_combine",
    )(rows, rows, x, wts, y)


def _expert_row_index(expert_id, rows_per_block, combine_rows):
    n, k = expert_id.shape
    a_total = n * k
    r = rows_per_block
    flat_e = expert_id.reshape(-1)
    order = jnp.argsort(flat_e).astype(jnp.int32)
    rank = jnp.argsort(order).astype(jnp.int32)
    experts = jnp.arange(N_EXPERTS, dtype=jnp.int32)
    onehot = experts[:, None] == flat_e[None, :]
    counts = jnp.sum(onehot.astype(jnp.int32), axis=1)
    padded = (counts + r - 1) // r * r
    pad_end = jnp.cumsum(padded)
    pad_start = pad_end - padded
    start = jnp.cumsum(counts) - counts
    n_blk = (a_total + N_EXPERTS * (r - 1) + r - 1) // r
    n_used = (pad_end[-1] // r).astype(jnp.int32)
    blk = jnp.arange(n_blk, dtype=jnp.int32)
    blk_row0 = blk * r
    blk_e = jnp.minimum(jnp.sum((pad_end[None, :] <= blk_row0[:, None]).astype(jnp.int32), axis=1),
                        N_EXPERTS - 1).astype(jnp.int32)
    in_e = blk_row0 - pad_start[blk_e]
    nvalid = jnp.clip(counts[blk_e] - in_e, 0, r)
    spread_inv = pow(ROW_SPREAD, -1, r)
    lane = (jnp.arange(r, dtype=jnp.int32) * ROW_SPREAD % r)[None, :]
    valid = (lane < nvalid[:, None]) & (blk[:, None] < n_used)
    idx = (start[blk_e] + in_e)[:, None] + lane
    src = order[jnp.clip(idx, 0, a_total - 1)]
    tok = jnp.where(valid, src // k, 0)
    tok = jnp.concatenate([tok, jnp.zeros((2, r), jnp.int32)], axis=0)
    shift = jnp.sum(jnp.where(onehot, (pad_start - start)[:, None], 0), axis=0)
    pos = rank + shift
    row = (pos // r * r + (pos % r) * spread_inv % r).astype(jnp.int32).reshape(n, k)
    row = row.reshape(n // combine_rows, combine_rows, k).transpose(0, 2, 1)
    used = blk < n_used
    first = used & ((blk == 0) | (blk_e != jnp.roll(blk_e, 1)))
    parity = (jnp.cumsum(first.astype(jnp.int32)) - 1) % 2
    next_blk = pad_end[blk_e] // r
    has_next = used & (next_blk < n_used)
    next_e = blk_e[jnp.minimum(next_blk, n_blk - 1)]
    seg = jnp.stack([first.astype(jnp.int32), parity, has_next.astype(jnp.int32), next_e])
    return (blk_e, n_used.reshape(1), seg.astype(jnp.int32), tok.reshape(n_blk + 2, 1, r),
            row.reshape(n // combine_rows, 1, k * combine_rows))


def hier_moe(x, gain, w_group, b_group, w_expert, b_expert, w_gate, w_up, w_down, layer):
    n, d = x.shape
    pad = ROUTER_LANES - N_GROUPS - N_EXPERTS
    w_router = jnp.concatenate([w_group, w_expert, jnp.zeros((d, pad), F32)], axis=1)
    w_hi = w_router.astype(BF16)
    w_lo = (w_router - w_hi.astype(F32)).astype(BF16)
    b_router = jnp.concatenate([b_group, b_expert, jnp.zeros((pad,), F32)]).reshape(1, ROUTER_LANES)
    h, ids, wts = moe_router(x, gain, jnp.concatenate([w_hi, w_lo], axis=1), b_router)
    tm = min(COMBINE_ROWS, n)
    blk_e, n_used, seg, tok, rows = _expert_row_index(ids[:, :TOP_K_IN_GROUP], EXPERT_ROWS, tm)
    y = expert_mlp(h, blk_e, n_used, seg, tok, w_gate, w_up, w_down, layer)
    return moe_combine(x, y, rows, wts, tm=tm)


def _rope_tables(t):
    inv_freq = ROPE_THETA ** (-jnp.arange(0, HEAD_DIM, 2, dtype=F32) / HEAD_DIM)
    ang = jnp.arange(t, dtype=F32)[:, None] * inv_freq[None, :]
    cos, sin = jnp.cos(ang), jnp.sin(ang)
    return jnp.concatenate([cos, cos], axis=1), jnp.concatenate([-sin, sin], axis=1)


def kernel(x, norm_mix, norm_ffn, a_w_in, a_lb_logits, a_o_gain, a_w_out, kv_norm, kv_w, k_norm,
           b_w_q, b_q_norm, b_w_out, moe_w_group, moe_b_group, moe_w_expert, moe_b_expert,
           moe_w_gate, moe_w_up, moe_w_down):
    b, t, d = x.shape
    n = b * t
    depth = norm_mix.shape[0]
    n_a = a_w_in.shape[0]
    lower_bounds = jnp.cumsum(jax.nn.softmax(a_lb_logits.astype(F32), axis=0), axis=0)
    cos, sin = _rope_tables(t)
    xf = x.reshape(n, d)
    kv = None
    for l in range(depth):
        if l < n_a:
            z = rms_matmul(xf, norm_mix[l], a_w_in.astype(BF16), l, F32)
            o = hgrn2_scan(z.reshape(b, t, 4 * d), lower_bounds[l], a_o_gain[l])
            xf = matmul_residual(o.reshape(n, d), a_w_out.astype(BF16), l, xf)
        else:
            if l == n_a:
                kv = rms_matmul(xf, kv_norm, kv_w[None].astype(BF16), 0, BF16,
                                rope=(k_norm, cos, sin, d), seq_len=t)
            lb_ = l - n_a
            q = rms_matmul(xf, norm_mix[l], b_w_q.astype(BF16), lb_, BF16,
                           rope=(b_q_norm[lb_], cos, sin, d), seq_len=t)
            o = moba_attention(q.reshape(b, t, d), kv.reshape(b, t, 2 * d))
            xf = matmul_residual(o.reshape(n, d), b_w_out.astype(BF16), lb_, xf)
        xf = hier_moe(xf, norm_ffn[l], moe_w_group[l], moe_b_group[l], moe_w_expert[l],
                      moe_b_expert[l], moe_w_gate, moe_w_up, moe_w_down, l)
    return xf.reshape(b, t, d)
```
